```python
import math, functools
import jax, jax.numpy as jnp
from jax import lax
import numpy as np

D_MODEL = 1024
BATCH = 2
SEQ = 8192
DEPTH = 2
DEC_BATCH = 128
DEC_SEQ = 4
PAST_LEN = 16384
PAGE_SIZE = 128

NSA_HEADS = 4
NSA_DH = 64
CMP_BLOCK = 32
SEL_BLOCK = 64
N_SEL = 16
WINDOW = 512
MLA_HEADS = 4
MLA_Q_RANK = 256
MLA_KV_RANK = 64
MLA_NOPE = 64
MLA_ROPE = 32
MLA_DV = 64
ROPE_BASE = 10000.0
FOX_HEADS = 8
FOX_KV_HEADS = 1
FOX_DH = 64
FOX_F_BIAS = 3.0
REL_BUCKETS = 32
REL_MAX_DIST = 128
D_FF = 2816
CONV_W = 3
Q_BLOCK = 128
EPS = 1e-6
NEG = -1e30

NSA_W = NSA_HEADS * NSA_DH
MLA_W = MLA_HEADS * MLA_DV
FOX_W = FOX_HEADS * FOX_DH
MIX_W = NSA_W + MLA_W + FOX_W
NSA_SCALE = NSA_DH ** -0.5
MLA_SCALE = (MLA_NOPE + MLA_ROPE) ** -0.5
FOX_SCALE = FOX_DH ** -0.5
IN_SPLITS = (NSA_W, 2 * NSA_DH, 2 * NSA_DH, 2 * NSA_DH, 3 * NSA_HEADS,
             MLA_Q_RANK, MLA_KV_RANK + MLA_ROPE,
             FOX_W, 2 * FOX_KV_HEADS * FOX_DH, FOX_HEADS,
             3 * D_MODEL)
IN_COLS = sum(IN_SPLITS)

kernel_name = 'hybrid_nsa_mla_fox_convffn_step'


def rmsnorm(x, g):
    xf = x.astype(jnp.float32)
    y = xf * lax.rsqrt(jnp.mean(xf * xf, axis=-1, keepdims=True) + EPS)
    return (y * g.astype(jnp.float32)).astype(x.dtype)


def rope(x, pos):
    half = x.shape[-1] // 2
    inv = ROPE_BASE ** (-jnp.arange(half, dtype=jnp.float32) / half)
    ang = pos.astype(jnp.float32)[:, None] * inv[None, :]
    shape = (1, pos.shape[0]) + (1,) * (x.ndim - 3) + (half,)
    cos, sin = jnp.cos(ang).reshape(shape), jnp.sin(ang).reshape(shape)
    xf = x.astype(jnp.float32)
    x1, x2 = xf[..., :half], xf[..., half:]
    return jnp.concatenate([x1 * cos - x2 * sin, x2 * cos + x1 * sin], axis=-1).astype(x.dtype)


def rel_bucket(dist):
    d = jnp.maximum(dist, 0)
    exact = REL_BUCKETS // 2
    scaled = jnp.log(jnp.maximum(d, 1).astype(jnp.float32) / exact) / math.log(REL_MAX_DIST / exact)
    large = jnp.minimum(exact + (scaled * (REL_BUCKETS - exact)).astype(jnp.int32), REL_BUCKETS - 1)
    return jnp.where(d < exact, d, large)


def rel_bias(table, dist):
    return table.astype(jnp.float32)[rel_bucket(dist)]


def masked_softmax(logits, mask):
    p = jax.nn.softmax(jnp.where(mask, logits, NEG), axis=-1)
    return jnp.where(mask, p, 0.0)


def block_means(x, n_blocks):
    xb = x[:, :n_blocks * CMP_BLOCK].reshape((x.shape[0], n_blocks, CMP_BLOCK) + x.shape[2:])
    return jnp.mean(xb.astype(jnp.float32), axis=2).astype(x.dtype)


def nsa_cmp_attn(q, kvc, q_pos, c_end, table):
    logits = jnp.einsum('bqhd,bnd->bhqn', q, kvc[:, :, 0]).astype(jnp.float32) * NSA_SCALE
    logits = logits + jnp.moveaxis(rel_bias(table, q_pos[:, None] - c_end[None, :]), -1, 0)
    p = masked_softmax(logits, c_end[None, :] <= q_pos[:, None])
    o = jnp.einsum('bhqn,bnd->bqhd', p.astype(q.dtype), kvc[:, :, 1])
    return o, p.sum(axis=1)


def nsa_select(p_cmp, q_pos, n_sel):
    B, Tq, NC = p_cmp.shape
    ratio = SEL_BLOCK // CMP_BLOCK
    score = jnp.pad(p_cmp, ((0, 0), (0, 0), (0, ratio * n_sel - NC))).reshape(B, Tq, n_sel, ratio).sum(-1)
    blk = jnp.arange(n_sel)[None, :]
    cur = (q_pos // SEL_BLOCK)[:, None]
    forced = (blk == 0) | (blk == cur) | (blk == cur - 1)
    future = blk * SEL_BLOCK > q_pos[:, None]
    score = jnp.where(forced, 1e6, jnp.where(future, -1e6, score))
    _, idx = lax.top_k(score, min(N_SEL, n_sel))
    return idx


def nsa_sel_attn(q, kv_sel, idx, q_pos, table):
    B, Tq, K = idx.shape
    k_pos = (idx[..., None] * SEL_BLOCK + jnp.arange(SEL_BLOCK)).reshape(B, Tq, K * SEL_BLOCK)
    kv = kv_sel.reshape(B, Tq, K * SEL_BLOCK, 2, NSA_DH)
    dist = q_pos[None, :, None] - k_pos
    logits = jnp.einsum('bqhd,bqsd->bhqs', q, kv[:, :, :, 0]).astype(jnp.float32) * NSA_SCALE
    logits = logits + jnp.moveaxis(rel_bias(table, dist), -1, 1)
    p = masked_softmax(logits, (dist >= 0)[:, None])
    return jnp.einsum('bhqs,bqsd->bqhd', p.astype(q.dtype), kv[:, :, :, 1])


def nsa_win_attn(q, kvw, q_pos, k_pos, table):
    dist = q_pos[:, None] - k_pos[None, :]
    logits = jnp.einsum('bqhd,bsd->bhqs', q, kvw[:, :, 0]).astype(jnp.float32) * NSA_SCALE
    logits = logits + jnp.moveaxis(rel_bias(table, dist), -1, 0)
    mask = (dist >= 0) & (dist < WINDOW) & (k_pos[None, :] >= 0)
    p = masked_softmax(logits, mask)
    return jnp.einsum('bhqs,bsd->bqhd', p.astype(q.dtype), kvw[:, :, 1])


def nsa_core(q, kvc, c_end, gather_sel, n_sel, kvw, kw_pos, q_pos, table):
    o_cmp, p_cmp = nsa_cmp_attn(q, kvc, q_pos, c_end, table)
    idx = nsa_select(p_cmp, q_pos, n_sel)
    o_sel = nsa_sel_attn(q, gather_sel(idx), idx, q_pos, table)
    o_win = nsa_win_attn(q, kvw, q_pos, kw_pos, table)
    return jnp.stack([o_cmp, o_sel, o_win], axis=2)


def mla_attend(q_abs, q_rope, rows, q_pos, k_pos):
    ckv, kr = rows[..., :MLA_KV_RANK], rows[..., MLA_KV_RANK:]
    logits = (jnp.einsum('bqhc,bsc->bhqs', q_abs, ckv)
              + jnp.einsum('bqhr,bsr->bhqs', q_rope, kr)).astype(jnp.float32) * MLA_SCALE
    p = masked_softmax(logits, k_pos[None, :] <= q_pos[:, None])
    return jnp.einsum('bhqs,bsc->bqhc', p.astype(rows.dtype), ckv)


def fox_attend(q, kv, cq, ck, q_pos, k_pos):
    B, Tq = q.shape[:2]
    L = kv.shape[1]
    G = FOX_HEADS // FOX_KV_HEADS
    logits = jnp.einsum('bqkgd,bskd->bkgqs', q, kv[:, :, 0]).astype(jnp.float32) * FOX_SCALE
    decay = (cq.reshape(B, Tq, FOX_KV_HEADS, G).transpose(0, 2, 3, 1)[..., None]
             - ck.reshape(B, L, FOX_KV_HEADS, G).transpose(0, 2, 3, 1)[..., None, :])
    p = masked_softmax(logits + decay, k_pos[None, :] <= q_pos[:, None])
    o = jnp.einsum('bkgqs,bskd->bqkgd', p.astype(kv.dtype), kv[:, :, 1])
    return o.reshape(B, Tq, FOX_W)


def layer_features(h, pos, w_in_l, mla_g_q_l, mla_w_uq_l, mla_g_kv_l, mla_w_uk_l, fox_b_f_l):
    B, T, _ = h.shape
    cuts = [int(i) for i in np.cumsum(IN_SPLITS)[:-1]]
    (nsa_q, nsa_c, nsa_s, nsa_w, nsa_g, mla_qd, mla_kvd,
     fox_q, fox_kv, fox_f, merge) = jnp.split(h @ w_in_l, cuts, axis=-1)
    q = (rmsnorm(mla_qd, mla_g_q_l) @ mla_w_uq_l).reshape(B, T, MLA_HEADS, MLA_NOPE + MLA_ROPE)
    return {
        'nsa_q': nsa_q.reshape(B, T, NSA_HEADS, NSA_DH),
        'nsa_c': nsa_c.reshape(B, T, 2, NSA_DH),
        'nsa_s': nsa_s.reshape(B, T, 2, NSA_DH),
        'nsa_w': nsa_w.reshape(B, T, 2, NSA_DH),
        'nsa_g': nsa_g.reshape(B, T, NSA_HEADS, 3),
        'q_abs': jnp.einsum('bthd,chd->bthc', q[..., :MLA_NOPE], mla_w_uk_l),
        'q_rope': rope(q[..., MLA_NOPE:], pos),
        'mla_rows': jnp.concatenate([rmsnorm(mla_kvd[..., :MLA_KV_RANK], mla_g_kv_l),
                                     rope(mla_kvd[..., MLA_KV_RANK:], pos)], axis=-1),
        'fox_q': fox_q.reshape(B, T, FOX_KV_HEADS, FOX_HEADS // FOX_KV_HEADS, FOX_DH),
        'fox_kv': fox_kv.reshape(B, T, 2, FOX_KV_HEADS, FOX_DH),
        'fox_logf': jax.nn.log_sigmoid((fox_f + fox_b_f_l).astype(jnp.float32)),
        'merge': jax.nn.sigmoid(merge.astype(jnp.float32)).astype(h.dtype).reshape(B, T, 3, D_MODEL),
    }


def attend_prompt(feats, table):
    nsa_q = feats['nsa_q']
    B, T = nsa_q.shape[:2]
    pos = jnp.arange(T)
    n_cmp = T // CMP_BLOCK
    kvc = block_means(feats['nsa_c'], n_cmp)
    c_end = jnp.arange(n_cmp) * CMP_BLOCK + CMP_BLOCK - 1
    n_sel = -(-T // SEL_BLOCK)
    sel_blocks = jnp.pad(feats['nsa_s'], ((0, 0), (0, n_sel * SEL_BLOCK - T), (0, 0), (0, 0))
                         ).reshape(B, n_sel, SEL_BLOCK, 2, NSA_DH)
    kvw_pad = jnp.pad(feats['nsa_w'], ((0, 0), (WINDOW, 0), (0, 0), (0, 0)))
    fox_c = jnp.cumsum(feats['fox_logf'], axis=1)
    bidx = jnp.arange(B)[:, None, None]

    def block(i):
        q0 = i * Q_BLOCK
        q_pos = q0 + jnp.arange(Q_BLOCK)
        sl = lambda a: lax.dynamic_slice_in_dim(a, q0, Q_BLOCK, axis=1)
        kvw = lax.dynamic_slice_in_dim(kvw_pad, q0, WINDOW + Q_BLOCK, axis=1)
        kw_pos = q0 - WINDOW + jnp.arange(WINDOW + Q_BLOCK)
        o_nsa = nsa_core(sl(nsa_q), kvc, c_end, lambda idx: sel_blocks[bidx, idx], n_sel,
                         kvw, kw_pos, q_pos, table)
        o_lat = mla_attend(sl(feats['q_abs']), sl(feats['q_rope']), feats['mla_rows'], q_pos, pos)
        o_fox = fox_attend(sl(feats['fox_q']), feats['fox_kv'], sl(fox_c), fox_c, q_pos, pos)
        return o_nsa, o_lat, o_fox

    outs = lax.map(block, jnp.arange(T // Q_BLOCK))
    unblock = lambda a: jnp.swapaxes(a, 0, 1).reshape((B, T) + a.shape[3:])
    o_nsa, o_lat, o_fox = (unblock(a) for a in outs)
    dt = nsa_q.dtype
    rows = (feats['nsa_c'], feats['nsa_s'], feats['nsa_w'][:, T - min(WINDOW, T):],
            feats['mla_rows'], feats['fox_kv'], feats['fox_logf'].astype(dt))
    return o_nsa, o_lat, o_fox, rows


def attend_sample(feats, layer, page_table, cache_nsa_cmp, cache_nsa_sel, state_nsa_win,
                  cache_mla, cache_fox_kv, cache_fox_logf, table):
    nsa_q = feats['nsa_q']
    DB, S = nsa_q.shape[:2]
    n_pages = page_table.shape[1]
    past = n_pages * PAGE_SIZE
    total = past + S
    q_pos = past + jnp.arange(S)
    k_pos = jnp.arange(total)
    bidx = jnp.arange(DB)[:, None, None]
    past_c = cache_nsa_cmp[layer, page_table].reshape(DB, past, 2, NSA_DH)
    kvc = jnp.concatenate([block_means(past_c, past // CMP_BLOCK),
                           block_means(feats['nsa_c'], S // CMP_BLOCK)], axis=1)
    c_end = jnp.arange(kvc.shape[1]) * CMP_BLOCK + CMP_BLOCK - 1
    n_sel = -(-total // SEL_BLOCK)
    nb_past = past // SEL_BLOCK
    nb_new = n_sel - nb_past
    new_blocks = jnp.pad(feats['nsa_s'], ((0, 0), (0, nb_new * SEL_BLOCK - S), (0, 0), (0, 0))
                         ).reshape(DB, nb_new, SEL_BLOCK, 2, NSA_DH)

    def gather_sel(idx):
        start = idx * SEL_BLOCK
        phys = page_table[bidx, jnp.minimum(start // PAGE_SIZE, n_pages - 1)]
        rows = (start % PAGE_SIZE)[..., None] + jnp.arange(SEL_BLOCK)
        from_past = cache_nsa_sel[layer, phys[..., None], rows]
        from_new = new_blocks[bidx, jnp.clip(idx - nb_past, 0, nb_new - 1)]
        return jnp.where((idx < nb_past)[..., None, None, None], from_past, from_new)

    buf = state_nsa_win[layer]
    wb = buf.shape[1]
    kvw = jnp.concatenate([buf, feats['nsa_w']], axis=1)
    kw_pos = past - wb + jnp.arange(wb + S)
    o_nsa = nsa_core(nsa_q, kvc, c_end, gather_sel, n_sel, kvw, kw_pos, q_pos, table)
    past_mla = cache_mla[layer, page_table].reshape(DB, past, MLA_KV_RANK + MLA_ROPE)
    mla_rows = jnp.concatenate([past_mla, feats['mla_rows']], axis=1)
    o_lat = mla_attend(feats['q_abs'], feats['q_rope'], mla_rows, q_pos, k_pos)
    past_kv = cache_fox_kv[layer, page_table].reshape(DB, past, 2, FOX_KV_HEADS, FOX_DH)
    past_lf = cache_fox_logf[layer, page_table].reshape(DB, past, FOX_HEADS)
    c_all = jnp.cumsum(jnp.concatenate([past_lf.astype(jnp.float32), feats['fox_logf']], axis=1), axis=1)
    o_fox = fox_attend(feats['fox_q'], jnp.concatenate([past_kv, feats['fox_kv']], axis=1),
                       c_all[:, past:], c_all, q_pos, k_pos)
    dt = nsa_q.dtype
    rows = (feats['nsa_c'], feats['nsa_s'], kvw[:, kvw.shape[1] - min(WINDOW, total):],
            feats['mla_rows'], feats['fox_kv'], feats['fox_logf'].astype(dt))
    return o_nsa, o_lat, o_fox, rows


def adaln(c, w, b):
    m = (jax.nn.silu(c) @ w + b)[:, None, :]
    return jnp.split(m, 6, axis=-1)


def conv_ffn(h, prefix, w_in, conv_w, conv_b, w_out):
    T = h.shape[1]
    a, b = jnp.split(h @ w_in, 2, axis=-1)
    ext = jnp.concatenate([prefix, a], axis=1)
    conv = sum(ext[:, k:k + T] * conv_w[k] for k in range(CONV_W)) + conv_b
    y = (jax.nn.gelu(conv) * b) @ w_out
    return y, ext[:, ext.shape[1] - (CONV_W - 1):]


def run_layer(x, c, pos, attend, conv_prefix, lw):
    (g_attn_l, g_ffn_l, w_ada_l, b_ada_l, w_in_l, mla_g_q_l, mla_w_uq_l, mla_g_kv_l, mla_w_uk_l,
     mla_w_uv_l, fox_b_f_l, w_br_l, w_o_l, w_ffn_in_l, conv_w_l, conv_b_l, w_ffn_out_l) = lw
    B, T, _ = x.shape
    sh_a, sc_a, gt_a, sh_f, sc_f, gt_f = adaln(c, w_ada_l, b_ada_l)
    h = rmsnorm(x, g_attn_l) * (1 + sc_a) + sh_a
    feats = layer_features(h, pos, w_in_l, mla_g_q_l, mla_w_uq_l, mla_g_kv_l, mla_w_uk_l, fox_b_f_l)
    o3, o_lat, o_fox, rows = attend(feats)
    g_nsa = jax.nn.sigmoid(feats['nsa_g'].astype(jnp.float32)).astype(x.dtype)
    o_nsa = jnp.einsum('bthc,btchd->bthd', g_nsa, o3).reshape(B, T, NSA_W)
    o_mla = jnp.einsum('bthc,chv->bthv', o_lat, mla_w_uv_l).reshape(B, T, MLA_W)
    mg = feats['merge']
    mix = (mg[:, :, 0] * (o_nsa @ w_br_l[:NSA_W])
           + mg[:, :, 1] * (o_mla @ w_br_l[NSA_W:NSA_W + MLA_W])
           + mg[:, :, 2] * (o_fox @ w_br_l[NSA_W + MLA_W:])) @ w_o_l
    x = x + gt_a * mix
    h = rmsnorm(x, g_ffn_l) * (1 + sc_f) + sh_f
    y, conv_state = conv_ffn(h, conv_prefix, w_ffn_in_l, conv_w_l, conv_b_l, w_ffn_out_l)
    x = x + gt_f * y
    return x, rows + (conv_state,)


def setup_inputs(seed: int = 0) -> dict:
    key = jax.random.key(seed)
    ks = iter(jax.random.split(key, 48))
    nrm = lambda shape, scale: jax.random.normal(next(ks), shape, jnp.float32) * scale
    n_pages = PAST_LEN // PAGE_SIZE
    n_used = DEC_BATCH * n_pages
    n_pool = n_used + n_used // 4
    win_len = min(WINDOW, PAST_LEN)
    d = D_MODEL
    page_table = jax.random.permutation(next(ks), n_pool)[:n_used].reshape(DEC_BATCH, n_pages).astype(jnp.int32)
    return {
        'x_prompt': nrm((BATCH, SEQ, d), 1.0),
        'x_sample': nrm((DEC_BATCH, DEC_SEQ, d), 1.0),
        'cache_nsa_cmp': nrm((DEPTH, n_pool, PAGE_SIZE, 2, NSA_DH), 1.0),
        'cache_nsa_sel': nrm((DEPTH, n_pool, PAGE_SIZE, 2, NSA_DH), 1.0),
        'state_nsa_win': nrm((DEPTH, DEC_BATCH, win_len, 2, NSA_DH), 1.0),
        'cache_mla': nrm((DEPTH, n_pool, PAGE_SIZE, MLA_KV_RANK + MLA_ROPE), 1.0),
        'cache_fox_kv': nrm((DEPTH, n_pool, PAGE_SIZE, 2, FOX_KV_HEADS, FOX_DH), 1.0),
        'cache_fox_logf': jax.nn.log_sigmoid(FOX_F_BIAS + nrm((DEPTH, n_pool, PAGE_SIZE, FOX_HEADS), 1.0)),
        'state_ffn_conv': nrm((DEPTH, DEC_BATCH, CONV_W - 1, D_FF), 1.0),
        'page_table': page_table,
        'c_prompt': nrm((BATCH, d), 1.0),
        'c_sample': nrm((DEC_BATCH, d), 1.0),
        'rel_table': nrm((REL_BUCKETS, NSA_HEADS), 0.5),
        'w_ada': nrm((DEPTH, d, 6 * d), 0.5 * d ** -0.5),
        'b_ada': nrm((DEPTH, 6 * d), 0.02),
        'g_attn': 1.0 + nrm((DEPTH, d), 0.05),
        'g_ffn': 1.0 + nrm((DEPTH, d), 0.05),
        'w_in': nrm((DEPTH, d, IN_COLS), d ** -0.5),
        'mla_g_q': 1.0 + nrm((DEPTH, MLA_Q_RANK), 0.05),
        'mla_w_uq': nrm((DEPTH, MLA_Q_RANK, MLA_HEADS * (MLA_NOPE + MLA_ROPE)), MLA_Q_RANK ** -0.5),
        'mla_g_kv': 1.0 + nrm((DEPTH, MLA_KV_RANK), 0.05),
        'mla_w_uk': nrm((DEPTH, MLA_KV_RANK, MLA_HEADS, MLA_NOPE), MLA_KV_RANK ** -0.5),
        'mla_w_uv': nrm((DEPTH, MLA_KV_RANK, MLA_HEADS, MLA_DV), MLA_KV_RANK ** -0.5),
        'fox_b_f': FOX_F_BIAS + nrm((DEPTH, FOX_HEADS), 0.1),
        'w_br': jnp.concatenate([nrm((DEPTH, NSA_W, d), NSA_W ** -0.5),
                                 nrm((DEPTH, MLA_W, d), MLA_W ** -0.5),
                                 nrm((DEPTH, FOX_W, d), FOX_W ** -0.5)], axis=1),
        'w_o': nrm((DEPTH, d, d), d ** -0.5),
        'w_ffn_in': nrm((DEPTH, d, 2 * D_FF), d ** -0.5),
        'conv_w': nrm((DEPTH, CONV_W, D_FF), CONV_W ** -0.5),
        'conv_b': nrm((DEPTH, D_FF), 0.02),
        'w_ffn_out': nrm((DEPTH, D_FF, d), D_FF ** -0.5),
        'g_final': 1.0 + nrm((d,), 0.05),
    }


def reference(x_prompt, x_sample, cache_nsa_cmp, cache_nsa_sel, state_nsa_win, cache_mla, cache_fox_kv,
              cache_fox_logf, state_ffn_conv, page_table, c_prompt, c_sample, rel_table, w_ada, b_ada,
              g_attn, g_ffn, w_in, mla_g_q, mla_w_uq, mla_g_kv, mla_w_uk, mla_w_uv, fox_b_f, w_br, w_o,
              w_ffn_in, conv_w, conv_b, w_ffn_out, g_final):
    B, T, _ = x_prompt.shape
    S = x_sample.shape[1]
    past = page_table.shape[1] * PAGE_SIZE
    pos_p = jnp.arange(T)
    pos_s = past + jnp.arange(S)
    xp, xs = x_prompt, x_sample
    rows_p, rows_s = [], []
    attend_p = functools.partial(attend_prompt, table=rel_table)
    for l in range(DEPTH):
        lw = (g_attn[l], g_ffn[l], w_ada[l], b_ada[l], w_in[l], mla_g_q[l], mla_w_uq[l], mla_g_kv[l],
              mla_w_uk[l], mla_w_uv[l], fox_b_f[l], w_br[l], w_o[l], w_ffn_in[l], conv_w[l], conv_b[l],
              w_ffn_out[l])
        attend_s = functools.partial(attend_sample, layer=l, page_table=page_table,
                                     cache_nsa_cmp=cache_nsa_cmp, cache_nsa_sel=cache_nsa_sel,
                                     state_nsa_win=state_nsa_win, cache_mla=cache_mla,
                                     cache_fox_kv=cache_fox_kv, cache_fox_logf=cache_fox_logf,
                                     table=rel_table)
        xp, r_p = run_layer(xp, c_prompt, pos_p, attend_p,
                            jnp.zeros((B, CONV_W - 1, D_FF), xp.dtype), lw)
        xs, r_s = run_layer(xs, c_sample, pos_s, attend_s, state_ffn_conv[l], lw)
        rows_p.append(r_p)
        rows_s.append(r_s)
    y_prompt = rmsnorm(xp, g_final)
    y_sample = rmsnorm(xs, g_final)
    nsa_cmp_p, nsa_sel_p, nsa_win_p, mla_p, fox_kv_p, fox_logf_p, conv_p = [jnp.stack(a) for a in zip(*rows_p)]
    nsa_cmp_s, nsa_sel_s, nsa_win_s, mla_s, fox_kv_s, fox_logf_s, conv_s = [jnp.stack(a) for a in zip(*rows_s)]
    return (y_prompt, y_sample,
            nsa_cmp_p, nsa_sel_p, nsa_win_p, mla_p, fox_kv_p, fox_logf_p, conv_p,
            nsa_cmp_s, nsa_sel_s, nsa_win_s, mla_s, fox_kv_s, fox_logf_s, conv_s)
```

```python
import functools
import math

import numpy as np
import jax
import jax.numpy as jnp
from jax import lax
from jax.experimental import pallas as pl
from jax.experimental.pallas import tpu as pltpu

F32 = jnp.float32
BF16 = jnp.bfloat16

NSA_HEADS = 4
NSA_DH = 64
CMP_BLOCK = 32
SEL_BLOCK = 64
N_SEL = 16
WINDOW = 512
MLA_HEADS = 4
MLA_NOPE = 64
MLA_ROPE = 32
MLA_KV_RANK = 64
ROPE_BASE = 10000.0
FOX_HEADS = 8
FOX_DH = 64
REL_BUCKETS = 32
REL_MAX_DIST = 128
CONV_W = 3
EPS = 1e-6
NEG = -1e30
LANE = 128
VMEM_LIMIT = 56 * 1024 * 1024

NSA_SCALE = NSA_DH ** -0.5
MLA_SCALE = (MLA_NOPE + MLA_ROPE) ** -0.5
FOX_SCALE = FOX_DH ** -0.5

_NT = (((1,), (1,)), ((), ()))


def _cparams(n_axes):
    return pltpu.CompilerParams(dimension_semantics=("arbitrary",) * n_axes,
                                vmem_limit_bytes=VMEM_LIMIT)


def _dot(a, b):
    return jnp.dot(a, b, preferred_element_type=F32)


def _dot_nt(a, b):
    return lax.dot_general(a, b, _NT, preferred_element_type=F32)


def _rms(x, g):
    return x * lax.rsqrt(jnp.mean(x * x, axis=-1, keepdims=True) + EPS) * g


def _ada_body(c_ref, w_ref, b_ref, o_ref):
    c = c_ref[...]
    sc = (c * jax.nn.sigmoid(c)).astype(BF16)
    o_ref[0] = _dot(sc, w_ref[0].astype(BF16)) + b_ref[0]


def _adaln(c_all, w_ada, b_ada):
    depth, d, n6 = w_ada.shape
    rows = c_all.shape[0]
    tn = 512
    return pl.pallas_call(
        _ada_body,
        grid=(depth, n6 // tn),
        in_specs=[pl.BlockSpec((rows, d), lambda l, n: (0, 0)),
                  pl.BlockSpec((1, d, tn), lambda l, n: (l, 0, n)),
                  pl.BlockSpec((1, 1, tn), lambda l, n: (l, 0, n))],
        out_specs=pl.BlockSpec((1, rows, tn), lambda l, n: (l, 0, n)),
        out_shape=jax.ShapeDtypeStruct((depth, rows, n6), F32),
        compiler_params=_cparams(2),
        name="adaln",
    )(c_all, w_ada, b_ada.reshape(depth, 1, n6))


def _pre_body(x_ref, sc_ref, sh_ref, g_ref, wrow_ref, wt_ref, gq_ref, wuq_ref, wuk_ref, pp_ref,
              gkv_ref, bf_ref, cs1_ref, cs2_ref, csq1_ref, csq2_ref, triu_ref,
              qn_ref, qf_ref, qm_ref, sm_ref, ct_ref, st_ref, wtt_ref, ft_ref, mt_ref, lf_ref, nc_ref,
              carry_ref, *, tiles_per_seq, with_cumsum):
    x = x_ref[...]
    h = _rms(x, g_ref[...]) * (1.0 + sc_ref[0]) + sh_ref[0]
    hb = h.astype(BF16)
    pr = _dot(hb, wrow_ref[...])
    nq = NSA_HEADS * LANE
    nf = FOX_HEADS * LANE
    qn_ref[...] = pr[:, :nq].astype(BF16)
    qf_ref[...] = pr[:, nq:nq + nf].astype(BF16)
    qd = pr[:, nq + nf:nq + nf + 256]
    sm_ref[...] = jax.nn.sigmoid(pr[:, nq + nf + 256:])
    q = _dot(_rms(qd, gq_ref[...]).astype(BF16), wuq_ref[...])
    qrot = q[:, 256:384] * csq1_ref[...] + q[:, 384:512] * csq2_ref[...]
    qm = _dot(q[:, :256].astype(BF16), wuk_ref[...]) * MLA_SCALE
    qm = qm + _dot((qrot * MLA_SCALE).astype(BF16), pp_ref[...])
    qm_ref[...] = qm.astype(BF16)
    pt = _dot_nt(wt_ref[...], hb)
    ct_ref[0] = pt[0:128]
    st_ref[0] = pt[128:256]
    wtt_ref[0] = pt[256:384]
    ft_ref[0] = pt[384:512]
    ckv = pt[512:576]
    ckv = ckv * lax.rsqrt(jnp.mean(ckv * ckv, axis=0, keepdims=True) + EPS) * gkv_ref[...]
    krot = pt[576:608] * cs1_ref[...] + pt[608:640] * cs2_ref[...]
    mt_ref[0, 0:64] = ckv
    mt_ref[0, 64:96] = krot
    mt_ref[0, 96:128] = jnp.zeros_like(krot)
    lf = jax.nn.log_sigmoid(pt[640:648] + bf_ref[...])
    lf_ref[0] = lf
    if with_cumsum:
        @pl.when(pl.program_id(0) % tiles_per_seq == 0)
        def _():
            carry_ref[...] = jnp.zeros_like(carry_ref)
        c = jnp.dot(lf, triu_ref[...], preferred_element_type=F32,
                    precision=lax.Precision.HIGHEST) + carry_ref[:, 0:1]
        nc_ref[0] = -c
        carry_ref[...] = jnp.broadcast_to(c[:, c.shape[1] - 1:], carry_ref.shape)
    else:
        nc_ref[0] = jnp.zeros_like(lf)


def _pre_proj(x, sc, sh, mod_index, lw, pos_tabs, *, n_groups, tiles_per_group, n_pos_tiles, tm,
              with_cumsum):
    n, d = x.shape
    cs1, cs2, csq1, csq2 = pos_tabs
    grid = (n // tm,)
    tpg = tiles_per_group
    triu = jnp.triu(jnp.ones((tm, tm), F32))
    row = lambda i: (i, 0)
    const2 = lambda i: (0, 0)
    tcol = lambda i: (i // tpg, 0, i % tpg)
    ncols = tpg * tm

    def tspec(r):
        return pl.BlockSpec((1, r, tm), tcol)

    def tshape(r):
        return jax.ShapeDtypeStruct((n_groups, r, ncols), F32)

    nq = NSA_HEADS * LANE
    nf = FOX_HEADS * LANE
    outs = pl.pallas_call(
        functools.partial(_pre_body, tiles_per_seq=tpg, with_cumsum=with_cumsum),
        grid=grid,
        in_specs=[pl.BlockSpec((tm, d), row),
                  pl.BlockSpec((1,) + sc.shape[1:], mod_index),
                  pl.BlockSpec((1,) + sh.shape[1:], mod_index),
                  pl.BlockSpec((1, d), const2),
                  pl.BlockSpec(lw["w_row"].shape, const2),
                  pl.BlockSpec(lw["w_t"].shape, const2),
                  pl.BlockSpec((1, 256), const2),
                  pl.BlockSpec(lw["w_uq"].shape, const2),
                  pl.BlockSpec(lw["w_uk"].shape, const2),
                  pl.BlockSpec(lw["pp"].shape, const2),
                  pl.BlockSpec((64, 1), const2),
                  pl.BlockSpec((8, 1), const2),
                  pl.BlockSpec((32, tm), lambda i: (0, i % n_pos_tiles)),
                  pl.BlockSpec((32, tm), lambda i: (0, i % n_pos_tiles)),
                  pl.BlockSpec((tm, 128), lambda i: (i % n_pos_tiles, 0)),
                  pl.BlockSpec((tm, 128), lambda i: (i % n_pos_tiles, 0)),
                  pl.BlockSpec((tm, tm), const2)],
        out_specs=[pl.BlockSpec((tm, nq), row), pl.BlockSpec((tm, nf), row),
                   pl.BlockSpec((tm, nq), row), pl.BlockSpec((tm, LANE), row),
                   tspec(128), tspec(128), tspec(128), tspec(128), tspec(128), tspec(8), tspec(8)],
        out_shape=[jax.ShapeDtypeStruct((n, nq), BF16), jax.ShapeDtypeStruct((n, nf), BF16),
                   jax.ShapeDtypeStruct((n, nq), BF16), jax.ShapeDtypeStruct((n, LANE), F32),
                   tshape(128), tshape(128), tshape(128), tshape(128), tshape(128), tshape(8),
                   tshape(8)],
        scratch_shapes=[pltpu.VMEM((8, LANE), F32)],
        compiler_params=_cparams(1),
        name="pre_proj",
    )(x, sc, sh, lw["g_attn"], lw["w_row"], lw["w_t"], lw["g_q"], lw["w_uq"], lw["w_uk"], lw["pp"],
      lw["g_kv"], lw["b_f"], cs1, cs2, csq1, csq2, triu)
    keys = ("qn", "qf", "qm", "gates", "ct", "st", "wt", "ft", "mt", "lft", "nct")
    return dict(zip(keys, outs))


def _mean_body(x_ref, a_ref, o_ref):
    o_ref[0] = jnp.dot(x_ref[0], a_ref[...], preferred_element_type=F32,
                       precision=lax.Precision.HIGHEST)


def _block_means_t(xt):
    g, r, t = xt.shape
    tc = min(t, CMP_BLOCK * LANE)
    assert t % tc == 0
    nb = tc // CMP_BLOCK
    a = (jnp.arange(tc)[:, None] // CMP_BLOCK == jnp.arange(nb)[None, :]).astype(F32) / CMP_BLOCK
    return pl.pallas_call(
        _mean_body,
        grid=(g, t // tc),
        in_specs=[pl.BlockSpec((1, r, tc), lambda b, c: (b, 0, c)),
                  pl.BlockSpec((tc, nb), lambda b, c: (0, 0))],
        out_specs=pl.BlockSpec((1, r, nb), lambda b, c: (b, 0, c)),
        out_shape=jax.ShapeDtypeStruct((g, r, t // CMP_BLOCK), F32),
        compiler_params=_cparams(2),
        name="block_means",
    )(xt, a)


def _flash_init(m_ref, l_ref, acc_ref):
    m_ref[...] = jnp.full(m_ref.shape, NEG, F32)
    l_ref[...] = jnp.zeros(l_ref.shape, F32)
    acc_ref[...] = jnp.zeros(acc_ref.shape, F32)


def _flash_update(s, kt, m_ref, l_ref, acc_ref):
    m_old = m_ref[...]
    m_new = jnp.maximum(m_old, jnp.max(s, axis=-1, keepdims=True))
    alpha = jnp.exp(m_old - m_new)
    p = jnp.exp(s - m_new)
    l_ref[...] = alpha * l_ref[...] + jnp.sum(p, axis=-1, keepdims=True)
    acc_ref[...] = alpha * acc_ref[...] + _dot_nt(p.astype(BF16), kt)
    m_ref[...] = m_new


def _stack_heads(q_ref, heads):
    return jnp.concatenate([q_ref[:, h * LANE:(h + 1) * LANE] for h in range(heads)], axis=0)


def _unstack_heads(o, heads, tq):
    return jnp.concatenate([o[h * tq:(h + 1) * tq] for h in range(heads)], axis=1)


def _causal_body(q_ref, kt_ref, *rest, heads, tq, tk, with_bias):
    if with_bias:
        nc_ref, o_ref, m_ref, l_ref, acc_ref = rest
    else:
        o_ref, m_ref, l_ref, acc_ref = rest
    i = pl.program_id(1)
    q0 = i * tq
    qs = _stack_heads(q_ref, heads)
    _flash_init(m_ref, l_ref, acc_ref)

    def logits(c0):
        kt = kt_ref[0, :, pl.ds(c0, tk)].astype(BF16)
        s = _dot(qs, kt)
        if with_bias:
            nc = nc_ref[0, :, pl.ds(c0, tk)]
            bias = jnp.concatenate(
                [jnp.broadcast_to(nc[h:h + 1], (tq, tk)) for h in range(heads)], axis=0)
            s = s + bias
        return s, kt

    def full_step(j, carry):
        c0 = pl.multiple_of(j * tk, tk)
        s, kt = logits(c0)
        _flash_update(s, kt, m_ref, l_ref, acc_ref)
        return carry

    n_full = q0 // tk
    lax.fori_loop(0, n_full, full_step, 0)
    c0 = pl.multiple_of(n_full * tk, tk)
    s, kt = logits(c0)
    col = c0 + lax.broadcasted_iota(jnp.int32, (tq, tk), 1)
    rowp = q0 + lax.broadcasted_iota(jnp.int32, (tq, tk), 0)
    s = jnp.where((col <= rowp)[None], s.reshape(heads, tq, tk), NEG).reshape(heads * tq, tk)
    _flash_update(s, kt, m_ref, l_ref, acc_ref)
    o = acc_ref[...] / l_ref[...]
    o_ref[...] = _unstack_heads(o, heads, tq).astype(BF16)


def _causal_attn(q, kt, nct, *, heads, tq, tk):
    b, r, t = kt.shape
    nq = t // tq
    with_bias = nct is not None
    in_specs = [pl.BlockSpec((tq, heads * LANE), lambda bi, i: (bi * nq + i, 0)),
                pl.BlockSpec((1, r, t), lambda bi, i: (bi, 0, 0))]
    args = [q, kt]
    if with_bias:
        in_specs.append(pl.BlockSpec((1, heads, t), lambda bi, i: (bi, 0, 0)))
        args.append(nct)
    m = heads * tq
    return pl.pallas_call(
        functools.partial(_causal_body, heads=heads, tq=tq, tk=tk, with_bias=with_bias),
        grid=(b, nq),
        in_specs=in_specs,
        out_specs=pl.BlockSpec((tq, heads * LANE), lambda bi, i: (bi * nq + i, 0)),
        out_shape=jax.ShapeDtypeStruct((b * t, heads * LANE), BF16),
        scratch_shapes=[pltpu.VMEM((m, 1), F32), pltpu.VMEM((m, 1), F32), pltpu.VMEM((m, LANE), F32)],
        compiler_params=_cparams(2),
        name="causal_attn_h%d" % heads,
    )(*args)


def _top_k_neg_mask(score, k):
    n = score.shape[-1]
    lane = lax.broadcasted_iota(jnp.int32, score.shape, 1).astype(F32)
    out = jnp.full(score.shape, NEG, F32)
    work = score
    for _ in range(k):
        mx = jnp.max(work, axis=-1, keepdims=True)
        idx = jnp.min(jnp.where(work == mx, lane, float(n)), axis=-1, keepdims=True)
        hit = lane == idx
        out = jnp.where(hit, 0.0, out)
        work = jnp.where(hit, -jnp.inf, work)
    return out


def _log2(n):
    assert n & (n - 1) == 0, n
    return n.bit_length() - 1


def _nsa_body(q_ref, kvc_ref, st_ref, wt_ref, g_ref, cb_ref, sb_ref, wb_ref, o_ref,
              m_ref, l_ref, acc_ref, m2_ref, l2_ref, acc2_ref, *, tq):
    hh = NSA_HEADS
    i = pl.program_id(1)
    q0 = i * tq
    qs = _stack_heads(q_ref, hh)
    kvc = kvc_ref[0].astype(BF16)
    nc = kvc.shape[1]
    cb = cb_ref[...].reshape(hh * tq, nc)
    s = _dot(qs, kvc) + cb
    e = jnp.exp(s - jnp.max(s, axis=-1, keepdims=True))
    p = e / jnp.sum(e, axis=-1, keepdims=True)
    p = jnp.where(cb > 0.5 * NEG, p, 0.0)
    o_cmp = _dot_nt(p.astype(BF16), kvc)
    pc = p[0:tq]
    for h in range(1, hh):
        pc = pc + p[h * tq:(h + 1) * tq]
    n_sel = nc // 2
    score = pc[:, :n_sel] + pc[:, n_sel:]
    blk = lax.broadcasted_iota(jnp.int32, (tq, n_sel), 1)
    qpos = q0 + lax.broadcasted_iota(jnp.int32, (tq, n_sel), 0)
    cur = jnp.right_shift(qpos, _log2(SEL_BLOCK))
    forced = (blk == 0) | (blk == cur) | (blk == cur - 1)
    future = blk * SEL_BLOCK > qpos
    score = jnp.where(forced, 1e6, jnp.where(future, -1e6, score))
    selneg = _top_k_neg_mask(score, min(N_SEL, n_sel)).astype(BF16)
    selneg = jnp.concatenate([selneg] * hh, axis=0)
    blk_id = lax.broadcasted_iota(jnp.int32, (n_sel, tq), 0)
    tok_blk = jnp.right_shift(lax.broadcasted_iota(jnp.int32, (n_sel, tq), 1), _log2(SEL_BLOCK))
    per_tile = tq // SEL_BLOCK

    _flash_init(m_ref, l_ref, acc_ref)

    def sel_logits(j):
        c0 = pl.multiple_of(j * tq, tq)
        kt = st_ref[0, :, pl.ds(c0, tq)].astype(BF16)
        ind = jnp.where(blk_id == j * per_tile + tok_blk, 1.0, 0.0).astype(BF16)
        return _dot(qs, kt) + _dot(selneg, ind), kt

    def far_step(j, carry):
        s_j, kt = sel_logits(j)
        _flash_update(s_j, kt, m_ref, l_ref, acc_ref)
        return carry

    lax.fori_loop(0, jnp.maximum(i - 1, 0), far_step, 0)

    @pl.when(i > 0)
    def _():
        s_j, kt = sel_logits(i - 1)
        _flash_update(s_j + sb_ref[:, 0:tq], kt, m_ref, l_ref, acc_ref)

    s_j, kt = sel_logits(i)
    _flash_update(s_j + sb_ref[:, tq:2 * tq], kt, m_ref, l_ref, acc_ref)
    o_sel = acc_ref[...] / l_ref[...]

    _flash_init(m2_ref, l2_ref, acc2_ref)
    n_back = WINDOW // tq
    for kk in range(n_back + 1):
        back = n_back - kk

        @pl.when(i >= back)
        def _(kk=kk, back=back):
            c0 = pl.multiple_of((i - back) * tq, tq)
            ktw = wt_ref[0, :, pl.ds(c0, tq)].astype(BF16)
            _flash_update(_dot(qs, ktw) + wb_ref[:, kk * tq:(kk + 1) * tq], ktw,
                          m2_ref, l2_ref, acc2_ref)

    o_win = acc2_ref[...] / l2_ref[...]
    g = g_ref[...]
    outs = []
    for h in range(hh):
        sl = slice(h * tq, (h + 1) * tq)
        outs.append(g[:, 3 * h:3 * h + 1] * o_cmp[sl] + g[:, 3 * h + 1:3 * h + 2] * o_sel[sl]
                    + g[:, 3 * h + 2:3 * h + 3] * o_win[sl])
    o_ref[...] = jnp.concatenate(outs, axis=1).astype(BF16)


def _rel_bucket(dist):
    d = jnp.maximum(dist, 0)
    exact = REL_BUCKETS // 2
    scaled = jnp.log(jnp.maximum(d, 1).astype(F32) / exact) / math.log(REL_MAX_DIST / exact)
    large = jnp.minimum(exact + (scaled * (REL_BUCKETS - exact)).astype(jnp.int32), REL_BUCKETS - 1)
    return jnp.where(d < exact, d, large)


def _rel_bias_t(table, dist):
    return jnp.moveaxis(table.astype(F32)[_rel_bucket(dist)], -1, 0)


def _nsa_prompt_tables(rel_table, t, tq):
    far = rel_table[REL_BUCKETS - 1].astype(F32)
    nc = t // CMP_BLOCK
    order = jnp.concatenate([jnp.arange(0, nc, 2), jnp.arange(1, nc, 2)])
    c_end = order * CMP_BLOCK + CMP_BLOCK - 1
    dist = jnp.arange(t)[:, None] - c_end[None, :]
    cb = jnp.where(dist >= 0, _rel_bias_t(rel_table, dist), NEG)
    dist = jnp.arange(tq)[:, None] + tq - jnp.arange(2 * tq)[None, :]
    sb = jnp.where(dist >= 0, _rel_bias_t(rel_table, dist) - far[:, None, None], NEG)
    nb = WINDOW // tq
    dist = jnp.arange(tq)[:, None] + nb * tq - jnp.arange((nb + 1) * tq)[None, :]
    wb = jnp.where((dist >= 0) & (dist < WINDOW), _rel_bias_t(rel_table, dist), NEG)
    hh = NSA_HEADS
    return cb, sb.reshape(hh * tq, 2 * tq), wb.reshape(hh * tq, (nb + 1) * tq)


def _nsa_prompt(qn, kvct, st, wt, gates, tables, *, tq):
    b, r, t = st.shape
    nq = t // tq
    cb, sb, wb = tables
    nc = kvct.shape[2]
    hh = NSA_HEADS
    m = hh * tq
    row = lambda bi, i: (bi * nq + i, 0)
    per_b = lambda bi, i: (bi, 0, 0)
    return pl.pallas_call(
        functools.partial(_nsa_body, tq=tq),
        grid=(b, nq),
        in_specs=[pl.BlockSpec((tq, hh * LANE), row),
                  pl.BlockSpec((1, r, nc), per_b),
                  pl.BlockSpec((1, r, t), per_b),
                  pl.BlockSpec((1, r, t), per_b),
                  pl.BlockSpec((tq, LANE), row),
                  pl.BlockSpec((hh, tq, nc), lambda bi, i: (0, i, 0)),
                  pl.BlockSpec(sb.shape, lambda bi, i: (0, 0)),
                  pl.BlockSpec(wb.shape, lambda bi, i: (0, 0))],
        out_specs=pl.BlockSpec((tq, hh * LANE), row),
        out_shape=jax.ShapeDtypeStruct((b * t, hh * LANE), BF16),
        scratch_shapes=[pltpu.VMEM((m, 1), F32), pltpu.VMEM((m, 1), F32), pltpu.VMEM((m, LANE), F32),
                        pltpu.VMEM((m, 1), F32), pltpu.VMEM((m, 1), F32), pltpu.VMEM((m, LANE), F32)],
        compiler_params=_cparams(2),
        name="nsa_prompt",
    )(qn, kvct, st, wt, gates, cb, sb, wb)


def _mix_body(x_ref, sc_ref, sh_ref, gt_ref, g_ref, on_ref, om_ref, of_ref, wmg_ref, wbn_ref,
              wuv_ref, wbm_ref, wbf_ref, wo_ref, o_ref):
    x = x_ref[...]
    d = x.shape[1]
    h = _rms(x, g_ref[...]) * (1.0 + sc_ref[0]) + sh_ref[0]
    mg = jax.nn.sigmoid(_dot(h.astype(BF16), wmg_ref[...]))
    o_mla = _dot(om_ref[...], wuv_ref[...]).astype(BF16)
    t = mg[:, :d] * _dot(on_ref[...], wbn_ref[...])
    t = t + mg[:, d:2 * d] * _dot(o_mla, wbm_ref[...])
    t = t + mg[:, 2 * d:] * _dot(of_ref[...], wbf_ref[...])
    o_ref[...] = x + gt_ref[0] * _dot(t.astype(BF16), wo_ref[...])


def _mix(x, sc, sh, gt, mod_index, lw, o_nsa, o_lat, o_fox, *, tm):
    n, d = x.shape
    row = lambda i: (i, 0)
    const2 = lambda i: (0, 0)
    ws = [lw["w_mg"], lw["w_br_n"], lw["w_uv"], lw["w_br_m"], lw["w_br_f"], lw["w_o"]]
    return pl.pallas_call(
        _mix_body,
        grid=(n // tm,),
        in_specs=[pl.BlockSpec((tm, d), row)]
        + [pl.BlockSpec((1,) + a.shape[1:], mod_index) for a in (sc, sh, gt)]
        + [pl.BlockSpec((1, d), const2)]
        + [pl.BlockSpec((tm, a.shape[1]), row) for a in (o_nsa, o_lat, o_fox)]
        + [pl.BlockSpec(w.shape, const2) for w in ws],
        out_specs=pl.BlockSpec((tm, d), row),
        out_shape=jax.ShapeDtypeStruct((n, d), F32),
        compiler_params=_cparams(1),
        name="mix",
    )(x, sc, sh, gt, lw["g_attn"], o_nsa, o_lat, o_fox, *ws)


def _gelu_tanh(x):
    return 0.5 * x * (1.0 + jnp.tanh(math.sqrt(2.0 / math.pi) * (x + 0.044715 * (x * x * x))))


def _ffn_body(x_ref, sc_ref, sh_ref, gt_ref, g_ref, pre_ref, win_ref, cw_ref, cb_ref, wout_ref,
              gf_ref, o_ref, tail_ref, hist_ref, *, tiles_per_seq, time_major, final_norm, d_ff):
    x = x_ref[...]
    tm = x.shape[0]
    i = pl.program_id(0)
    h = _rms(x, g_ref[...]) * (1.0 + sc_ref[0]) + sh_ref[0]
    ab = _dot(h.astype(BF16), win_ref[...])
    a = ab[:, :d_ff]
    b = ab[:, d_ff:]
    cw = cw_ref[...]
    if time_major:
        @pl.when(i % tiles_per_seq == 0)
        def _():
            hist_ref[0] = pre_ref[0]
            hist_ref[1] = pre_ref[1]
        a2 = hist_ref[0]
        a1 = hist_ref[1]
        hist_ref[0] = a1
        hist_ref[1] = a
        tail_ref[0] = a
    else:
        @pl.when(i % tiles_per_seq == 0)
        def _():
            hist_ref[0, 6:8] = pre_ref[0]
        prev = hist_ref[0, 6:8]
        rid = lax.broadcasted_iota(jnp.int32, (tm, 1), 0)
        a1 = jnp.where(rid == 0, prev[1:2], pltpu.roll(a, 1, 0))
        a2 = jnp.where(rid == 0, prev[0:1], jnp.where(rid == 1, prev[1:2], pltpu.roll(a, 2, 0)))
        hist_ref[0] = a[tm - 8:]
        tail_ref[0] = a[tm - 2:]
    conv = a2 * cw[0:1] + a1 * cw[1:2] + a * cw[2:3] + cb_ref[...]
    y = _dot((_gelu_tanh(conv) * b).astype(BF16), wout_ref[...])
    out = x + gt_ref[0] * y
    if final_norm:
        out = _rms(out, gf_ref[...])
    o_ref[...] = out


def _ffn(x, sc, sh, gt, mod_index, lw, prefix, g_final, *, tm, tiles_per_seq, time_major, final_norm):
    n, d = x.shape
    d_ff = lw["w_ffn_out"].shape[0]
    row = lambda i: (i, 0)
    const2 = lambda i: (0, 0)
    n_tiles = n // tm
    if time_major:
        pre_spec = pl.BlockSpec(prefix.shape, lambda i: (0, 0, 0))
        tail_spec = pl.BlockSpec((1, tm, d_ff),
                                 lambda i: (jnp.maximum(i - (tiles_per_seq - 2), 0), 0, 0))
        tail_shape = jax.ShapeDtypeStruct((2, tm, d_ff), F32)
        hist = pltpu.VMEM((2, tm, d_ff), F32)
    else:
        pre_spec = pl.BlockSpec((1, 2, d_ff), lambda i: (i // tiles_per_seq, 0, 0))
        tail_spec = pl.BlockSpec((1, 2, d_ff), lambda i: (i // tiles_per_seq, 0, 0))
        tail_shape = jax.ShapeDtypeStruct((n_tiles // tiles_per_seq, 2, d_ff), F32)
        hist = pltpu.VMEM((1, 8, d_ff), F32)
    return pl.pallas_call(
        functools.partial(_ffn_body, tiles_per_seq=tiles_per_seq, time_major=time_major,
                          final_norm=final_norm, d_ff=d_ff),
        grid=(n_tiles,),
        in_specs=[pl.BlockSpec((tm, d), row)]
        + [pl.BlockSpec((1,) + a.shape[1:], mod_index) for a in (sc, sh, gt)]
        + [pl.BlockSpec((1, d), const2), pre_spec,
           pl.BlockSpec(lw["w_ffn_in"].shape, const2),
           pl.BlockSpec((CONV_W, d_ff), const2), pl.BlockSpec((1, d_ff), const2),
           pl.BlockSpec(lw["w_ffn_out"].shape, const2), pl.BlockSpec((1, d), const2)],
        out_specs=[pl.BlockSpec((tm, d), row), tail_spec],
        out_shape=[jax.ShapeDtypeStruct((n, d), F32), tail_shape],
        scratch_shapes=[hist],
        compiler_params=_cparams(1),
        name="conv_ffn",
    )(x, sc, sh, gt, lw["g_ffn"], prefix, lw["w_ffn_in"], lw["conv_w"], lw["conv_b"],
      lw["w_ffn_out"], g_final)


def _pad_heads(w, heads, dh, scale=1.0):
    k = w.shape[0]
    w = (w * scale).reshape(k, heads, dh)
    return jnp.pad(w, ((0, 0), (0, 0), (0, LANE - dh))).reshape(k, heads * LANE)


def _pad_head_rows(w, heads, dh, offset):
    n = w.shape[1]
    w = w.reshape(heads, dh, n)
    return jnp.pad(w, ((0, 0), (offset, LANE - dh - offset), (0, 0))).reshape(heads * LANE, n)


def _prep_layer(l, p):
    d = p["w_in"].shape[1]
    w_in = p["w_in"][l]
    nsa_w = NSA_HEADS * NSA_DH
    fox_w = FOX_HEADS * FOX_DH
    q_rank = p["mla_g_q"].shape[1]
    kv_w = MLA_KV_RANK + MLA_ROPE
    splits = (nsa_w, 2 * NSA_DH, 2 * NSA_DH, 2 * NSA_DH, 3 * NSA_HEADS, q_rank, kv_w,
              fox_w, 2 * FOX_DH, FOX_HEADS, 3 * d)
    cuts = [int(c) for c in np.cumsum(splits)[:-1]]
    (w_nq, w_nc, w_ns, w_nw, w_ng, w_qd, w_kvd, w_fq, w_fkv, w_ff, w_mg) = jnp.split(w_in, cuts, axis=1)
    half = MLA_ROPE // 2
    w_small = jnp.pad(w_ng, ((0, 0), (0, LANE - w_ng.shape[1])))
    w_row = jnp.concatenate([_pad_heads(w_nq, NSA_HEADS, NSA_DH, NSA_SCALE),
                             _pad_heads(w_fq, FOX_HEADS, FOX_DH, FOX_SCALE),
                             w_qd, w_small], axis=1).astype(BF16)
    w_kr = w_kvd[:, MLA_KV_RANK:]
    w_kr_sw = jnp.concatenate([w_kr[:, half:], w_kr[:, :half]], axis=1)
    w_t = jnp.concatenate([w_nc, w_ns, w_nw, w_fkv, w_kvd, w_kr_sw, w_ff], axis=1).T.astype(BF16)
    w_uq = p["mla_w_uq"][l].reshape(q_rank, MLA_HEADS, MLA_NOPE + MLA_ROPE)
    uq_nope = w_uq[:, :, :MLA_NOPE].reshape(q_rank, -1)
    uq_rope = w_uq[:, :, MLA_NOPE:]
    uq_rope_sw = jnp.concatenate([uq_rope[:, :, half:], uq_rope[:, :, :half]], axis=2)
    w_uq2 = jnp.concatenate([uq_nope, uq_rope.reshape(q_rank, -1), uq_rope_sw.reshape(q_rank, -1)],
                            axis=1).astype(BF16)
    w_uk = p["mla_w_uk"][l]
    eye_h = jnp.eye(MLA_HEADS, dtype=F32)
    uk_bd = jnp.einsum("chd,hg->hdgc", w_uk, eye_h)
    uk_bd = jnp.pad(uk_bd, ((0, 0), (0, 0), (0, 0), (0, LANE - MLA_KV_RANK)))
    uk_bd = uk_bd.reshape(MLA_HEADS * MLA_NOPE, MLA_HEADS * LANE).astype(BF16)
    rr = jnp.arange(MLA_HEADS * MLA_ROPE)
    pp = (jnp.arange(MLA_HEADS * LANE)[None, :]
          == ((rr // MLA_ROPE) * LANE + MLA_KV_RANK + rr % MLA_ROPE)[:, None]).astype(BF16)
    w_uv = p["mla_w_uv"][l]
    uv_bd = jnp.einsum("chv,hg->hcgv", w_uv, eye_h)
    uv_bd = jnp.pad(uv_bd, ((0, 0), (0, LANE - MLA_KV_RANK), (0, 0), (0, 0)))
    uv_bd = uv_bd.reshape(MLA_HEADS * LANE, -1).astype(BF16)
    w_br = p["w_br"][l]
    mla_w = w_uv.shape[1] * w_uv.shape[2]
    return {
        "g_attn": p["g_attn"][l][None, :], "g_ffn": p["g_ffn"][l][None, :],
        "w_row": w_row, "w_t": w_t, "g_q": p["mla_g_q"][l][None, :], "w_uq": w_uq2, "w_uk": uk_bd,
        "pp": pp, "g_kv": p["mla_g_kv"][l][:, None], "b_f": p["fox_b_f"][l][:, None],
        "w_mg": w_mg.astype(BF16),
        "w_br_n": _pad_head_rows(w_br[:nsa_w], NSA_HEADS, NSA_DH, NSA_DH).astype(BF16),
        "w_uv": uv_bd,
        "w_br_m": w_br[nsa_w:nsa_w + mla_w].astype(BF16),
        "w_br_f": _pad_head_rows(w_br[nsa_w + mla_w:], FOX_HEADS, FOX_DH, FOX_DH).astype(BF16),
        "w_o": p["w_o"][l].astype(BF16),
        "w_ffn_in": p["w_ffn_in"][l].astype(BF16), "conv_w": p["conv_w"][l],
        "conv_b": p["conv_b"][l][None, :], "w_ffn_out": p["w_ffn_out"][l].astype(BF16),
    }


def _rope_tables(pos):
    half = MLA_ROPE // 2
    inv = ROPE_BASE ** (-jnp.arange(half, dtype=F32) / half)
    ang = pos.astype(F32)[:, None] * inv[None, :]
    cos, sin = jnp.cos(ang), jnp.sin(ang)
    c1 = jnp.concatenate([cos, cos], axis=1)
    c2 = jnp.concatenate([-sin, sin], axis=1)
    return c1.T, c2.T, jnp.tile(c1, (1, MLA_HEADS)), jnp.tile(c2, (1, MLA_HEADS))


def _page_copy(cache_ref, layer, page, buf, slot, p, rows, sem):
    dst = buf.at[slot, pl.ds(0, rows), pl.ds(pl.multiple_of(p * LANE, LANE), LANE)]
    return pltpu.make_async_copy(cache_ref.at[layer, page], dst, sem)


def _softmax_parts(parts):
    m = parts[0].max(axis=-1, keepdims=True)
    for s in parts[1:]:
        m = jnp.maximum(m, s.max(axis=-1, keepdims=True))
    ps = [jnp.exp(s - m) for s in parts]
    l = ps[0].sum(axis=-1, keepdims=True)
    for p in ps[1:]:
        l = l + p.sum(axis=-1, keepdims=True)
    return ps, l


def _paged_body(pt_ref, q_ref, knew_ref, *rest, layer, n_seq, n_pages, rows, heads, s_len, tk,
                with_decay):
    if with_decay:
        (lfn_ref, tri_s_ref, tri_ref, cache_ref, lcache_ref, o_ref, kbuf, lbuf, s_all, sem) = rest
    else:
        (cache_ref, o_ref, kbuf, s_all, sem) = rest
    b = pl.program_id(0)
    slot = b % 2
    past = n_pages * LANE
    sp = knew_ref.shape[1]

    def issue(seq, sl):
        def body(p, c):
            pg = pt_ref[seq, p]
            _page_copy(cache_ref, layer, pg, kbuf, sl, p, rows, sem.at[sl, 0]).start()
            if with_decay:
                pltpu.make_async_copy(lcache_ref.at[layer, pg], lbuf.at[sl, p], sem.at[sl, 1]).start()
            return c
        lax.fori_loop(0, n_pages, body, 0)

    def wait(sl):
        def body(p, c):
            _page_copy(cache_ref, layer, 0, kbuf, sl, 0, rows, sem.at[sl, 0]).wait()
            if with_decay:
                pltpu.make_async_copy(lcache_ref.at[layer, 0], lbuf.at[sl, 0], sem.at[sl, 1]).wait()
            return c
        lax.fori_loop(0, n_pages, body, 0)

    @pl.when(b == 0)
    def _():
        if rows < LANE:
            kbuf[:, rows:, :] = jnp.zeros((2, LANE - rows, past), F32)
        issue(0, 0)

    @pl.when(b + 1 < n_seq)
    def _():
        issue(b + 1, 1 - slot)

    wait(slot)

    q = q_ref[0]
    m_rows = q.shape[0]
    knew = knew_ref[0].astype(BF16)
    s_new = _dot_nt(q, knew)
    if with_decay:
        cin = jnp.dot(lbuf[slot].reshape(n_pages * 8, LANE), tri_ref[...],
                      preferred_element_type=F32, precision=lax.Precision.HIGHEST)
        run = jnp.zeros((8, 1), F32)
        negc = []
        for p in range(n_pages):
            cp = cin[p * 8:(p + 1) * 8]
            negc.append(-(cp + run))
            run = run + cp[:, LANE - 1:]
        cs_new = jnp.dot(lfn_ref[0], tri_s_ref[...], preferred_element_type=F32,
                         precision=lax.Precision.HIGHEST)
        s_new = s_new + jnp.concatenate([-(run + cs_new)] * s_len, axis=0)
    sq = jnp.right_shift(lax.broadcasted_iota(jnp.int32, (m_rows, sp), 0), _log2(heads))
    jj = lax.broadcasted_iota(jnp.int32, (m_rows, sp), 1)
    s_new = jnp.where(jj <= sq, s_new, NEG)
    n_chunks = past // tk
    ppc = tk // LANE
    for c in range(n_chunks):
        kt = kbuf[slot, :, c * tk:(c + 1) * tk].astype(BF16)
        s = _dot(q, kt)
        if with_decay:
            nc = jnp.concatenate(negc[c * ppc:(c + 1) * ppc], axis=1)
            s = s + jnp.concatenate([nc] * s_len, axis=0)
        s_all[:, c * tk:(c + 1) * tk] = s
    (p_all, p_new), l = _softmax_parts([s_all[...], s_new])
    acc = _dot(p_new.astype(BF16), knew)
    s_all[...] = p_all
    for c in range(n_chunks):
        kt = kbuf[slot, :, c * tk:(c + 1) * tk].astype(BF16)
        acc = acc + _dot_nt(s_all[:, c * tk:(c + 1) * tk].astype(BF16), kt)
    o_ref[0] = (acc / l).astype(BF16)


def _paged_attn(page_table, q, knew, cache_t, layer, *, heads, s_len, decay=None):
    n_seq, n_pages = page_table.shape
    rows = cache_t.shape[2]
    past = n_pages * LANE
    tk = min(past, 2048)
    m = q.shape[1]
    sp = knew.shape[1]
    with_decay = decay is not None
    per_b = lambda b, pt: (b, 0, 0)
    in_specs = [pl.BlockSpec((1, m, LANE), per_b), pl.BlockSpec((1, sp, LANE), per_b)]
    args = [q, knew]
    scratch = [pltpu.VMEM((2, LANE, past), F32)]
    if with_decay:
        lfn, lcache_t = decay
        tri_s = jnp.triu(jnp.ones((sp, sp), F32))
        tri = jnp.triu(jnp.ones((LANE, LANE), F32))
        in_specs += [pl.BlockSpec((1, 8, sp), per_b), pl.BlockSpec((sp, sp), lambda b, pt: (0, 0)),
                     pl.BlockSpec((LANE, LANE), lambda b, pt: (0, 0)),
                     pl.BlockSpec(memory_space=pl.ANY), pl.BlockSpec(memory_space=pl.ANY)]
        args += [lfn, tri_s, tri, cache_t, lcache_t]
        scratch.append(pltpu.VMEM((2, n_pages, 8, LANE), F32))
    else:
        in_specs.append(pl.BlockSpec(memory_space=pl.ANY))
        args.append(cache_t)
    scratch += [pltpu.VMEM((m, past), F32), pltpu.SemaphoreType.DMA((2, 2))]
    return pl.pallas_call(
        functools.partial(_paged_body, layer=layer, n_seq=n_seq, n_pages=n_pages, rows=rows,
                          heads=heads, s_len=s_len, tk=tk, with_decay=with_decay),
        grid_spec=pltpu.PrefetchScalarGridSpec(
            num_scalar_prefetch=1, grid=(n_seq,), in_specs=in_specs,
            out_specs=pl.BlockSpec((1, m, LANE), per_b), scratch_shapes=scratch),
        out_shape=jax.ShapeDtypeStruct((n_seq, m, LANE), BF16),
        compiler_params=_cparams(1),
        name="paged_attn_decay" if with_decay else "paged_attn",
    )(page_table, *args)


def _nsa_s1_body(pt_ref, q_ref, wnew_ref, win_ref, cb_ref, wb_ref, wnb_ref, amat_ref, pair_ref,
                 cache_ref, ocmp_ref, owin_ref, idx_ref, kbuf, sem, *, layer, n_seq, n_pages, s_len,
                 k_free):
    b = pl.program_id(0)
    slot = b % 2
    past = n_pages * LANE
    hh = NSA_HEADS

    def issue(seq, sl):
        def body(p, c):
            _page_copy(cache_ref, layer, pt_ref[seq, p], kbuf, sl, p, LANE, sem.at[sl]).start()
            return c
        lax.fori_loop(0, n_pages, body, 0)

    def wait(sl):
        def body(p, c):
            _page_copy(cache_ref, layer, 0, kbuf, sl, 0, LANE, sem.at[sl]).wait()
            return c
        lax.fori_loop(0, n_pages, body, 0)

    @pl.when(b == 0)
    def _():
        issue(0, 0)

    @pl.when(b + 1 < n_seq)
    def _():
        issue(b + 1, 1 - slot)

    wait(slot)
    q = q_ref[0]
    tc = amat_ref.shape[0]
    means = [jnp.dot(kbuf[slot, :, c * tc:(c + 1) * tc], amat_ref[...], preferred_element_type=F32,
                     precision=lax.Precision.HIGHEST) for c in range(past // tc)]
    kvc = jnp.concatenate(means, axis=1).astype(BF16)
    s = _dot(q, kvc) + cb_ref[...]
    e = jnp.exp(s - jnp.max(s, axis=-1, keepdims=True))
    p = e / jnp.sum(e, axis=-1, keepdims=True)
    ocmp_ref[0] = _dot_nt(p.astype(BF16), kvc)
    ps = jnp.dot(p, pair_ref[...], preferred_element_type=F32, precision=lax.Precision.HIGHEST)
    score = ps[0:s_len]
    for h in range(1, hh):
        score = score + ps[h * s_len:(h + 1) * s_len]
    n_past = score.shape[1]
    lane = lax.broadcasted_iota(jnp.int32, score.shape, 1).astype(F32)
    work = jnp.where((lane == 0.0) | (lane == n_past - 1.0), -jnp.inf, score)
    out_lane = lax.broadcasted_iota(jnp.int32, (s_len, LANE), 1)
    idx_out = jnp.zeros((s_len, LANE), F32)
    for r in range(k_free):
        mx = jnp.max(work, axis=-1, keepdims=True)
        idx = jnp.min(jnp.where(work == mx, lane, float(n_past)), axis=-1, keepdims=True)
        idx_out = jnp.where(out_lane == r, idx, idx_out)
        work = jnp.where(lane == idx, -jnp.inf, work)
    idx_ref[0] = idx_out.astype(jnp.int32)
    wst = win_ref[0, 0].astype(BF16)
    wnew = wnew_ref[0].astype(BF16)
    (p_w, p_n), l = _softmax_parts([_dot(q, wst) + wb_ref[...], _dot_nt(q, wnew) + wnb_ref[...]])
    owin_ref[0] = (_dot_nt(p_w.astype(BF16), wst) + _dot(p_n.astype(BF16), wnew)) / l


def _nsa_s1(page_table, q, wnew, win_t, cache_t, layer, tables, *, s_len, k_free):
    n_seq, n_pages = page_table.shape
    past = n_pages * LANE
    cb, wb, wnb = tables
    m = q.shape[1]
    sp = wnew.shape[1]
    wlen = win_t.shape[3]
    tc = min(past, CMP_BLOCK * LANE)
    amat = (jnp.arange(tc)[:, None] // CMP_BLOCK == jnp.arange(tc // CMP_BLOCK)[None, :]).astype(F32)
    amat = amat / CMP_BLOCK
    nc = past // CMP_BLOCK
    ratio = SEL_BLOCK // CMP_BLOCK
    pair = (jnp.arange(nc)[:, None] // ratio == jnp.arange(nc // ratio)[None, :]).astype(F32)
    per_b = lambda b, pt: (b, 0, 0)
    c2 = lambda b, pt: (0, 0)
    return pl.pallas_call(
        functools.partial(_nsa_s1_body, layer=layer, n_seq=n_seq, n_pages=n_pages, s_len=s_len,
                          k_free=k_free),
        grid_spec=pltpu.PrefetchScalarGridSpec(
            num_scalar_prefetch=1, grid=(n_seq,),
            in_specs=[pl.BlockSpec((1, m, LANE), per_b), pl.BlockSpec((1, sp, LANE), per_b),
                      pl.BlockSpec((1, 1, LANE, wlen), lambda b, pt: (layer, b, 0, 0)),
                      pl.BlockSpec(cb.shape, c2), pl.BlockSpec(wb.shape, c2),
                      pl.BlockSpec(wnb.shape, c2), pl.BlockSpec(amat.shape, c2),
                      pl.BlockSpec(pair.shape, c2), pl.BlockSpec(memory_space=pl.ANY)],
            out_specs=[pl.BlockSpec((1, m, LANE), per_b), pl.BlockSpec((1, m, LANE), per_b),
                       pl.BlockSpec((1, s_len, LANE), per_b)],
            scratch_shapes=[pltpu.VMEM((2, LANE, past), F32), pltpu.SemaphoreType.DMA((2,))]),
        out_shape=[jax.ShapeDtypeStruct((n_seq, m, LANE), F32),
                   jax.ShapeDtypeStruct((n_seq, m, LANE), F32),
                   jax.ShapeDtypeStruct((n_seq, s_len, LANE), jnp.int32)],
        compiler_params=_cparams(1),
        name="nsa_sample_cmp",
    )(page_table, q, wnew, win_t, cb, wb, wnb, amat, pair, cache_t)


def _nsa_s2_body(pt_ref, idx_ref, q_ref, snew_ref, g_ref, ocmp_ref, owin_ref, lb_ref, nb_ref,
                 cache_ref, o_ref, kbuf, sem, *, layer, n_seq, n_pages, s_len, k_free):
    b = pl.program_id(0)
    slot = b % 2
    hh = NSA_HEADS
    n_own = s_len * k_free
    n_slots = n_own + 2
    per_page = LANE // SEL_BLOCK

    def block_of(seq, j):
        if j < n_own:
            return idx_ref[(seq * s_len + j // k_free) * LANE + j % k_free]
        return 0 if j == n_own else n_pages * per_page - 1

    def issue(seq, sl):
        for j in range(n_slots):
            blk = block_of(seq, j)
            pg = pt_ref[seq, blk >> _log2(per_page)]
            _page_copy(cache_ref, layer, pg, kbuf, sl, j, LANE, sem.at[sl]).start()

    def wait(sl):
        for j in range(n_slots):
            _page_copy(cache_ref, layer, 0, kbuf, sl, 0, LANE, sem.at[sl]).wait()

    @pl.when(b == 0)
    def _():
        issue(0, 0)

    @pl.when(b + 1 < n_seq)
    def _():
        issue(b + 1, 1 - slot)

    wait(slot)
    q = q_ref[0]
    m_rows = q.shape[0]
    kt = kbuf[slot].astype(BF16)
    s = _dot(q, kt)
    assert s_len & (s_len - 1) == 0
    row_s = jnp.bitwise_and(lax.broadcasted_iota(jnp.int32, (m_rows, LANE), 0), s_len - 1)
    lane_half = jnp.right_shift(lax.broadcasted_iota(jnp.int32, (m_rows, LANE), 1), _log2(SEL_BLOCK))
    bias = []
    last_page = (n_pages - 1) * per_page
    for j in range(n_slots):
        blk = block_of(b, j)
        ok = lane_half == (blk & (per_page - 1))
        if j < n_own:
            ok = ok & (row_s == j // k_free)
            near = jnp.where(blk >= last_page, lb_ref[...], 0.0)
        else:
            near = lb_ref[...] if j == n_slots - 1 else 0.0
        bias.append(jnp.where(ok, near, NEG))
    s = s + jnp.concatenate(bias, axis=1)
    snew = snew_ref[0].astype(BF16)
    (p_s, p_n), l = _softmax_parts([s, _dot_nt(q, snew) + nb_ref[...]])
    o_sel = (_dot_nt(p_s.astype(BF16), kt) + _dot(p_n.astype(BF16), snew)) / l
    g = g_ref[0]
    o = g[:, 0:1] * ocmp_ref[0] + g[:, 1:2] * o_sel + g[:, 2:3] * owin_ref[0]
    o_ref[0] = o.astype(BF16)


def _nsa_s2(page_table, idx, q, snew, gates, o_cmp, o_win, cache_t, layer, tables, *, s_len, k_free):
    n_seq, n_pages = page_table.shape
    lb, nb = tables
    m = q.shape[1]
    sp = snew.shape[1]
    n_slots = s_len * k_free + 2
    per_b = lambda b, pt, ix: (b, 0, 0)
    c2 = lambda b, pt, ix: (0, 0)
    return pl.pallas_call(
        functools.partial(_nsa_s2_body, layer=layer, n_seq=n_seq, n_pages=n_pages, s_len=s_len,
                          k_free=k_free),
        grid_spec=pltpu.PrefetchScalarGridSpec(
            num_scalar_prefetch=2, grid=(n_seq,),
            in_specs=[pl.BlockSpec((1, m, LANE), per_b), pl.BlockSpec((1, sp, LANE), per_b),
                      pl.BlockSpec((1, m, LANE), per_b), pl.BlockSpec((1, m, LANE), per_b),
                      pl.BlockSpec((1, m, LANE), per_b), pl.BlockSpec(lb.shape, c2),
                      pl.BlockSpec(nb.shape, c2), pl.BlockSpec(memory_space=pl.ANY)],
            out_specs=pl.BlockSpec((1, m, LANE), per_b),
            scratch_shapes=[pltpu.VMEM((2, LANE, n_slots * LANE), F32),
                            pltpu.SemaphoreType.DMA((2,))]),
        out_shape=jax.ShapeDtypeStruct((n_seq, m, LANE), BF16),
        compiler_params=_cparams(1),
        name="nsa_sample_sel",
    )(page_table, idx.reshape(-1), q, snew, gates, o_cmp, o_win, lb, nb, cache_t)


def _nsa_sample_tables(rel_table, past, s_len, sp, wlen):
    hh = NSA_HEADS
    far = rel_table[REL_BUCKETS - 1].astype(F32)
    qpos = past + jnp.arange(s_len)
    rows = lambda a: a.reshape(hh * s_len, a.shape[-1])
    c_end = jnp.arange(past // CMP_BLOCK) * CMP_BLOCK + CMP_BLOCK - 1
    cb = rows(_rel_bias_t(rel_table, qpos[:, None] - c_end[None, :]))
    dist = qpos[:, None] - (past - wlen + jnp.arange(wlen))[None, :]
    wb = rows(jnp.where(dist < WINDOW, _rel_bias_t(rel_table, dist), NEG))
    dist = jnp.arange(s_len)[:, None] - jnp.arange(sp)[None, :]
    newb = _rel_bias_t(rel_table, dist)
    wnb = rows(jnp.where(dist >= 0, newb, NEG))
    assert LANE >= REL_MAX_DIST
    dist = qpos[:, None] - (past - LANE + jnp.arange(LANE))[None, :]
    lb = rows(_rel_bias_t(rel_table, dist) - far[:, None, None])
    dist = jnp.arange(s_len)[:, None] - jnp.arange(sp)[None, :]
    nb = rows(jnp.where(dist >= 0, newb - far[:, None, None], NEG))
    return (cb, wb, wnb), (lb, nb)


TQ = 128
TK = 512
TM_PROMPT = 256
NEW_PAD = 16


def _fm_to_rows(a, mid):
    lead = a.shape[:-2]
    n = a.shape[-1]
    a = a.reshape(lead + mid + (n,))
    return jnp.moveaxis(a, -1, len(lead))


def kernel(x_prompt, x_sample, cache_nsa_cmp, cache_nsa_sel, state_nsa_win, cache_mla, cache_fox_kv, cache_fox_logf, state_ffn_conv, page_table, c_prompt, c_sample, rel_table, w_ada, b_ada, g_attn, g_ffn, w_in, mla_g_q, mla_w_uq, mla_g_kv, mla_w_uk, mla_w_uv, fox_b_f, w_br, w_o, w_ffn_in, conv_w, conv_b, w_ffn_out, g_final):
    b, t, d = x_prompt.shape
    db, s_len, _ = x_sample.shape
    depth = w_in.shape[0]
    pool = cache_nsa_cmp.shape[1]
    n_pages = page_table.shape[1]
    past = n_pages * LANE
    d_ff = w_ffn_out.shape[1]
    wlen = state_nsa_win.shape[2]
    assert cache_nsa_cmp.shape[2] == LANE and TQ >= REL_MAX_DIST and t % TK == 0
    params = dict(w_in=w_in, mla_g_q=mla_g_q, mla_w_uq=mla_w_uq, mla_g_kv=mla_g_kv, mla_w_uk=mla_w_uk,
                  mla_w_uv=mla_w_uv, fox_b_f=fox_b_f, w_br=w_br, w_o=w_o, w_ffn_in=w_ffn_in,
                  conv_w=conv_w, conv_b=conv_b, w_ffn_out=w_ffn_out, g_attn=g_attn, g_ffn=g_ffn)

    n_c = b + db
    c_all = jnp.pad(jnp.concatenate([c_prompt, c_sample], axis=0), ((0, -n_c % 8), (0, 0)))
    mods = _adaln(c_all, w_ada, b_ada)

    cmp_t = jnp.transpose(cache_nsa_cmp, (0, 1, 3, 4, 2)).reshape(depth, pool, LANE, LANE)
    sel_t = jnp.transpose(cache_nsa_sel, (0, 1, 3, 4, 2)).reshape(depth, pool, LANE, LANE)
    fkv_t = jnp.transpose(cache_fox_kv, (0, 1, 3, 4, 5, 2)).reshape(depth, pool, LANE, LANE)
    mla_t = jnp.transpose(cache_mla, (0, 1, 3, 2))
    lf_t = jnp.transpose(cache_fox_logf, (0, 1, 3, 2))
    win_t = jnp.transpose(state_nsa_win, (0, 1, 3, 4, 2)).reshape(depth, db, LANE, wlen)
    conv_pre_s = jnp.transpose(state_ffn_conv, (0, 2, 1, 3))
    conv_pre_p = jnp.zeros((b, CONV_W - 1, d_ff), F32)

    rope_p = _rope_tables(jnp.arange(t))
    rope_s = _rope_tables(jnp.repeat(past + jnp.arange(s_len), db))
    tabs_p = _nsa_prompt_tables(rel_table, t, TQ)
    tabs_s1, tabs_s2 = _nsa_sample_tables(rel_table, past, s_len, NEW_PAD, wlen)
    n_sel_s = -(-(past + s_len) // SEL_BLOCK)
    k_free = min(N_SEL, n_sel_s) - 3

    tpb = t // TM_PROMPT
    idx_p = lambda i: (i // tpb, 0, 0)
    idx_s = lambda i: (0, 0, 0)
    xp = x_prompt.reshape(b * t, d)
    xs = jnp.transpose(x_sample, (1, 0, 2)).reshape(s_len * db, d)

    def new_rows(a):
        return jnp.pad(jnp.transpose(a, (2, 0, 1)), ((0, 0), (0, NEW_PAD - s_len), (0, 0)))

    def q_rows(q, heads, head_major):
        q = q.reshape(s_len, db, heads, LANE)
        q = jnp.transpose(q, (1, 2, 0, 3) if head_major else (1, 0, 2, 3))
        return q.reshape(db, s_len * heads, LANE)

    def o_rows(o, heads, head_major):
        o = o.reshape((db, heads, s_len, LANE) if head_major else (db, s_len, heads, LANE))
        o = jnp.transpose(o, (2, 0, 1, 3) if head_major else (1, 0, 2, 3))
        return o.reshape(s_len * db, heads * LANE)

    rows_p, rows_s = [], []
    for l in range(depth):
        lw = _prep_layer(l, params)
        m6 = [mods[l][:, k * d:(k + 1) * d] for k in range(6)]
        sh_a, sc_a, gt_a, sh_f, sc_f, gt_f = [a[:b][:, None, :] for a in m6]
        sh_as, sc_as, gt_as, sh_fs, sc_fs, gt_fs = [a[b:n_c][None] for a in m6]
        last = l == depth - 1

        pre = _pre_proj(xp, sc_a, sh_a, idx_p, lw, rope_p, n_groups=b, tiles_per_group=tpb,
                        n_pos_tiles=tpb, tm=TM_PROMPT, with_cumsum=True)
        nc = t // CMP_BLOCK
        kvct = _block_means_t(pre["ct"])
        kvct = jnp.transpose(kvct.reshape(b, LANE, nc // 2, 2), (0, 1, 3, 2)).reshape(b, LANE, nc)
        o_nsa = _nsa_prompt(pre["qn"], kvct, pre["st"], pre["wt"], pre["gates"], tabs_p, tq=TQ)
        o_lat = _causal_attn(pre["qm"], pre["mt"], None, heads=MLA_HEADS, tq=TQ, tk=TK)
        o_fox = _causal_attn(pre["qf"], pre["ft"], pre["nct"], heads=FOX_HEADS, tq=TQ, tk=TK)
        x1 = _mix(xp, sc_a, sh_a, gt_a, idx_p, lw, o_nsa, o_lat, o_fox, tm=TM_PROMPT)
        xp, tail_p = _ffn(x1, sc_f, sh_f, gt_f, idx_p, lw, conv_pre_p, g_final[None, :], tm=TM_PROMPT,
                          tiles_per_seq=tpb, time_major=False, final_norm=last)
        wkeep = min(WINDOW, t)
        rows_p.append((_fm_to_rows(pre["ct"], (2, NSA_DH)), _fm_to_rows(pre["st"], (2, NSA_DH)),
                       _fm_to_rows(pre["wt"][:, :, t - wkeep:], (2, NSA_DH)),
                       _fm_to_rows(pre["mt"][:, :MLA_KV_RANK + MLA_ROPE], (MLA_KV_RANK + MLA_ROPE,)),
                       _fm_to_rows(pre["ft"], (2, 1, FOX_DH)), _fm_to_rows(pre["lft"], (FOX_HEADS,)),
                       tail_p))

        pre_s = _pre_proj(xs, sc_as, sh_as, idx_s, lw, rope_s, n_groups=s_len, tiles_per_group=1,
                          n_pos_tiles=s_len, tm=db, with_cumsum=False)
        qn_s = q_rows(pre_s["qn"], NSA_HEADS, True)
        gates_s = pre_s["gates"][:, :3 * NSA_HEADS].reshape(s_len, db, NSA_HEADS, 3)
        gates_s = jnp.transpose(gates_s, (1, 2, 0, 3)).reshape(db, NSA_HEADS * s_len, 3)
        gates_s = jnp.pad(gates_s, ((0, 0), (0, 0), (0, LANE - 3)))
        o_cmp, o_win, idx = _nsa_s1(page_table, qn_s, new_rows(pre_s["wt"]), win_t, cmp_t, l, tabs_s1,
                                    s_len=s_len, k_free=k_free)
        o_nsa_s = _nsa_s2(page_table, idx, qn_s, new_rows(pre_s["st"]), gates_s, o_cmp, o_win, sel_t, l,
                          tabs_s2, s_len=s_len, k_free=k_free)
        o_lat_s = _paged_attn(page_table, q_rows(pre_s["qm"], MLA_HEADS, False), new_rows(pre_s["mt"]),
                              mla_t, l, heads=MLA_HEADS, s_len=s_len)
        lfn = jnp.pad(jnp.transpose(pre_s["lft"], (2, 1, 0)), ((0, 0), (0, 0), (0, NEW_PAD - s_len)))
        o_fox_s = _paged_attn(page_table, q_rows(pre_s["qf"], FOX_HEADS, False), new_rows(pre_s["ft"]),
                              fkv_t, l, heads=FOX_HEADS, s_len=s_len, decay=(lfn, lf_t))
        x1s = _mix(xs, sc_as, sh_as, gt_as, idx_s, lw, o_rows(o_nsa_s, NSA_HEADS, True),
                   o_rows(o_lat_s, MLA_HEADS, False), o_rows(o_fox_s, FOX_HEADS, False), tm=db)
        xs, tail_s = _ffn(x1s, sc_fs, sh_fs, gt_fs, idx_s, lw, conv_pre_s[l], g_final[None, :], tm=db,
                          tiles_per_seq=s_len, time_major=True, final_norm=last)
        wfull = jnp.concatenate([win_t[l], jnp.transpose(pre_s["wt"], (2, 1, 0))], axis=2)
        wfull = wfull[:, :, wfull.shape[2] - min(WINDOW, past + s_len):]
        fm_s = lambda a, mid: jnp.swapaxes(_fm_to_rows(a, mid), 0, 1)
        rows_s.append((fm_s(pre_s["ct"], (2, NSA_DH)), fm_s(pre_s["st"], (2, NSA_DH)),
                       _fm_to_rows(wfull, (2, NSA_DH)),
                       fm_s(pre_s["mt"][:, :MLA_KV_RANK + MLA_ROPE], (MLA_KV_RANK + MLA_ROPE,)),
                       fm_s(pre_s["ft"], (2, 1, FOX_DH)), fm_s(pre_s["lft"], (FOX_HEADS,)),
                       jnp.transpose(tail_s, (1, 0, 2))))

    y_prompt = xp.reshape(b, t, d)
    y_sample = jnp.transpose(xs.reshape(s_len, db, d), (1, 0, 2))
    outs_p = [jnp.stack(a) for a in zip(*rows_p)]
    outs_s = [jnp.stack(a) for a in zip(*rows_s)]
    return (y_prompt, y_sample, *outs_p, *outs_s)
```

```python
import functools
import math

import numpy as np
import jax
import jax.numpy as jnp
from jax import lax
from jax.experimental import pallas as pl
from jax.experimental.pallas import tpu as pltpu

F32 = jnp.float32
BF16 = jnp.bfloat16

NSA_HEADS = 4
NSA_DH = 64
CMP_BLOCK = 32
SEL_BLOCK = 64
N_SEL = 16
WINDOW = 512
MLA_HEADS = 4
MLA_NOPE = 64
MLA_ROPE = 32
MLA_KV_RANK = 64
ROPE_BASE = 10000.0
FOX_HEADS = 8
FOX_DH = 64
REL_BUCKETS = 32
REL_MAX_DIST = 128
CONV_W = 3
EPS = 1e-6
NEG = -1e30
LOG2E = math.log2(math.e)
LANE = 128
VMEM_LIMIT = 56 * 1024 * 1024

NSA_SCALE = NSA_DH ** -0.5
MLA_SCALE = (MLA_NOPE + MLA_ROPE) ** -0.5
FOX_SCALE = FOX_DH ** -0.5

ONES_ROW = 64
ONES_ROWS = 8
DECAY_PARTS = 3

_NT = (((1,), (1,)), ((), ()))


def _cparams(n_axes):
    return pltpu.CompilerParams(dimension_semantics=("arbitrary",) * n_axes,
                                vmem_limit_bytes=VMEM_LIMIT)


def _dot(a, b):
    return jnp.dot(a, b, preferred_element_type=F32)


def _dot_nt(a, b):
    return lax.dot_general(a, b, _NT, preferred_element_type=F32)


def _dot_f32(a, b):
    return jnp.dot(a, b, preferred_element_type=F32, precision=lax.Precision.HIGHEST)


def _rms(x, g):
    return x * lax.rsqrt(jnp.mean(x * x, axis=-1, keepdims=True) + EPS) * g


def _log2(n):
    assert n & (n - 1) == 0, n
    return n.bit_length() - 1


def _ada_body(c_ref, w_ref, b_ref, o_ref):
    c = c_ref[...]
    sc = (c * jax.nn.sigmoid(c)).astype(BF16)
    o_ref[0] = _dot(sc, w_ref[0].astype(BF16)) + b_ref[0]


def _adaln(c_all, w_ada, b_ada):
    depth, d, n6 = w_ada.shape
    rows = c_all.shape[0]
    tn = 512
    return pl.pallas_call(
        _ada_body,
        grid=(depth, n6 // tn),
        in_specs=[pl.BlockSpec((rows, d), lambda l, n: (0, 0)),
                  pl.BlockSpec((1, d, tn), lambda l, n: (l, 0, n)),
                  pl.BlockSpec((1, 1, tn), lambda l, n: (l, 0, n))],
        out_specs=pl.BlockSpec((1, rows, tn), lambda l, n: (l, 0, n)),
        out_shape=jax.ShapeDtypeStruct((depth, rows, n6), F32),
        compiler_params=_cparams(2),
        name="adaln",
    )(c_all, w_ada, b_ada.reshape(depth, 1, n6))


_T_C, _T_S, _T_W, _T_F, _T_M, _T_LF, _T_G, _T_QD = 0, 128, 256, 384, 512, 640, 648, 664
_T_QN = _T_QD + 256
_T_QF = _T_QN + NSA_HEADS * LANE
_T_END = _T_QF + FOX_HEADS * LANE
_R_KS, _R_KW, _R_KF, _R_KM, _R_KMS, _R_C, _R_LF, _R_END = 0, 128, 256, 384, 512, 640, 768, 896


def _value_tile(vt):
    n = vt.shape[1]
    return jnp.concatenate([vt, jnp.ones((ONES_ROWS, n), F32),
                            jnp.zeros((LANE - ONES_ROW - ONES_ROWS, n), F32)], axis=0).astype(BF16)


def _pre_body(x_ref, sc_ref, sh_ref, g_ref, wrow_ref, wt_ref, gq_ref, wuq_ref, wuk_ref, pp_ref,
              gkv_ref, gkvr_ref, bf_ref, bfr_ref, aug_ref, cs1_ref, cs2_ref, csq1_ref, csq2_ref,
              cr1_ref, cr2_ref, tril_ref, avg_ref,
              ct_ref, st_ref, wtt_ref, ft_ref, mt_ref, lf_ref, gt_ref, qn_ref, qf_ref, qm_ref,
              vs_ref, vw_ref, vf_ref, vm_ref, ks_ref, kw_ref, kf_ref, km_ref, kvc_ref,
              carry_ref, *, tiles_per_seq, with_keys):
    x = x_ref[...]
    tm = x.shape[0]
    h = _rms(x, g_ref[...]) * (1.0 + sc_ref[0]) + sh_ref[0]
    hb = h.astype(BF16)

    pt = _dot_nt(wt_ref[...], hb)
    ct_ref[0] = pt[_T_C:_T_C + 128]
    st_ref[0] = pt[_T_S:_T_S + 128]
    wtt_ref[0] = pt[_T_W:_T_W + 128]
    ft_ref[0] = pt[_T_F:_T_F + 128]
    ckv = pt[_T_M:_T_M + 64]
    ckv = ckv * lax.rsqrt(jnp.mean(ckv * ckv, axis=0, keepdims=True) + EPS) * gkv_ref[...]
    krot = pt[_T_M + 64:_T_M + 96] * cs1_ref[...] + pt[_T_M + 96:_T_M + 128] * cs2_ref[...]
    mt_ref[0, 0:64] = ckv
    mt_ref[0, 64:96] = krot
    mt_ref[0, 96:128] = jnp.zeros_like(krot)
    lf_ref[0] = jax.nn.log_sigmoid(pt[_T_LF:_T_LF + 8] + bf_ref[...])
    gt_ref[0] = jax.nn.sigmoid(pt[_T_G:_T_G + 16])
    qn_ref[0] = pt[_T_QN:_T_QF].astype(BF16)
    qf_ref[0] = (pt[_T_QF:_T_END] + aug_ref[...]).astype(BF16)
    qd = pt[_T_QD:_T_QD + 256]
    qd = qd * lax.rsqrt(jnp.mean(qd * qd, axis=0, keepdims=True) + EPS) * gq_ref[...]
    q = _dot(wuq_ref[...], qd.astype(BF16))
    qrot = q[256:384] * csq1_ref[...] + q[384:512] * csq2_ref[...]
    qm = _dot(wuk_ref[...], q[:256].astype(BF16)) * (MLA_SCALE * LOG2E)
    qm = qm + _dot(pp_ref[...], (qrot * (MLA_SCALE * LOG2E)).astype(BF16))
    qm_ref[0] = qm.astype(BF16)
    vs_ref[0] = _value_tile(pt[_T_S + 64:_T_S + 128])
    vw_ref[0] = _value_tile(pt[_T_W + 64:_T_W + 128])
    vf_ref[0] = _value_tile(pt[_T_F + 64:_T_F + 128])
    vm_ref[0] = _value_tile(ckv)

    if with_keys:
        pr = _dot(hb, wrow_ref[...])
        lane = lax.broadcasted_iota(jnp.int32, (tm, LANE), 1)
        t_in_seq = (pl.program_id(0) % tiles_per_seq) * tm
        pos = t_in_seq + lax.broadcasted_iota(jnp.int32, (tm, LANE), 0)
        blk_ind = jnp.where(lane == jnp.right_shift(pos, _log2(SEL_BLOCK)), 1.0, 0.0)
        ks_ref[:, 0:LANE] = pr[:, _R_KS:_R_KS + LANE].astype(BF16)
        ks_ref[:, LANE:2 * LANE] = blk_ind.astype(BF16)
        kw_ref[...] = pr[:, _R_KW:_R_KW + LANE].astype(BF16)
        kvd = pr[:, _R_KM:_R_KM + LANE]
        is_c = lane < MLA_KV_RANK
        ms = jnp.sum(jnp.where(is_c, kvd * kvd, 0.0), axis=1, keepdims=True) / MLA_KV_RANK
        km = jnp.where(is_c, kvd * lax.rsqrt(ms + EPS) * gkvr_ref[...],
                       kvd * cr1_ref[...] + pr[:, _R_KMS:_R_KMS + LANE] * cr2_ref[...])
        km_ref[...] = km.astype(BF16)
        @pl.when(pl.program_id(0) % tiles_per_seq == 0)
        def _():
            carry_ref[...] = jnp.zeros_like(carry_ref)
        lfr = jnp.where(lane < FOX_HEADS, jax.nn.log_sigmoid(pr[:, _R_LF:_R_LF + LANE] + bfr_ref[...]),
                        0.0)
        csum = _dot_f32(tril_ref[...], lfr) + carry_ref[0:1]
        carry_ref[...] = jnp.broadcast_to(csum[tm - 1:], carry_ref.shape)
        kf = pr[:, _R_KF:_R_KF + LANE]
        rest = -csum * LOG2E
        for part in range(DECAY_PARTS):
            term = rest.astype(BF16).astype(F32)
            rest = rest - term
            kf = kf + pltpu.roll(term, FOX_DH + part * FOX_HEADS, 1)
        kf_ref[...] = kf.astype(BF16)
        kvc_ref[...] = _dot_f32(avg_ref[...], pr[:, _R_C:_R_C + LANE])
    else:
        ks_ref[...] = jnp.zeros(ks_ref.shape, BF16)
        kw_ref[...] = jnp.zeros(kw_ref.shape, BF16)
        kf_ref[...] = jnp.zeros(kf_ref.shape, BF16)
        km_ref[...] = jnp.zeros(km_ref.shape, BF16)
        kvc_ref[...] = jnp.zeros(kvc_ref.shape, F32)


def _pre_proj(x, sc, sh, mod_index, lw, pos_tabs, *, n_groups, tiles_per_group, n_pos_tiles, tm,
              with_keys):
    n, d = x.shape
    cs1, cs2, csq1, csq2, cr1, cr2 = pos_tabs
    n_tiles = n // tm
    tpg = tiles_per_group
    tril = jnp.tril(jnp.ones((tm, tm), F32))
    nmean = 8
    avg = (jnp.arange(tm)[None, :] // CMP_BLOCK == jnp.arange(nmean)[:, None]).astype(F32) / CMP_BLOCK
    row = lambda i: (i, 0)
    const2 = lambda i: (0, 0)
    tcol = lambda i: (i // tpg, 0, i % tpg)
    pcol = lambda i: (0, i % n_pos_tiles)
    prow = lambda i: (i % n_pos_tiles, 0)
    ncols = tpg * tm

    def tspec(r):
        return pl.BlockSpec((1, r, tm), tcol)

    def tshape(r, dt=F32):
        return jax.ShapeDtypeStruct((n_groups, r, ncols), dt)

    nq = NSA_HEADS * LANE
    nf = FOX_HEADS * LANE
    consts = [lw["g_attn"], lw["w_row"], lw["w_t"], lw["g_q"], lw["w_uq"], lw["w_uk"], lw["pp"],
              lw["g_kv"], lw["g_kv_row"], lw["b_f"], lw["b_f_row"], lw["q_aug"]]
    outs = pl.pallas_call(
        functools.partial(_pre_body, tiles_per_seq=tpg, with_keys=with_keys),
        grid=(n_tiles,),
        in_specs=[pl.BlockSpec((tm, d), row),
                  pl.BlockSpec((1,) + sc.shape[1:], mod_index),
                  pl.BlockSpec((1,) + sh.shape[1:], mod_index)]
        + [pl.BlockSpec(c.shape, const2) for c in consts]
        + [pl.BlockSpec((32, tm), pcol), pl.BlockSpec((32, tm), pcol),
           pl.BlockSpec((128, tm), pcol), pl.BlockSpec((128, tm), pcol),
           pl.BlockSpec((tm, 128), prow), pl.BlockSpec((tm, 128), prow),
           pl.BlockSpec((tm, tm), const2), pl.BlockSpec((nmean, tm), const2)],
        out_specs=[tspec(128), tspec(128), tspec(128), tspec(128), tspec(128), tspec(8), tspec(16),
                   tspec(nq), tspec(nf), tspec(nq), tspec(128), tspec(128), tspec(128), tspec(128),
                   pl.BlockSpec((tm, 2 * LANE), row), pl.BlockSpec((tm, LANE), row),
                   pl.BlockSpec((tm, LANE), row), pl.BlockSpec((tm, LANE), row),
                   pl.BlockSpec((nmean, LANE), row)],
        out_shape=[tshape(128), tshape(128), tshape(128), tshape(128), tshape(128), tshape(8),
                   tshape(16), tshape(nq, BF16), tshape(nf, BF16), tshape(nq, BF16),
                   tshape(128, BF16), tshape(128, BF16), tshape(128, BF16), tshape(128, BF16),
                   jax.ShapeDtypeStruct((n, 2 * LANE), BF16), jax.ShapeDtypeStruct((n, LANE), BF16),
                   jax.ShapeDtypeStruct((n, LANE), BF16), jax.ShapeDtypeStruct((n, LANE), BF16),
                   jax.ShapeDtypeStruct((n_tiles * nmean, LANE), F32)],
        scratch_shapes=[pltpu.VMEM((8, LANE), F32)],
        compiler_params=_cparams(1),
        name="pre_proj",
    )(x, sc, sh, *consts, cs1, cs2, csq1, csq2, cr1, cr2, tril, avg)
    keys = ("ct", "st", "wt", "ft", "mt", "lft", "gt", "qnt", "qft", "qmt", "vst", "vwt", "vft", "vmt",
            "ks", "kw", "kf", "km", "kvc")
    return dict(zip(keys, outs))


def _tflash_init(m_ref, acc_ref):
    m_ref[...] = jnp.full(m_ref.shape, NEG, F32)
    acc_ref[...] = jnp.zeros(acc_ref.shape, F32)


def _tflash_update(s, vt, m_ref, acc_ref):
    m_old = m_ref[...]
    m_new = jnp.maximum(m_old, jnp.max(s, axis=0, keepdims=True))
    p = jnp.exp2(s - m_new).astype(BF16)
    acc_ref[...] = jnp.exp2(m_old - m_new) * acc_ref[...] + _dot(vt, p)
    m_ref[...] = m_new


def _tflash_out(acc, heads, tq):
    o = acc / acc[ONES_ROW:ONES_ROW + 1]
    return jnp.concatenate([o[:, h * tq:(h + 1) * tq].T for h in range(heads)], axis=1).astype(BF16)


def _lane_stack(q_ref, heads):
    return jnp.concatenate([q_ref[0, h * LANE:(h + 1) * LANE, :] for h in range(heads)], axis=1)


def _causal_body(q_ref, k_ref, v_ref, o_ref, m_ref, acc_ref, *, heads, tq, tk):
    i = pl.program_id(1)
    q0 = i * tq
    qt = _lane_stack(q_ref, heads)
    _tflash_init(m_ref, acc_ref)

    def tile(c0):
        return _dot(k_ref[0, pl.ds(c0, tk), :], qt), v_ref[0, :, pl.ds(c0, tk)]

    def full_step(j, carry):
        s, vt = tile(pl.multiple_of(j * tk, tk))
        _tflash_update(s, vt, m_ref, acc_ref)
        return carry

    n_full = q0 // tk
    lax.fori_loop(0, n_full, full_step, 0)
    c0 = pl.multiple_of(n_full * tk, tk)
    s, vt = tile(c0)
    key = c0 + lax.broadcasted_iota(jnp.int32, s.shape, 0)
    qpos = q0 + jnp.bitwise_and(lax.broadcasted_iota(jnp.int32, s.shape, 1), tq - 1)
    _tflash_update(jnp.where(key <= qpos, s, NEG), vt, m_ref, acc_ref)
    o_ref[...] = _tflash_out(acc_ref[...], heads, tq)


def _causal_attn(qt, k, vt, *, heads, tq, tk):
    b, _, t = qt.shape
    nq = t // tq
    assert tq & (tq - 1) == 0 and tk % tq == 0 and t % tk == 0
    m = heads * tq
    return pl.pallas_call(
        functools.partial(_causal_body, heads=heads, tq=tq, tk=tk),
        grid=(b, nq),
        in_specs=[pl.BlockSpec((1, heads * LANE, tq), lambda bi, i: (bi, 0, i)),
                  pl.BlockSpec((1, t, LANE), lambda bi, i: (bi, 0, 0)),
                  pl.BlockSpec((1, LANE, t), lambda bi, i: (bi, 0, 0))],
        out_specs=pl.BlockSpec((tq, heads * LANE), lambda bi, i: (bi * nq + i, 0)),
        out_shape=jax.ShapeDtypeStruct((b * t, heads * LANE), BF16),
        scratch_shapes=[pltpu.VMEM((1, m), F32), pltpu.VMEM((LANE, m), F32)],
        compiler_params=_cparams(2),
        name="causal_attn_h%d" % heads,
    )(qt, k, vt)


def _top_k_neg_mask_t(score, k):
    n = score.shape[0]
    rowi = lax.broadcasted_iota(jnp.int32, score.shape, 0).astype(F32)
    out = jnp.full(score.shape, NEG, F32)
    work = score
    for _ in range(k):
        mx = jnp.max(work, axis=0, keepdims=True)
        idx = jnp.min(jnp.where(work == mx, rowi, float(n)), axis=0, keepdims=True)
        hit = rowi == idx
        out = jnp.where(hit, 0.0, out)
        work = jnp.where(hit, -jnp.inf, work)
    return out


def _nsa_body(q_ref, kvc_ref, kvct_ref, ks_ref, vs_ref, kw_ref, vw_ref, g_ref, cb_ref, sb_ref, wb_ref,
              o_ref, m_ref, acc_ref, m2_ref, acc2_ref, *, tq, tk, pad):
    hh = NSA_HEADS
    i = pl.program_id(1)
    q0 = i * tq
    qt = _lane_stack(q_ref, hh)
    cb = jnp.concatenate([cb_ref[h] for h in range(hh)], axis=1)
    s = _dot(kvc_ref[0], qt) + cb
    e = jnp.exp2(s - jnp.max(s, axis=0, keepdims=True))
    p = e / jnp.sum(e, axis=0, keepdims=True)
    p = jnp.where(cb > 0.5 * NEG, p, 0.0)
    o_cmp = _dot(kvct_ref[0], p.astype(BF16))
    nc = p.shape[0]
    pc = p[:, 0:tq]
    for h in range(1, hh):
        pc = pc + p[:, h * tq:(h + 1) * tq]
    n_sel = nc // 2
    score = pc[:n_sel] + pc[n_sel:]
    blk = lax.broadcasted_iota(jnp.int32, (n_sel, tq), 0)
    qpos = q0 + lax.broadcasted_iota(jnp.int32, (n_sel, tq), 1)
    cur = jnp.right_shift(qpos, _log2(SEL_BLOCK))
    forced = (blk == 0) | (blk == cur) | (blk == cur - 1)
    future = blk * SEL_BLOCK > qpos
    score = jnp.where(forced, 1e6, jnp.where(future, -1e6, score))
    selneg = _top_k_neg_mask_t(score, min(N_SEL, n_sel)).astype(BF16)
    if n_sel < LANE:
        selneg = jnp.concatenate([selneg, jnp.zeros((LANE - n_sel, tq), BF16)], axis=0)
    qa = jnp.concatenate([qt, jnp.concatenate([selneg] * hh, axis=1)], axis=0)

    _tflash_init(m_ref, acc_ref)

    def far(c0, n):
        s_j = _dot(ks_ref[0, pl.ds(pad + c0, n), :], qa)
        _tflash_update(s_j, vs_ref[0, :, pl.ds(pad + c0, n)], m_ref, acc_ref)

    n_far = jnp.maximum(i - 1, 0)
    per = tk // tq
    n_big = n_far // per

    def big_step(j, carry):
        far(pl.multiple_of(j * tk, tk), tk)
        return carry

    def small_step(j, carry):
        far(pl.multiple_of(j * tq, tq), tq)
        return carry

    lax.fori_loop(0, n_big, big_step, 0)
    lax.fori_loop(n_big * per, n_far, small_step, 0)
    c0 = pl.multiple_of(q0 - tq, tq)
    s_n = _dot(ks_ref[0, pl.ds(pad + c0, 2 * tq), :], qa) + sb_ref[...]
    key = c0 + lax.broadcasted_iota(jnp.int32, s_n.shape, 0)
    _tflash_update(jnp.where(key >= 0, s_n, NEG), vs_ref[0, :, pl.ds(pad + c0, 2 * tq)], m_ref, acc_ref)
    acc_s = acc_ref[...]
    o_sel = acc_s / acc_s[ONES_ROW:ONES_ROW + 1]

    c0 = pl.multiple_of(q0 - WINDOW, tq)
    nw = WINDOW + tq
    s_w = _dot(kw_ref[0, pl.ds(pad + c0, nw), :], qt) + wb_ref[...]
    key = c0 + lax.broadcasted_iota(jnp.int32, s_w.shape, 0)
    _tflash_init(m2_ref, acc2_ref)
    _tflash_update(jnp.where(key >= 0, s_w, NEG), vw_ref[0, :, pl.ds(pad + c0, nw)], m2_ref, acc2_ref)
    acc_w = acc2_ref[...]
    o_win = acc_w / acc_w[ONES_ROW:ONES_ROW + 1]

    g = g_ref[0]
    outs = []
    for h in range(hh):
        sl = slice(h * tq, (h + 1) * tq)
        o_h = (g[3 * h:3 * h + 1] * o_cmp[NSA_DH:, sl] + g[3 * h + 1:3 * h + 2] * o_sel[:NSA_DH, sl]
               + g[3 * h + 2:3 * h + 3] * o_win[:NSA_DH, sl])
        outs.append(jnp.concatenate([o_h, jnp.zeros_like(o_h)], axis=0).T)
    o_ref[...] = jnp.concatenate(outs, axis=1).astype(BF16)


def _rel_bucket(dist):
    d = jnp.maximum(dist, 0)
    exact = REL_BUCKETS // 2
    scaled = jnp.log(jnp.maximum(d, 1).astype(F32) / exact) / math.log(REL_MAX_DIST / exact)
    large = jnp.minimum(exact + (scaled * (REL_BUCKETS - exact)).astype(jnp.int32), REL_BUCKETS - 1)
    return jnp.where(d < exact, d, large)


def _rel_bias_t(table, dist):
    bucket = _rel_bucket(dist)
    tab = table.astype(F32)
    out = jnp.zeros((tab.shape[1],) + dist.shape, F32)
    extra = (None,) * dist.ndim
    for k in range(REL_BUCKETS):
        out = jnp.where(bucket[None] == k, tab[k][(slice(None),) + extra], out)
    return out


def _head_lanes(a):
    return jnp.concatenate([a[h] for h in range(a.shape[0])], axis=1)


def _nsa_prompt_tables(rel_table, t, tq):
    far = rel_table[REL_BUCKETS - 1].astype(F32)[:, None, None]
    nc = t // CMP_BLOCK
    order = jnp.concatenate([jnp.arange(0, nc, 2), jnp.arange(1, nc, 2)])
    c_end = order * CMP_BLOCK + CMP_BLOCK - 1
    dist = jnp.arange(t)[None, :] - c_end[:, None]
    cb = jnp.where(dist >= 0, _rel_bias_t(rel_table, dist) * LOG2E, NEG)
    dist = jnp.arange(tq)[None, :] + tq - jnp.arange(2 * tq)[:, None]
    sb = jnp.where(dist >= 0, (_rel_bias_t(rel_table, dist) - far) * LOG2E, NEG)
    dist = jnp.arange(tq)[None, :] + WINDOW - jnp.arange(WINDOW + tq)[:, None]
    wb = jnp.where((dist >= 0) & (dist < WINDOW), _rel_bias_t(rel_table, dist) * LOG2E, NEG)
    return cb, _head_lanes(sb), _head_lanes(wb)


def _nsa_prompt(qnt, kvc, kvct, ks, vst, kw, vwt, gt, tables, *, tq, tk, pad):
    b, _, t = qnt.shape
    nq = t // tq
    cb, sb, wb = tables
    nc = kvc.shape[1]
    hh = NSA_HEADS
    m = hh * tq
    per_b = lambda bi, i: (bi, 0, 0)
    tile = lambda bi, i: (bi, 0, i)
    return pl.pallas_call(
        functools.partial(_nsa_body, tq=tq, tk=tk, pad=pad),
        grid=(b, nq),
        in_specs=[pl.BlockSpec((1, hh * LANE, tq), tile),
                  pl.BlockSpec((1, nc, LANE), per_b),
                  pl.BlockSpec((1, LANE, nc), per_b),
                  pl.BlockSpec((1,) + ks.shape[1:], per_b),
                  pl.BlockSpec((1,) + vst.shape[1:], per_b),
                  pl.BlockSpec((1,) + kw.shape[1:], per_b),
                  pl.BlockSpec((1,) + vwt.shape[1:], per_b),
                  pl.BlockSpec((1, 16, tq), tile),
                  pl.BlockSpec((hh, nc, tq), lambda bi, i: (0, 0, i)),
                  pl.BlockSpec(sb.shape, lambda bi, i: (0, 0)),
                  pl.BlockSpec(wb.shape, lambda bi, i: (0, 0))],
        out_specs=pl.BlockSpec((tq, hh * LANE), lambda bi, i: (bi * nq + i, 0)),
        out_shape=jax.ShapeDtypeStruct((b * t, hh * LANE), BF16),
        scratch_shapes=[pltpu.VMEM((1, m), F32), pltpu.VMEM((LANE, m), F32),
                        pltpu.VMEM((1, m), F32), pltpu.VMEM((LANE, m), F32)],
        compiler_params=_cparams(2),
        name="nsa_prompt",
    )(qnt, kvc, kvct, ks, vst, kw, vwt, gt, cb, sb, wb)


def _mix_body(x_ref, sc_ref, sh_ref, gt_ref, g_ref, on_ref, om_ref, of_ref, wmg_ref, wbn_ref,
              wuv_ref, wbm_ref, wbf_ref, wo_ref, o_ref):
    x = x_ref[...]
    d = x.shape[1]
    h = _rms(x, g_ref[...]) * (1.0 + sc_ref[0]) + sh_ref[0]
    mg = jax.nn.sigmoid(_dot(h.astype(BF16), wmg_ref[...]))
    o_mla = _dot(om_ref[...], wuv_ref[...]).astype(BF16)
    t = mg[:, :d] * _dot(on_ref[...], wbn_ref[...])
    t = t + mg[:, d:2 * d] * _dot(o_mla, wbm_ref[...])
    t = t + mg[:, 2 * d:] * _dot(of_ref[...], wbf_ref[...])
    o_ref[...] = x + gt_ref[0] * _dot(t.astype(BF16), wo_ref[...])


def _mix(x, sc, sh, gt, mod_index, lw, o_nsa, o_lat, o_fox, *, tm, value_lane):
    n, d = x.shape
    row = lambda i: (i, 0)
    const2 = lambda i: (0, 0)
    ws = [lw["w_mg"], lw["w_br_n"][value_lane], lw["w_uv"], lw["w_br_m"], lw["w_br_f"][value_lane],
          lw["w_o"]]
    return pl.pallas_call(
        _mix_body,
        grid=(n // tm,),
        in_specs=[pl.BlockSpec((tm, d), row)]
        + [pl.BlockSpec((1,) + a.shape[1:], mod_index) for a in (sc, sh, gt)]
        + [pl.BlockSpec((1, d), const2)]
        + [pl.BlockSpec((tm, a.shape[1]), row) for a in (o_nsa, o_lat, o_fox)]
        + [pl.BlockSpec(w.shape, const2) for w in ws],
        out_specs=pl.BlockSpec((tm, d), row),
        out_shape=jax.ShapeDtypeStruct((n, d), F32),
        compiler_params=_cparams(1),
        name="mix",
    )(x, sc, sh, gt, lw["g_attn"], o_nsa, o_lat, o_fox, *ws)


def _gelu_tanh(x):
    return 0.5 * x * (1.0 + jnp.tanh(math.sqrt(2.0 / math.pi) * (x + 0.044715 * (x * x * x))))


def _ffn_body(x_ref, sc_ref, sh_ref, gt_ref, g_ref, pre_ref, win_ref, cw_ref, cb_ref, wout_ref,
              gf_ref, o_ref, tail_ref, hist_ref, *, tiles_per_seq, time_major, final_norm, d_ff):
    x = x_ref[...]
    tm = x.shape[0]
    i = pl.program_id(0)
    h = _rms(x, g_ref[...]) * (1.0 + sc_ref[0]) + sh_ref[0]
    ab = _dot(h.astype(BF16), win_ref[...])
    a = ab[:, :d_ff]
    b = ab[:, d_ff:]
    cw = cw_ref[...]
    if time_major:
        @pl.when(i % tiles_per_seq == 0)
        def _():
            hist_ref[0] = pre_ref[0]
            hist_ref[1] = pre_ref[1]
        a2 = hist_ref[0]
        a1 = hist_ref[1]
        hist_ref[0] = a1
        hist_ref[1] = a
        tail_ref[0] = a
    else:
        @pl.when(i % tiles_per_seq == 0)
        def _():
            hist_ref[0, 6:8] = pre_ref[0]
        prev = hist_ref[0, 6:8]
        rid = lax.broadcasted_iota(jnp.int32, (tm, 1), 0)
        a1 = jnp.where(rid == 0, prev[1:2], pltpu.roll(a, 1, 0))
        a2 = jnp.where(rid == 0, prev[0:1], jnp.where(rid == 1, prev[1:2], pltpu.roll(a, 2, 0)))
        hist_ref[0] = a[tm - 8:]
        tail_ref[0] = a[tm - 2:]
    conv = a2 * cw[0:1] + a1 * cw[1:2] + a * cw[2:3] + cb_ref[...]
    y = _dot((_gelu_tanh(conv) * b).astype(BF16), wout_ref[...])
    out = x + gt_ref[0] * y
    if final_norm:
        out = _rms(out, gf_ref[...])
    o_ref[...] = out


def _ffn(x, sc, sh, gt, mod_index, lw, prefix, g_final, *, tm, tiles_per_seq, time_major, final_norm):
    n, d = x.shape
    d_ff = lw["w_ffn_out"].shape[0]
    row = lambda i: (i, 0)
    const2 = lambda i: (0, 0)
    n_tiles = n // tm
    if time_major:
        pre_spec = pl.BlockSpec(prefix.shape, lambda i: (0, 0, 0))
        tail_spec = pl.BlockSpec((1, tm, d_ff),
                                 lambda i: (jnp.maximum(i - (tiles_per_seq - 2), 0), 0, 0))
        tail_shape = jax.ShapeDtypeStruct((2, tm, d_ff), F32)
        hist = pltpu.VMEM((2, tm, d_ff), F32)
    else:
        pre_spec = pl.BlockSpec((1, 2, d_ff), lambda i: (i // tiles_per_seq, 0, 0))
        tail_spec = pl.BlockSpec((1, 2, d_ff), lambda i: (i // tiles_per_seq, 0, 0))
        tail_shape = jax.ShapeDtypeStruct((n_tiles // tiles_per_seq, 2, d_ff), F32)
        hist = pltpu.VMEM((1, 8, d_ff), F32)
    return pl.pallas_call(
        functools.partial(_ffn_body, tiles_per_seq=tiles_per_seq, time_major=time_major,
                          final_norm=final_norm, d_ff=d_ff),
        grid=(n_tiles,),
        in_specs=[pl.BlockSpec((tm, d), row)]
        + [pl.BlockSpec((1,) + a.shape[1:], mod_index) for a in (sc, sh, gt)]
        + [pl.BlockSpec((1, d), const2), pre_spec,
           pl.BlockSpec(lw["w_ffn_in"].shape, const2),
           pl.BlockSpec((CONV_W, d_ff), const2), pl.BlockSpec((1, d_ff), const2),
           pl.BlockSpec(lw["w_ffn_out"].shape, const2), pl.BlockSpec((1, d), const2)],
        out_specs=[pl.BlockSpec((tm, d), row), tail_spec],
        out_shape=[jax.ShapeDtypeStruct((n, d), F32), tail_shape],
        scratch_shapes=[hist],
        compiler_params=_cparams(1),
        name="conv_ffn",
    )(x, sc, sh, gt, lw["g_ffn"], prefix, lw["w_ffn_in"], lw["conv_w"], lw["conv_b"],
      lw["w_ffn_out"], g_final)


def _pad_heads(w, heads, dh, scale=1.0):
    k = w.shape[0]
    w = (w * scale).reshape(k, heads, dh)
    return jnp.pad(w, ((0, 0), (0, 0), (0, LANE - dh))).reshape(k, heads * LANE)


def _pad_head_rows(w, heads, dh, offset):
    n = w.shape[1]
    w = w.reshape(heads, dh, n)
    return jnp.pad(w, ((0, 0), (offset, LANE - dh - offset), (0, 0))).reshape(heads * LANE, n)


def _pad_cols(w, n=LANE):
    return jnp.pad(w, ((0, 0), (0, n - w.shape[1])))


def _prep_layer(l, p):
    d = p["w_in"].shape[1]
    w_in = p["w_in"][l]
    nsa_w = NSA_HEADS * NSA_DH
    fox_w = FOX_HEADS * FOX_DH
    q_rank = p["mla_g_q"].shape[1]
    kv_w = MLA_KV_RANK + MLA_ROPE
    splits = (nsa_w, 2 * NSA_DH, 2 * NSA_DH, 2 * NSA_DH, 3 * NSA_HEADS, q_rank, kv_w,
              fox_w, 2 * FOX_DH, FOX_HEADS, 3 * d)
    cuts = [int(c) for c in np.cumsum(splits)[:-1]]
    (w_nq, w_nc, w_ns, w_nw, w_ng, w_qd, w_kvd, w_fq, w_fkv, w_ff, w_mg) = jnp.split(w_in, cuts, axis=1)
    half = MLA_ROPE // 2
    w_kr = w_kvd[:, MLA_KV_RANK:]
    w_kr_sw = jnp.concatenate([w_kr[:, half:], w_kr[:, :half]], axis=1)
    w_t = jnp.concatenate([w_nc, w_ns, w_nw, w_fkv, w_kvd, w_kr_sw, w_ff, _pad_cols(w_ng, 16), w_qd,
                           _pad_heads(w_nq, NSA_HEADS, NSA_DH, NSA_SCALE * LOG2E),
                           _pad_heads(w_fq, FOX_HEADS, FOX_DH, FOX_SCALE * LOG2E)], axis=1)
    assert w_t.shape[1] == _T_END
    w_row = jnp.concatenate([_pad_cols(w_ns[:, :NSA_DH]), _pad_cols(w_nw[:, :NSA_DH]),
                             _pad_cols(w_fkv[:, :FOX_DH]), _pad_cols(w_kvd),
                             _pad_cols(jnp.pad(w_kr_sw, ((0, 0), (MLA_KV_RANK, 0)))),
                             w_nc, _pad_cols(w_ff)], axis=1)
    assert w_row.shape[1] == _R_END
    aug = np.zeros((FOX_HEADS, LANE), np.float32)
    for h in range(FOX_HEADS):
        for part in range(DECAY_PARTS):
            aug[h, FOX_DH + part * FOX_HEADS + h] = 1.0
    w_uq = p["mla_w_uq"][l].reshape(q_rank, MLA_HEADS, MLA_NOPE + MLA_ROPE)
    uq_nope = w_uq[:, :, :MLA_NOPE].reshape(q_rank, -1)
    uq_rope = w_uq[:, :, MLA_NOPE:]
    uq_rope_sw = jnp.concatenate([uq_rope[:, :, half:], uq_rope[:, :, :half]], axis=2)
    w_uq2 = jnp.concatenate([uq_nope, uq_rope.reshape(q_rank, -1), uq_rope_sw.reshape(q_rank, -1)],
                            axis=1)
    w_uk = p["mla_w_uk"][l]
    eye_h = jnp.eye(MLA_HEADS, dtype=F32)
    uk_bd = jnp.einsum("chd,hg->hdgc", w_uk, eye_h)
    uk_bd = jnp.pad(uk_bd, ((0, 0), (0, 0), (0, 0), (0, LANE - MLA_KV_RANK)))
    uk_bd = uk_bd.reshape(MLA_HEADS * MLA_NOPE, MLA_HEADS * LANE)
    rr = jnp.arange(MLA_HEADS * MLA_ROPE)
    pp = (jnp.arange(MLA_HEADS * LANE)[None, :]
          == ((rr // MLA_ROPE) * LANE + MLA_KV_RANK + rr % MLA_ROPE)[:, None]).astype(F32)
    w_uv = p["mla_w_uv"][l]
    uv_bd = jnp.einsum("chv,hg->hcgv", w_uv, eye_h)
    uv_bd = jnp.pad(uv_bd, ((0, 0), (0, LANE - MLA_KV_RANK), (0, 0), (0, 0)))
    uv_bd = uv_bd.reshape(MLA_HEADS * LANE, -1).astype(BF16)
    w_br = p["w_br"][l]
    mla_w = w_uv.shape[1] * w_uv.shape[2]
    br_n, br_f = w_br[:nsa_w], w_br[nsa_w + mla_w:]
    g_kv = p["mla_g_kv"][l]
    b_f = p["fox_b_f"][l]
    return {
        "g_attn": p["g_attn"][l][None, :], "g_ffn": p["g_ffn"][l][None, :],
        "w_row": w_row.astype(BF16), "w_t": w_t.T.astype(BF16), "g_q": p["mla_g_q"][l][:, None],
        "w_uq": w_uq2.T.astype(BF16), "w_uk": uk_bd.T.astype(BF16), "pp": pp.T.astype(BF16),
        "g_kv": g_kv[:, None], "g_kv_row": _pad_cols(g_kv[None, :]),
        "b_f": b_f[:, None], "b_f_row": _pad_cols(b_f[None, :]),
        "q_aug": jnp.asarray(aug.reshape(FOX_HEADS * LANE, 1)),
        "w_mg": w_mg.astype(BF16),
        "w_br_n": {o: _pad_head_rows(br_n, NSA_HEADS, NSA_DH, o).astype(BF16) for o in (0, NSA_DH)},
        "w_uv": uv_bd,
        "w_br_m": w_br[nsa_w:nsa_w + mla_w].astype(BF16),
        "w_br_f": {o: _pad_head_rows(br_f, FOX_HEADS, FOX_DH, o).astype(BF16) for o in (0, FOX_DH)},
        "w_o": p["w_o"][l].astype(BF16),
        "w_ffn_in": p["w_ffn_in"][l].astype(BF16), "conv_w": p["conv_w"][l],
        "conv_b": p["conv_b"][l][None, :], "w_ffn_out": p["w_ffn_out"][l].astype(BF16),
    }


def _rope_tables(pos):
    half = MLA_ROPE // 2
    inv = ROPE_BASE ** (-jnp.arange(half, dtype=F32) / half)
    ang = pos.astype(F32)[:, None] * inv[None, :]
    cos, sin = jnp.cos(ang), jnp.sin(ang)
    c1 = jnp.concatenate([cos, cos], axis=1)
    c2 = jnp.concatenate([-sin, sin], axis=1)
    padr = ((0, 0), (MLA_KV_RANK, LANE - MLA_KV_RANK - MLA_ROPE))
    return (c1.T, c2.T, jnp.tile(c1, (1, MLA_HEADS)).T, jnp.tile(c2, (1, MLA_HEADS)).T,
            jnp.pad(c1, padr), jnp.pad(c2, padr))


def _page_copy(cache_ref, layer, page, buf, slot, p, rows, sem):
    dst = buf.at[slot, pl.ds(0, rows), pl.ds(pl.multiple_of(p * LANE, LANE), LANE)]
    return pltpu.make_async_copy(cache_ref.at[layer, page], dst, sem)


def _softmax_parts(parts):
    m = parts[0].max(axis=-1, keepdims=True)
    for s in parts[1:]:
        m = jnp.maximum(m, s.max(axis=-1, keepdims=True))
    ps = [jnp.exp2(s - m) for s in parts]
    l = ps[0].sum(axis=-1, keepdims=True)
    for p in ps[1:]:
        l = l + p.sum(axis=-1, keepdims=True)
    return ps, l


def _paged_body(pt_ref, q_ref, knew_ref, *rest, layer, n_seq, n_pages, rows, heads, s_len, tk,
                with_decay):
    if with_decay:
        (lfn_ref, tri_s_ref, tri_ref, cache_ref, lcache_ref, o_ref, kbuf, lbuf, sem) = rest
    else:
        (cache_ref, o_ref, kbuf, sem) = rest
    b = pl.program_id(0)
    slot = b % 2
    past = n_pages * LANE
    sp = knew_ref.shape[1]

    def issue(seq, sl):
        def body(p, c):
            pg = pt_ref[seq, p]
            _page_copy(cache_ref, layer, pg, kbuf, sl, p, rows, sem.at[sl, 0]).start()
            if with_decay:
                pltpu.make_async_copy(lcache_ref.at[layer, pg], lbuf.at[sl, p], sem.at[sl, 1]).start()
            return c
        lax.fori_loop(0, n_pages, body, 0)

    def wait(sl):
        def body(p, c):
            _page_copy(cache_ref, layer, 0, kbuf, sl, 0, rows, sem.at[sl, 0]).wait()
            if with_decay:
                pltpu.make_async_copy(lcache_ref.at[layer, 0], lbuf.at[sl, 0], sem.at[sl, 1]).wait()
            return c
        lax.fori_loop(0, n_pages, body, 0)

    @pl.when(b == 0)
    def _():
        if rows < LANE:
            kbuf[:, rows:, :] = jnp.zeros((2, LANE - rows, past), F32)
        issue(0, 0)

    @pl.when(b + 1 < n_seq)
    def _():
        issue(b + 1, 1 - slot)

    wait(slot)

    q = q_ref[0]
    m_rows = q.shape[0]
    knew = knew_ref[0].astype(BF16)
    s_new = _dot_nt(q, knew)
    if with_decay:
        cin = _dot_f32(lbuf[slot].reshape(n_pages * 8, LANE), tri_ref[...])
        run = jnp.zeros((8, 1), F32)
        negc = []
        for p in range(n_pages):
            cp = cin[p * 8:(p + 1) * 8]
            negc.append(-(cp + run) * LOG2E)
            run = run + cp[:, LANE - 1:]
        cs_new = _dot_f32(lfn_ref[0], tri_s_ref[...])
        s_new = s_new + jnp.concatenate([-(run + cs_new) * LOG2E] * s_len, axis=0)
    sq = jnp.right_shift(lax.broadcasted_iota(jnp.int32, (m_rows, sp), 0), _log2(heads))
    jj = lax.broadcasted_iota(jnp.int32, (m_rows, sp), 1)
    s_new = jnp.where(jj <= sq, s_new, NEG)
    m = jnp.max(s_new, axis=-1, keepdims=True)
    p = jnp.exp2(s_new - m)
    l = jnp.sum(p, axis=-1, keepdims=True)
    acc = _dot(p.astype(BF16), knew)
    ppc = tk // LANE
    for c in range(past // tk):
        kt = kbuf[slot, :, c * tk:(c + 1) * tk].astype(BF16)
        s = _dot(q, kt)
        if with_decay:
            nc = jnp.concatenate(negc[c * ppc:(c + 1) * ppc], axis=1)
            s = s + jnp.concatenate([nc] * s_len, axis=0)
        m_new = jnp.maximum(m, jnp.max(s, axis=-1, keepdims=True))
        alpha = jnp.exp2(m - m_new)
        p = jnp.exp2(s - m_new)
        l = alpha * l + jnp.sum(p, axis=-1, keepdims=True)
        acc = alpha * acc + _dot_nt(p.astype(BF16), kt)
        m = m_new
    o_ref[0] = (acc / l).astype(BF16)


def _paged_attn(page_table, q, knew, cache_t, layer, *, heads, s_len, decay=None):
    n_seq, n_pages = page_table.shape
    rows = cache_t.shape[2]
    past = n_pages * LANE
    tk = min(past, 2048)
    m = q.shape[1]
    sp = knew.shape[1]
    with_decay = decay is not None
    per_b = lambda b, pt: (b, 0, 0)
    in_specs = [pl.BlockSpec((1, m, LANE), per_b), pl.BlockSpec((1, sp, LANE), per_b)]
    args = [q, knew]
    scratch = [pltpu.VMEM((2, LANE, past), F32)]
    if with_decay:
        lfn, lcache_t = decay
        tri_s = jnp.triu(jnp.ones((sp, sp), F32))
        tri = jnp.triu(jnp.ones((LANE, LANE), F32))
        in_specs += [pl.BlockSpec((1, 8, sp), per_b), pl.BlockSpec((sp, sp), lambda b, pt: (0, 0)),
                     pl.BlockSpec((LANE, LANE), lambda b, pt: (0, 0)),
                     pl.BlockSpec(memory_space=pl.ANY), pl.BlockSpec(memory_space=pl.ANY)]
        args += [lfn, tri_s, tri, cache_t, lcache_t]
        scratch.append(pltpu.VMEM((2, n_pages, 8, LANE), F32))
    else:
        in_specs.append(pl.BlockSpec(memory_space=pl.ANY))
        args.append(cache_t)
    scratch.append(pltpu.SemaphoreType.DMA((2, 2)))
    return pl.pallas_call(
        functools.partial(_paged_body, layer=layer, n_seq=n_seq, n_pages=n_pages, rows=rows,
                          heads=heads, s_len=s_len, tk=tk, with_decay=with_decay),
        grid_spec=pltpu.PrefetchScalarGridSpec(
            num_scalar_prefetch=1, grid=(n_seq,), in_specs=in_specs,
            out_specs=pl.BlockSpec((1, m, LANE), per_b), scratch_shapes=scratch),
        out_shape=jax.ShapeDtypeStruct((n_seq, m, LANE), BF16),
        compiler_params=_cparams(1),
        name="paged_attn_decay" if with_decay else "paged_attn",
    )(page_table, *args)


def _nsa_s1_body(pt_ref, q_ref, wnew_ref, win_ref, cb_ref, wb_ref, wnb_ref, amat_ref, pair_ref,
                 cache_ref, ocmp_ref, owin_ref, idx_ref, kbuf, sem, *, layer, n_seq, n_pages, s_len,
                 k_free):
    b = pl.program_id(0)
    slot = b % 2
    past = n_pages * LANE
    hh = NSA_HEADS

    def issue(seq, sl):
        def body(p, c):
            _page_copy(cache_ref, layer, pt_ref[seq, p], kbuf, sl, p, LANE, sem.at[sl]).start()
            return c
        lax.fori_loop(0, n_pages, body, 0)

    def wait(sl):
        def body(p, c):
            _page_copy(cache_ref, layer, 0, kbuf, sl, 0, LANE, sem.at[sl]).wait()
            return c
        lax.fori_loop(0, n_pages, body, 0)

    @pl.when(b == 0)
    def _():
        issue(0, 0)

    @pl.when(b + 1 < n_seq)
    def _():
        issue(b + 1, 1 - slot)

    wait(slot)
    q = q_ref[0]
    tc = amat_ref.shape[0]
    amat = amat_ref[...]
    means = []
    for c in range(past // tc):
        xc = kbuf[slot, :, c * tc:(c + 1) * tc]
        hi = xc.astype(BF16)
        lo = (xc - hi.astype(F32)).astype(BF16)
        means.append(_dot(hi, amat) + _dot(lo, amat))
    kvc = jnp.concatenate(means, axis=1).astype(BF16)
    s = _dot(q, kvc) + cb_ref[...]
    e = jnp.exp2(s - jnp.max(s, axis=-1, keepdims=True))
    p = e / jnp.sum(e, axis=-1, keepdims=True)
    ocmp_ref[0] = _dot_nt(p.astype(BF16), kvc)
    ps = _dot_f32(p, pair_ref[...])
    score = ps[0:s_len]
    for h in range(1, hh):
        score = score + ps[h * s_len:(h + 1) * s_len]
    n_past = score.shape[1]
    lane = lax.broadcasted_iota(jnp.int32, score.shape, 1).astype(F32)
    work = jnp.where((lane == 0.0) | (lane == n_past - 1.0), -jnp.inf, score)
    out_lane = lax.broadcasted_iota(jnp.int32, (s_len, LANE), 1)
    idx_out = jnp.zeros((s_len, LANE), F32)
    for r in range(k_free):
        mx = jnp.max(work, axis=-1, keepdims=True)
        idx = jnp.min(jnp.where(work == mx, lane, float(n_past)), axis=-1, keepdims=True)
        idx_out = jnp.where(out_lane == r, idx, idx_out)
        work = jnp.where(lane == idx, -jnp.inf, work)
    idx_ref[0] = idx_out.astype(jnp.int32)
    wst = win_ref[0, 0].astype(BF16)
    wnew = wnew_ref[0].astype(BF16)
    (p_w, p_n), l = _softmax_parts([_dot(q, wst) + wb_ref[...], _dot_nt(q, wnew) + wnb_ref[...]])
    owin_ref[0] = (_dot_nt(p_w.astype(BF16), wst) + _dot(p_n.astype(BF16), wnew)) / l


def _nsa_s1(page_table, q, wnew, win_t, cache_t, layer, tables, *, s_len, k_free):
    n_seq, n_pages = page_table.shape
    past = n_pages * LANE
    cb, wb, wnb = tables
    m = q.shape[1]
    sp = wnew.shape[1]
    wlen = win_t.shape[3]
    tc = min(past, CMP_BLOCK * LANE)
    amat = (jnp.arange(tc)[:, None] // CMP_BLOCK == jnp.arange(tc // CMP_BLOCK)[None, :]).astype(F32)
    amat = (amat / CMP_BLOCK).astype(BF16)
    nc = past // CMP_BLOCK
    ratio = SEL_BLOCK // CMP_BLOCK
    pair = (jnp.arange(nc)[:, None] // ratio == jnp.arange(nc // ratio)[None, :]).astype(F32)
    per_b = lambda b, pt: (b, 0, 0)
    c2 = lambda b, pt: (0, 0)
    return pl.pallas_call(
        functools.partial(_nsa_s1_body, layer=layer, n_seq=n_seq, n_pages=n_pages, s_len=s_len,
                          k_free=k_free),
        grid_spec=pltpu.PrefetchScalarGridSpec(
            num_scalar_prefetch=1, grid=(n_seq,),
            in_specs=[pl.BlockSpec((1, m, LANE), per_b), pl.BlockSpec((1, sp, LANE), per_b),
                      pl.BlockSpec((1, 1, LANE, wlen), lambda b, pt: (layer, b, 0, 0)),
                      pl.BlockSpec(cb.shape, c2), pl.BlockSpec(wb.shape, c2),
                      pl.BlockSpec(wnb.shape, c2), pl.BlockSpec(amat.shape, c2),
                      pl.BlockSpec(pair.shape, c2), pl.BlockSpec(memory_space=pl.ANY)],
            out_specs=[pl.BlockSpec((1, m, LANE), per_b), pl.BlockSpec((1, m, LANE), per_b),
                       pl.BlockSpec((1, s_len, LANE), per_b)],
            scratch_shapes=[pltpu.VMEM((2, LANE, past), F32), pltpu.SemaphoreType.DMA((2,))]),
        out_shape=[jax.ShapeDtypeStruct((n_seq, m, LANE), F32),
                   jax.ShapeDtypeStruct((n_seq, m, LANE), F32),
                   jax.ShapeDtypeStruct((n_seq, s_len, LANE), jnp.int32)],
        compiler_params=_cparams(1),
        name="nsa_sample_cmp",
    )(page_table, q, wnew, win_t, cb, wb, wnb, amat, pair, cache_t)


def _nsa_s2_body(pt_ref, idx_ref, q_ref, snew_ref, g_ref, ocmp_ref, owin_ref, lb_ref, nb_ref,
                 cache_ref, o_ref, kbuf, sem, *, layer, n_seq, n_pages, s_len, k_free):
    b = pl.program_id(0)
    slot = b % 2
    n_own = s_len * k_free
    n_slots = n_own + 2
    per_page = LANE // SEL_BLOCK

    def block_of(seq, j):
        if j < n_own:
            return idx_ref[(seq * s_len + j // k_free) * LANE + j % k_free]
        return 0 if j == n_own else n_pages * per_page - 1

    def issue(seq, sl):
        for j in range(n_slots):
            blk = block_of(seq, j)
            pg = pt_ref[seq, blk >> _log2(per_page)]
            _page_copy(cache_ref, layer, pg, kbuf, sl, j, LANE, sem.at[sl]).start()

    def wait(sl):
        for j in range(n_slots):
            _page_copy(cache_ref, layer, 0, kbuf, sl, 0, LANE, sem.at[sl]).wait()

    @pl.when(b == 0)
    def _():
        issue(0, 0)

    @pl.when(b + 1 < n_seq)
    def _():
        issue(b + 1, 1 - slot)

    wait(slot)
    q = q_ref[0]
    m_rows = q.shape[0]
    kt = kbuf[slot].astype(BF16)
    s = _dot(q, kt)
    assert s_len & (s_len - 1) == 0
    row_s = jnp.bitwise_and(lax.broadcasted_iota(jnp.int32, (m_rows, LANE), 0), s_len - 1)
    lane_half = jnp.right_shift(lax.broadcasted_iota(jnp.int32, (m_rows, LANE), 1), _log2(SEL_BLOCK))
    bias = []
    last_page = (n_pages - 1) * per_page
    for j in range(n_slots):
        blk = block_of(b, j)
        ok = lane_half == (blk & (per_page - 1))
        if j < n_own:
            ok = ok & (row_s == j // k_free)
            near = jnp.where(blk >= last_page, lb_ref[...], 0.0)
        else:
            near = lb_ref[...] if j == n_slots - 1 else 0.0
        bias.append(jnp.where(ok, near, NEG))
    s = s + jnp.concatenate(bias, axis=1)
    snew = snew_ref[0].astype(BF16)
    (p_s, p_n), l = _softmax_parts([s, _dot_nt(q, snew) + nb_ref[...]])
    o_sel = (_dot_nt(p_s.astype(BF16), kt) + _dot(p_n.astype(BF16), snew)) / l
    g = g_ref[0]
    o = g[:, 0:1] * ocmp_ref[0] + g[:, 1:2] * o_sel + g[:, 2:3] * owin_ref[0]
    o_ref[0] = o.astype(BF16)


def _nsa_s2(page_table, idx, q, snew, gates, o_cmp, o_win, cache_t, layer, tables, *, s_len, k_free):
    n_seq, n_pages = page_table.shape
    lb, nb = tables
    m = q.shape[1]
    sp = snew.shape[1]
    n_slots = s_len * k_free + 2
    per_b = lambda b, pt, ix: (b, 0, 0)
    c2 = lambda b, pt, ix: (0, 0)
    return pl.pallas_call(
        functools.partial(_nsa_s2_body, layer=layer, n_seq=n_seq, n_pages=n_pages, s_len=s_len,
                          k_free=k_free),
        grid_spec=pltpu.PrefetchScalarGridSpec(
            num_scalar_prefetch=2, grid=(n_seq,),
            in_specs=[pl.BlockSpec((1, m, LANE), per_b), pl.BlockSpec((1, sp, LANE), per_b),
                      pl.BlockSpec((1, m, LANE), per_b), pl.BlockSpec((1, m, LANE), per_b),
                      pl.BlockSpec((1, m, LANE), per_b), pl.BlockSpec(lb.shape, c2),
                      pl.BlockSpec(nb.shape, c2), pl.BlockSpec(memory_space=pl.ANY)],
            out_specs=pl.BlockSpec((1, m, LANE), per_b),
            scratch_shapes=[pltpu.VMEM((2, LANE, n_slots * LANE), F32),
                            pltpu.SemaphoreType.DMA((2,))]),
        out_shape=jax.ShapeDtypeStruct((n_seq, m, LANE), BF16),
        compiler_params=_cparams(1),
        name="nsa_sample_sel",
    )(page_table, idx.reshape(-1), q, snew, gates, o_cmp, o_win, lb, nb, cache_t)


def _nsa_sample_tables(rel_table, past, s_len, sp, wlen):
    hh = NSA_HEADS
    far = rel_table[REL_BUCKETS - 1].astype(F32)[:, None, None]
    qpos = past + jnp.arange(s_len)
    rows = lambda a: a.reshape(hh * s_len, a.shape[-1])
    c_end = jnp.arange(past // CMP_BLOCK) * CMP_BLOCK + CMP_BLOCK - 1
    cb = rows(_rel_bias_t(rel_table, qpos[:, None] - c_end[None, :]) * LOG2E)
    dist = qpos[:, None] - (past - wlen + jnp.arange(wlen))[None, :]
    wb = rows(jnp.where(dist < WINDOW, _rel_bias_t(rel_table, dist) * LOG2E, NEG))
    dist_new = jnp.arange(s_len)[:, None] - jnp.arange(sp)[None, :]
    newb = _rel_bias_t(rel_table, dist_new)
    wnb = rows(jnp.where(dist_new >= 0, newb * LOG2E, NEG))
    assert LANE >= REL_MAX_DIST
    dist = qpos[:, None] - (past - LANE + jnp.arange(LANE))[None, :]
    lb = rows((_rel_bias_t(rel_table, dist) - far) * LOG2E)
    nb = rows(jnp.where(dist_new >= 0, (newb - far) * LOG2E, NEG))
    return (cb, wb, wnb), (lb, nb)


TQ = 128
TK = 512
TM_PROMPT = 256
NEW_PAD = 16
KEY_PAD = WINDOW


def _fm_to_rows(a, mid):
    lead = a.shape[:-2]
    n = a.shape[-1]
    a = a.reshape(lead + mid + (n,))
    return jnp.moveaxis(a, -1, len(lead))


def kernel(x_prompt, x_sample, cache_nsa_cmp, cache_nsa_sel, state_nsa_win, cache_mla, cache_fox_kv, cache_fox_logf, state_ffn_conv, page_table, c_prompt, c_sample, rel_table, w_ada, b_ada, g_attn, g_ffn, w_in, mla_g_q, mla_w_uq, mla_g_kv, mla_w_uk, mla_w_uv, fox_b_f, w_br, w_o, w_ffn_in, conv_w, conv_b, w_ffn_out, g_final):
    b, t, d = x_prompt.shape
    db, s_len, _ = x_sample.shape
    depth = w_in.shape[0]
    pool = cache_nsa_cmp.shape[1]
    n_pages = page_table.shape[1]
    past = n_pages * LANE
    d_ff = w_ffn_out.shape[1]
    wlen = state_nsa_win.shape[2]
    assert cache_nsa_cmp.shape[2] == LANE and TQ >= REL_MAX_DIST and t % TK == 0
    assert t // SEL_BLOCK <= LANE and TM_PROMPT == 8 * CMP_BLOCK
    params = dict(w_in=w_in, mla_g_q=mla_g_q, mla_w_uq=mla_w_uq, mla_g_kv=mla_g_kv, mla_w_uk=mla_w_uk,
                  mla_w_uv=mla_w_uv, fox_b_f=fox_b_f, w_br=w_br, w_o=w_o, w_ffn_in=w_ffn_in,
                  conv_w=conv_w, conv_b=conv_b, w_ffn_out=w_ffn_out, g_attn=g_attn, g_ffn=g_ffn)

    n_c = b + db
    c_all = jnp.pad(jnp.concatenate([c_prompt, c_sample], axis=0), ((0, -n_c % 8), (0, 0)))
    mods = _adaln(c_all, w_ada, b_ada)

    cmp_t = jnp.transpose(cache_nsa_cmp, (0, 1, 3, 4, 2)).reshape(depth, pool, LANE, LANE)
    sel_t = jnp.transpose(cache_nsa_sel, (0, 1, 3, 4, 2)).reshape(depth, pool, LANE, LANE)
    fkv_t = jnp.transpose(cache_fox_kv, (0, 1, 3, 4, 5, 2)).reshape(depth, pool, LANE, LANE)
    mla_t = jnp.transpose(cache_mla, (0, 1, 3, 2))
    lf_t = jnp.transpose(cache_fox_logf, (0, 1, 3, 2))
    win_t = jnp.transpose(state_nsa_win, (0, 1, 3, 4, 2)).reshape(depth, db, LANE, wlen)
    conv_pre_s = jnp.transpose(state_ffn_conv, (0, 2, 1, 3))
    conv_pre_p = jnp.zeros((b, CONV_W - 1, d_ff), F32)

    rope_p = _rope_tables(jnp.arange(t))
    rope_s = _rope_tables(jnp.repeat(past + jnp.arange(s_len), db))
    tabs_p = _nsa_prompt_tables(rel_table, t, TQ)
    tabs_s1, tabs_s2 = _nsa_sample_tables(rel_table, past, s_len, NEW_PAD, wlen)
    n_sel_s = -(-(past + s_len) // SEL_BLOCK)
    k_free = min(N_SEL, n_sel_s) - 3

    tpb = t // TM_PROMPT
    idx_p = lambda i: (i // tpb, 0, 0)
    idx_s = lambda i: (0, 0, 0)
    xp = x_prompt.reshape(b * t, d)
    xs = jnp.transpose(x_sample, (1, 0, 2)).reshape(s_len * db, d)

    def new_rows(a):
        return jnp.pad(jnp.transpose(a, (2, 0, 1)), ((0, 0), (0, NEW_PAD - s_len), (0, 0)))

    def q_rows(qt, heads, width, head_major):
        q = jnp.transpose(qt.reshape(s_len, heads, LANE, db)[:, :, :width],
                          (3, 1, 0, 2) if head_major else (3, 0, 1, 2))
        return jnp.pad(q.reshape(db, s_len * heads, width), ((0, 0), (0, 0), (0, LANE - width)))

    def o_rows(o, heads, head_major):
        o = o.reshape((db, heads, s_len, LANE) if head_major else (db, s_len, heads, LANE))
        o = jnp.transpose(o, (2, 0, 1, 3) if head_major else (1, 0, 2, 3))
        return o.reshape(s_len * db, heads * LANE)

    def front_pad(a, axis):
        pads = [(0, 0)] * a.ndim
        pads[axis] = (KEY_PAD, 0)
        return jnp.pad(a, pads)

    rows_p, rows_s = [], []
    for l in range(depth):
        lw = _prep_layer(l, params)
        m6 = [mods[l][:, k * d:(k + 1) * d] for k in range(6)]
        sh_a, sc_a, gt_a, sh_f, sc_f, gt_f = [a[:b][:, None, :] for a in m6]
        sh_as, sc_as, gt_as, sh_fs, sc_fs, gt_fs = [a[b:n_c][None] for a in m6]
        last = l == depth - 1

        pre = _pre_proj(xp, sc_a, sh_a, idx_p, lw, rope_p, n_groups=b, tiles_per_group=tpb,
                        n_pos_tiles=tpb, tm=TM_PROMPT, with_keys=True)
        nc = t // CMP_BLOCK
        kvc = pre["kvc"].reshape(b, nc // 2, 2, LANE)
        kvc = jnp.transpose(kvc, (0, 2, 1, 3)).reshape(b, nc, LANE)
        kvc_k = jnp.pad(kvc[:, :, :NSA_DH], ((0, 0), (0, 0), (0, LANE - NSA_DH))).astype(BF16)
        kvct = jnp.transpose(kvc, (0, 2, 1)).astype(BF16)
        rs = lambda a: a.reshape(b, t, a.shape[-1])
        o_nsa = _nsa_prompt(pre["qnt"], kvc_k, kvct, front_pad(rs(pre["ks"]), 1),
                            front_pad(pre["vst"], 2), front_pad(rs(pre["kw"]), 1),
                            front_pad(pre["vwt"], 2), pre["gt"], tabs_p, tq=TQ, tk=TK, pad=KEY_PAD)
        o_lat = _causal_attn(pre["qmt"], rs(pre["km"]), pre["vmt"], heads=MLA_HEADS, tq=TQ, tk=TK)
        o_fox = _causal_attn(pre["qft"], rs(pre["kf"]), pre["vft"], heads=FOX_HEADS, tq=TQ, tk=TK)
        x1 = _mix(xp, sc_a, sh_a, gt_a, idx_p, lw, o_nsa, o_lat, o_fox, tm=TM_PROMPT, value_lane=0)
        xp, tail_p = _ffn(x1, sc_f, sh_f, gt_f, idx_p, lw, conv_pre_p, g_final[None, :], tm=TM_PROMPT,
                          tiles_per_seq=tpb, time_major=False, final_norm=last)
        wkeep = min(WINDOW, t)
        rows_p.append((_fm_to_rows(pre["ct"], (2, NSA_DH)), _fm_to_rows(pre["st"], (2, NSA_DH)),
                       _fm_to_rows(pre["wt"][:, :, t - wkeep:], (2, NSA_DH)),
                       _fm_to_rows(pre["mt"][:, :MLA_KV_RANK + MLA_ROPE], (MLA_KV_RANK + MLA_ROPE,)),
                       _fm_to_rows(pre["ft"], (2, 1, FOX_DH)), _fm_to_rows(pre["lft"], (FOX_HEADS,)),
                       tail_p))

        pre_s = _pre_proj(xs, sc_as, sh_as, idx_s, lw, rope_s, n_groups=s_len, tiles_per_group=1,
                          n_pos_tiles=s_len, tm=db, with_keys=False)
        qn_s = q_rows(pre_s["qnt"], NSA_HEADS, NSA_DH, True)
        gates_s = jnp.transpose(pre_s["gt"][:, :3 * NSA_HEADS].reshape(s_len, NSA_HEADS, 3, db),
                                (3, 1, 0, 2)).reshape(db, NSA_HEADS * s_len, 3)
        gates_s = jnp.pad(gates_s, ((0, 0), (0, 0), (0, LANE - 3)))
        o_cmp, o_win, idx = _nsa_s1(page_table, qn_s, new_rows(pre_s["wt"]), win_t, cmp_t, l, tabs_s1,
                                    s_len=s_len, k_free=k_free)
        o_nsa_s = _nsa_s2(page_table, idx, qn_s, new_rows(pre_s["st"]), gates_s, o_cmp, o_win, sel_t, l,
                          tabs_s2, s_len=s_len, k_free=k_free)
        o_lat_s = _paged_attn(page_table, q_rows(pre_s["qmt"], MLA_HEADS, LANE, False),
                              new_rows(pre_s["mt"]), mla_t, l, heads=MLA_HEADS, s_len=s_len)
        lfn = jnp.pad(jnp.transpose(pre_s["lft"], (2, 1, 0)), ((0, 0), (0, 0), (0, NEW_PAD - s_len)))
        o_fox_s = _paged_attn(page_table, q_rows(pre_s["qft"], FOX_HEADS, FOX_DH, False),
                              new_rows(pre_s["ft"]), fkv_t, l, heads=FOX_HEADS, s_len=s_len,
                              decay=(lfn, lf_t))
        x1s = _mix(xs, sc_as, sh_as, gt_as, idx_s, lw, o_rows(o_nsa_s, NSA_HEADS, True),
                   o_rows(o_lat_s, MLA_HEADS, False), o_rows(o_fox_s, FOX_HEADS, False), tm=db,
                   value_lane=NSA_DH)
        xs, tail_s = _ffn(x1s, sc_fs, sh_fs, gt_fs, idx_s, lw, conv_pre_s[l], g_final[None, :], tm=db,
                          tiles_per_seq=s_len, time_major=True, final_norm=last)
        wfull = jnp.concatenate([win_t[l], jnp.transpose(pre_s["wt"], (2, 1, 0))], axis=2)
        wfull = wfull[:, :, wfull.shape[2] - min(WINDOW, past + s_len):]
        fm_s = lambda a, mid: jnp.swapaxes(_fm_to_rows(a, mid), 0, 1)
        rows_s.append((fm_s(pre_s["ct"], (2, NSA_DH)), fm_s(pre_s["st"], (2, NSA_DH)),
                       _fm_to_rows(wfull, (2, NSA_DH)),
                       fm_s(pre_s["mt"][:, :MLA_KV_RANK + MLA_ROPE], (MLA_KV_RANK + MLA_ROPE,)),
                       fm_s(pre_s["ft"], (2, 1, FOX_DH)), fm_s(pre_s["lft"], (FOX_HEADS,)),
                       jnp.transpose(tail_s, (1, 0, 2))))

    y_prompt = xp.reshape(b, t, d)
    y_sample = jnp.transpose(xs.reshape(s_len, db, d), (1, 0, 2))
    outs_p = [jnp.stack(a) for a in zip(*rows_p)]
    outs_s = [jnp.stack(a) for a in zip(*rows_s)]
    return (y_prompt, y_sample, *outs_p, *outs_s)
```

```python
import functools
import math

import numpy as np
import jax
import jax.numpy as jnp
from jax import lax
from jax.experimental import pallas as pl
from jax.experimental.pallas import tpu as pltpu

F32 = jnp.float32
BF16 = jnp.bfloat16

NSA_HEADS = 4
NSA_DH = 64
CMP_BLOCK = 32
SEL_BLOCK = 64
N_SEL = 16
WINDOW = 512
MLA_HEADS = 4
MLA_NOPE = 64
MLA_ROPE = 32
MLA_KV_RANK = 64
ROPE_BASE = 10000.0
FOX_HEADS = 8
FOX_DH = 64
REL_BUCKETS = 32
REL_MAX_DIST = 128
CONV_W = 3
EPS = 1e-6
NEG = -1e30
LOG2E = math.log2(math.e)
LANE = 128
VMEM_LIMIT = 56 * 1024 * 1024

NSA_SCALE = NSA_DH ** -0.5
MLA_SCALE = (MLA_NOPE + MLA_ROPE) ** -0.5
FOX_SCALE = FOX_DH ** -0.5

ONES_ROW = 64
ONES_ROWS = 8
DECAY_PARTS = 3

_NT = (((1,), (1,)), ((), ()))


def _cparams(n_axes):
    return pltpu.CompilerParams(dimension_semantics=("arbitrary",) * n_axes,
                                vmem_limit_bytes=VMEM_LIMIT)


def _dot(a, b):
    return jnp.dot(a, b, preferred_element_type=F32)


def _dot_nt(a, b):
    return lax.dot_general(a, b, _NT, preferred_element_type=F32)


def _dot_f32(a, b):
    return jnp.dot(a, b, preferred_element_type=F32, precision=lax.Precision.HIGHEST)


def _dot_terms(x, w, terms):
    out = None
    rest = x
    for _ in range(terms):
        piece = rest.astype(BF16)
        rest = rest - piece.astype(F32)
        d = _dot(piece, w)
        out = d if out is None else out + d
    return out


def _rms(x, g):
    return x * lax.rsqrt(jnp.mean(x * x, axis=-1, keepdims=True) + EPS) * g


def _log2(n):
    assert n & (n - 1) == 0, n
    return n.bit_length() - 1


def _ada_body(c_ref, w_ref, b_ref, o_ref):
    c = c_ref[...]
    sc = (c * jax.nn.sigmoid(c)).astype(BF16)
    o_ref[0] = _dot(sc, w_ref[0].astype(BF16)) + b_ref[0]


def _adaln(c_all, w_ada, b_ada):
    depth, d, n6 = w_ada.shape
    rows = c_all.shape[0]
    tn = 512
    return pl.pallas_call(
        _ada_body,
        grid=(depth, n6 // tn),
        in_specs=[pl.BlockSpec((rows, d), lambda l, n: (0, 0)),
                  pl.BlockSpec((1, d, tn), lambda l, n: (l, 0, n)),
                  pl.BlockSpec((1, 1, tn), lambda l, n: (l, 0, n))],
        out_specs=pl.BlockSpec((1, rows, tn), lambda l, n: (l, 0, n)),
        out_shape=jax.ShapeDtypeStruct((depth, rows, n6), F32),
        compiler_params=_cparams(2),
        name="adaln",
    )(c_all, w_ada, b_ada.reshape(depth, 1, n6))


_T_C, _T_S, _T_W, _T_F, _T_M, _T_LF, _T_G, _T_QD = 0, 128, 256, 384, 512, 640, 648, 664
_T_QN = _T_QD + 256
_T_QF = _T_QN + NSA_HEADS * LANE
_T_END = _T_QF + FOX_HEADS * LANE
_R_KS, _R_KW, _R_KF, _R_KM, _R_KMS, _R_C, _R_LF, _R_END = 0, 128, 256, 384, 512, 640, 768, 896


def _value_tile(vt):
    n = vt.shape[1]
    return jnp.concatenate([vt, jnp.ones((ONES_ROWS, n), F32),
                            jnp.zeros((LANE - ONES_ROW - ONES_ROWS, n), F32)], axis=0).astype(BF16)


def _pre_body(x_ref, sc_ref, sh_ref, g_ref, wrow_ref, wt_ref, gq_ref, wuq_ref, wuk_ref, pp_ref,
              gkv_ref, gkvr_ref, bf_ref, bfr_ref, aug_ref, cs1_ref, cs2_ref, csq1_ref, csq2_ref,
              cr1_ref, cr2_ref, tril_ref, avg_ref,
              ct_ref, st_ref, wtt_ref, ft_ref, mt_ref, lf_ref, gt_ref, qn_ref, qf_ref, qm_ref,
              vs_ref, vw_ref, vf_ref, vm_ref, ks_ref, kw_ref, kf_ref, km_ref, kvc_ref,
              carry_ref, *, tiles_per_seq, with_keys):
    x = x_ref[...]
    tm = x.shape[0]
    h = _rms(x, g_ref[...]) * (1.0 + sc_ref[0]) + sh_ref[0]
    hb = h.astype(BF16)

    pt = _dot_nt(wt_ref[...], hb)
    ct_ref[0] = pt[_T_C:_T_C + 128]
    st_ref[0] = pt[_T_S:_T_S + 128]
    wtt_ref[0] = pt[_T_W:_T_W + 128]
    ft_ref[0] = pt[_T_F:_T_F + 128]
    ckv = pt[_T_M:_T_M + 64]
    ckv = ckv * lax.rsqrt(jnp.mean(ckv * ckv, axis=0, keepdims=True) + EPS) * gkv_ref[...]
    krot = pt[_T_M + 64:_T_M + 96] * cs1_ref[...] + pt[_T_M + 96:_T_M + 128] * cs2_ref[...]
    mt_ref[0, 0:64] = ckv
    mt_ref[0, 64:96] = krot
    mt_ref[0, 96:128] = jnp.zeros_like(krot)
    lf_ref[0] = jax.nn.log_sigmoid(pt[_T_LF:_T_LF + 8] + bf_ref[...])
    gt_ref[0] = jax.nn.sigmoid(pt[_T_G:_T_G + 16])
    qn_ref[0] = pt[_T_QN:_T_QF].astype(BF16)
    qf_ref[0] = (pt[_T_QF:_T_END] + aug_ref[...]).astype(BF16)
    qd = pt[_T_QD:_T_QD + 256]
    qd = qd * lax.rsqrt(jnp.mean(qd * qd, axis=0, keepdims=True) + EPS) * gq_ref[...]
    q = _dot(wuq_ref[...], qd.astype(BF16))
    qrot = q[256:384] * csq1_ref[...] + q[384:512] * csq2_ref[...]
    qm = _dot(wuk_ref[...], q[:256].astype(BF16)) * (MLA_SCALE * LOG2E)
    qm = qm + _dot(pp_ref[...], (qrot * (MLA_SCALE * LOG2E)).astype(BF16))
    qm_ref[0] = qm.astype(BF16)
    vs_ref[0] = _value_tile(pt[_T_S + 64:_T_S + 128])
    vw_ref[0] = _value_tile(pt[_T_W + 64:_T_W + 128])
    vf_ref[0] = _value_tile(pt[_T_F + 64:_T_F + 128])
    vm_ref[0] = _value_tile(ckv)

    if with_keys:
        pr = _dot(hb, wrow_ref[...])
        lane = lax.broadcasted_iota(jnp.int32, (tm, LANE), 1)
        t_in_seq = (pl.program_id(0) % tiles_per_seq) * tm
        pos = t_in_seq + lax.broadcasted_iota(jnp.int32, (tm, LANE), 0)
        blk_ind = jnp.where(lane == jnp.right_shift(pos, _log2(SEL_BLOCK)), 1.0, 0.0)
        ks_ref[:, 0:LANE] = pr[:, _R_KS:_R_KS + LANE].astype(BF16)
        ks_ref[:, LANE:2 * LANE] = blk_ind.astype(BF16)
        kw_ref[...] = pr[:, _R_KW:_R_KW + LANE].astype(BF16)
        kvd = pr[:, _R_KM:_R_KM + LANE]
        is_c = lane < MLA_KV_RANK
        ms = jnp.sum(jnp.where(is_c, kvd * kvd, 0.0), axis=1, keepdims=True) / MLA_KV_RANK
        km = jnp.where(is_c, kvd * lax.rsqrt(ms + EPS) * gkvr_ref[...],
                       kvd * cr1_ref[...] + pr[:, _R_KMS:_R_KMS + LANE] * cr2_ref[...])
        km_ref[...] = km.astype(BF16)
        @pl.when(pl.program_id(0) % tiles_per_seq == 0)
        def _():
            carry_ref[...] = jnp.zeros_like(carry_ref)
        lfr = jnp.where(lane < FOX_HEADS, jax.nn.log_sigmoid(pr[:, _R_LF:_R_LF + LANE] + bfr_ref[...]),
                        0.0)
        csum = _dot_f32(tril_ref[...], lfr) + carry_ref[0:1]
        carry_ref[...] = jnp.broadcast_to(csum[tm - 1:], carry_ref.shape)
        kf = pr[:, _R_KF:_R_KF + LANE]
        rest = -csum * LOG2E
        for part in range(DECAY_PARTS):
            term = rest.astype(BF16).astype(F32)
            rest = rest - term
            kf = kf + pltpu.roll(term, FOX_DH + part * FOX_HEADS, 1)
        kf_ref[...] = kf.astype(BF16)
        kvc_ref[...] = _dot_f32(avg_ref[...], pr[:, _R_C:_R_C + LANE])
    else:
        ks_ref[...] = jnp.zeros(ks_ref.shape, BF16)
        kw_ref[...] = jnp.zeros(kw_ref.shape, BF16)
        kf_ref[...] = jnp.zeros(kf_ref.shape, BF16)
        km_ref[...] = jnp.zeros(km_ref.shape, BF16)
        kvc_ref[...] = jnp.zeros(kvc_ref.shape, F32)


def _pre_proj(x, sc, sh, mod_index, lw, pos_tabs, *, n_groups, tiles_per_group, n_pos_tiles, tm,
              with_keys):
    n, d = x.shape
    cs1, cs2, csq1, csq2, cr1, cr2 = pos_tabs
    n_tiles = n // tm
    tpg = tiles_per_group
    tril = jnp.tril(jnp.ones((tm, tm), F32))
    nmean = 8
    avg = (jnp.arange(tm)[None, :] // CMP_BLOCK == jnp.arange(nmean)[:, None]).astype(F32) / CMP_BLOCK
    row = lambda i: (i, 0)
    const2 = lambda i: (0, 0)
    tcol = lambda i: (i // tpg, 0, i % tpg)
    pcol = lambda i: (0, i % n_pos_tiles)
    prow = lambda i: (i % n_pos_tiles, 0)
    ncols = tpg * tm

    def tspec(r):
        return pl.BlockSpec((1, r, tm), tcol)

    def tshape(r, dt=F32):
        return jax.ShapeDtypeStruct((n_groups, r, ncols), dt)

    nq = NSA_HEADS * LANE
    nf = FOX_HEADS * LANE
    consts = [lw["g_attn"], lw["w_row"], lw["w_t"], lw["g_q"], lw["w_uq"], lw["w_uk"], lw["pp"],
              lw["g_kv"], lw["g_kv_row"], lw["b_f"], lw["b_f_row"], lw["q_aug"]]
    outs = pl.pallas_call(
        functools.partial(_pre_body, tiles_per_seq=tpg, with_keys=with_keys),
        grid=(n_tiles,),
        in_specs=[pl.BlockSpec((tm, d), row),
                  pl.BlockSpec((1,) + sc.shape[1:], mod_index),
                  pl.BlockSpec((1,) + sh.shape[1:], mod_index)]
        + [pl.BlockSpec(c.shape, const2) for c in consts]
        + [pl.BlockSpec((32, tm), pcol), pl.BlockSpec((32, tm), pcol),
           pl.BlockSpec((128, tm), pcol), pl.BlockSpec((128, tm), pcol),
           pl.BlockSpec((tm, 128), prow), pl.BlockSpec((tm, 128), prow),
           pl.BlockSpec((tm, tm), const2), pl.BlockSpec((nmean, tm), const2)],
        out_specs=[tspec(128), tspec(128), tspec(128), tspec(128), tspec(128), tspec(8), tspec(16),
                   tspec(nq), tspec(nf), tspec(nq), tspec(128), tspec(128), tspec(128), tspec(128),
                   pl.BlockSpec((tm, 2 * LANE), row), pl.BlockSpec((tm, LANE), row),
                   pl.BlockSpec((tm, LANE), row), pl.BlockSpec((tm, LANE), row),
                   pl.BlockSpec((nmean, LANE), row)],
        out_shape=[tshape(128), tshape(128), tshape(128), tshape(128), tshape(128), tshape(8),
                   tshape(16), tshape(nq, BF16), tshape(nf, BF16), tshape(nq, BF16),
                   tshape(128, BF16), tshape(128, BF16), tshape(128, BF16), tshape(128, BF16),
                   jax.ShapeDtypeStruct((n, 2 * LANE), BF16), jax.ShapeDtypeStruct((n, LANE), BF16),
                   jax.ShapeDtypeStruct((n, LANE), BF16), jax.ShapeDtypeStruct((n, LANE), BF16),
                   jax.ShapeDtypeStruct((n_tiles * nmean, LANE), F32)],
        scratch_shapes=[pltpu.VMEM((8, LANE), F32)],
        compiler_params=_cparams(1),
        name="pre_proj",
    )(x, sc, sh, *consts, cs1, cs2, csq1, csq2, cr1, cr2, tril, avg)
    keys = ("ct", "st", "wt", "ft", "mt", "lft", "gt", "qnt", "qft", "qmt", "vst", "vwt", "vft", "vmt",
            "ks", "kw", "kf", "km", "kvc")
    return dict(zip(keys, outs))


def _tflash_init(m_ref, acc_ref):
    m_ref[...] = jnp.full(m_ref.shape, NEG, F32)
    acc_ref[...] = jnp.zeros(acc_ref.shape, F32)


def _tflash_update(s, vt, m_ref, acc_ref):
    m_old = m_ref[...]
    m_new = jnp.maximum(m_old, jnp.max(s, axis=0, keepdims=True))
    p = jnp.exp2(s - m_new).astype(BF16)
    acc_ref[...] = jnp.exp2(m_old - m_new) * acc_ref[...] + _dot(vt, p)
    m_ref[...] = m_new


def _tflash_out(acc, heads, tq):
    o = acc / acc[ONES_ROW:ONES_ROW + 1]
    return jnp.concatenate([o[:, h * tq:(h + 1) * tq].T for h in range(heads)], axis=1).astype(BF16)


def _lane_stack(q_ref, heads):
    return jnp.concatenate([q_ref[0, h * LANE:(h + 1) * LANE, :] for h in range(heads)], axis=1)


def _causal_body(q_ref, k_ref, v_ref, o_ref, m_ref, acc_ref, *, heads, tq, tk):
    i = pl.program_id(1)
    q0 = i * tq
    qt = _lane_stack(q_ref, heads)
    _tflash_init(m_ref, acc_ref)

    def tile(c0):
        return _dot(k_ref[0, pl.ds(c0, tk), :], qt), v_ref[0, :, pl.ds(c0, tk)]

    def full_step(j, carry):
        s, vt = tile(pl.multiple_of(j * tk, tk))
        _tflash_update(s, vt, m_ref, acc_ref)
        return carry

    n_full = q0 // tk
    lax.fori_loop(0, n_full, full_step, 0)
    c0 = pl.multiple_of(n_full * tk, tk)
    s, vt = tile(c0)
    key = c0 + lax.broadcasted_iota(jnp.int32, s.shape, 0)
    qpos = q0 + jnp.bitwise_and(lax.broadcasted_iota(jnp.int32, s.shape, 1), tq - 1)
    _tflash_update(jnp.where(key <= qpos, s, NEG), vt, m_ref, acc_ref)
    o_ref[...] = _tflash_out(acc_ref[...], heads, tq)


def _causal_attn(qt, k, vt, *, heads, tq, tk):
    b, _, t = qt.shape
    nq = t // tq
    assert tq & (tq - 1) == 0 and tk % tq == 0 and t % tk == 0
    m = heads * tq
    return pl.pallas_call(
        functools.partial(_causal_body, heads=heads, tq=tq, tk=tk),
        grid=(b, nq),
        in_specs=[pl.BlockSpec((1, heads * LANE, tq), lambda bi, i: (bi, 0, i)),
                  pl.BlockSpec((1, t, LANE), lambda bi, i: (bi, 0, 0)),
                  pl.BlockSpec((1, LANE, t), lambda bi, i: (bi, 0, 0))],
        out_specs=pl.BlockSpec((tq, heads * LANE), lambda bi, i: (bi * nq + i, 0)),
        out_shape=jax.ShapeDtypeStruct((b * t, heads * LANE), BF16),
        scratch_shapes=[pltpu.VMEM((1, m), F32), pltpu.VMEM((LANE, m), F32)],
        compiler_params=_cparams(2),
        name="causal_attn_h%d" % heads,
    )(qt, k, vt)


def _top_k_neg_mask_t(score, k):
    n = score.shape[0]
    rowi = lax.broadcasted_iota(jnp.int32, score.shape, 0).astype(F32)
    out = jnp.full(score.shape, NEG, F32)
    work = score
    for _ in range(k):
        mx = jnp.max(work, axis=0, keepdims=True)
        idx = jnp.min(jnp.where(work == mx, rowi, float(n)), axis=0, keepdims=True)
        hit = rowi == idx
        out = jnp.where(hit, 0.0, out)
        work = jnp.where(hit, -jnp.inf, work)
    return out


def _nsa_body(q_ref, kvc_ref, kvct_ref, ks_ref, vs_ref, kw_ref, vw_ref, g_ref, cb_ref, sb_ref, wb_ref,
              o_ref, m_ref, acc_ref, m2_ref, acc2_ref, *, tq, tk, pad):
    hh = NSA_HEADS
    i = pl.program_id(1)
    q0 = i * tq
    qt = _lane_stack(q_ref, hh)
    cb = jnp.concatenate([cb_ref[h] for h in range(hh)], axis=1)
    s = _dot(kvc_ref[0], qt) + cb
    e = jnp.exp2(s - jnp.max(s, axis=0, keepdims=True))
    p = e / jnp.sum(e, axis=0, keepdims=True)
    p = jnp.where(cb > 0.5 * NEG, p, 0.0)
    o_cmp = _dot(kvct_ref[0], p.astype(BF16))
    nc = p.shape[0]
    pc = p[:, 0:tq]
    for h in range(1, hh):
        pc = pc + p[:, h * tq:(h + 1) * tq]
    n_sel = nc // 2
    score = pc[:n_sel] + pc[n_sel:]
    blk = lax.broadcasted_iota(jnp.int32, (n_sel, tq), 0)
    qpos = q0 + lax.broadcasted_iota(jnp.int32, (n_sel, tq), 1)
    cur = jnp.right_shift(qpos, _log2(SEL_BLOCK))
    forced = (blk == 0) | (blk == cur) | (blk == cur - 1)
    future = blk * SEL_BLOCK > qpos
    score = jnp.where(forced, 1e6, jnp.where(future, -1e6, score))
    selneg = _top_k_neg_mask_t(score, min(N_SEL, n_sel)).astype(BF16)
    if n_sel < LANE:
        selneg = jnp.concatenate([selneg, jnp.zeros((LANE - n_sel, tq), BF16)], axis=0)
    qa = jnp.concatenate([qt, jnp.concatenate([selneg] * hh, axis=1)], axis=0)

    _tflash_init(m_ref, acc_ref)

    def far(c0, n):
        s_j = _dot(ks_ref[0, pl.ds(pad + c0, n), :], qa)
        _tflash_update(s_j, vs_ref[0, :, pl.ds(pad + c0, n)], m_ref, acc_ref)

    n_far = jnp.maximum(i - 1, 0)
    per = tk // tq
    n_big = n_far // per

    def big_step(j, carry):
        far(pl.multiple_of(j * tk, tk), tk)
        return carry

    def small_step(j, carry):
        far(pl.multiple_of(j * tq, tq), tq)
        return carry

    lax.fori_loop(0, n_big, big_step, 0)
    lax.fori_loop(n_big * per, n_far, small_step, 0)
    c0 = pl.multiple_of(q0 - tq, tq)
    s_n = _dot(ks_ref[0, pl.ds(pad + c0, 2 * tq), :], qa) + sb_ref[...]
    key = c0 + lax.broadcasted_iota(jnp.int32, s_n.shape, 0)
    _tflash_update(jnp.where(key >= 0, s_n, NEG), vs_ref[0, :, pl.ds(pad + c0, 2 * tq)], m_ref, acc_ref)
    acc_s = acc_ref[...]
    o_sel = acc_s / acc_s[ONES_ROW:ONES_ROW + 1]

    c0 = pl.multiple_of(q0 - WINDOW, tq)
    nw = WINDOW + tq
    s_w = _dot(kw_ref[0, pl.ds(pad + c0, nw), :], qt) + wb_ref[...]
    key = c0 + lax.broadcasted_iota(jnp.int32, s_w.shape, 0)
    _tflash_init(m2_ref, acc2_ref)
    _tflash_update(jnp.where(key >= 0, s_w, NEG), vw_ref[0, :, pl.ds(pad + c0, nw)], m2_ref, acc2_ref)
    acc_w = acc2_ref[...]
    o_win = acc_w / acc_w[ONES_ROW:ONES_ROW + 1]

    g = g_ref[0]
    outs = []
    for h in range(hh):
        sl = slice(h * tq, (h + 1) * tq)
        o_h = (g[3 * h:3 * h + 1] * o_cmp[NSA_DH:, sl] + g[3 * h + 1:3 * h + 2] * o_sel[:NSA_DH, sl]
               + g[3 * h + 2:3 * h + 3] * o_win[:NSA_DH, sl])
        outs.append(jnp.concatenate([o_h, jnp.zeros_like(o_h)], axis=0).T)
    o_ref[...] = jnp.concatenate(outs, axis=1).astype(BF16)


def _rel_bucket(dist):
    d = jnp.maximum(dist, 0)
    exact = REL_BUCKETS // 2
    scaled = jnp.log(jnp.maximum(d, 1).astype(F32) / exact) / math.log(REL_MAX_DIST / exact)
    large = jnp.minimum(exact + (scaled * (REL_BUCKETS - exact)).astype(jnp.int32), REL_BUCKETS - 1)
    return jnp.where(d < exact, d, large)


def _rel_bias_t(table, dist):
    bucket = _rel_bucket(dist)
    tab = table.astype(F32)
    out = jnp.zeros((tab.shape[1],) + dist.shape, F32)
    extra = (None,) * dist.ndim
    for k in range(REL_BUCKETS):
        out = jnp.where(bucket[None] == k, tab[k][(slice(None),) + extra], out)
    return out


def _head_lanes(a):
    return jnp.concatenate([a[h] for h in range(a.shape[0])], axis=1)


def _nsa_prompt_tables(rel_table, t, tq):
    far = rel_table[REL_BUCKETS - 1].astype(F32)[:, None, None]
    nc = t // CMP_BLOCK
    order = jnp.concatenate([jnp.arange(0, nc, 2), jnp.arange(1, nc, 2)])
    c_end = order * CMP_BLOCK + CMP_BLOCK - 1
    dist = jnp.arange(t)[None, :] - c_end[:, None]
    cb = jnp.where(dist >= 0, _rel_bias_t(rel_table, dist) * LOG2E, NEG)
    dist = jnp.arange(tq)[None, :] + tq - jnp.arange(2 * tq)[:, None]
    sb = jnp.where(dist >= 0, (_rel_bias_t(rel_table, dist) - far) * LOG2E, NEG)
    dist = jnp.arange(tq)[None, :] + WINDOW - jnp.arange(WINDOW + tq)[:, None]
    wb = jnp.where((dist >= 0) & (dist < WINDOW), _rel_bias_t(rel_table, dist) * LOG2E, NEG)
    return cb, _head_lanes(sb), _head_lanes(wb)


def _nsa_prompt(qnt, kvc, kvct, ks, vst, kw, vwt, gt, tables, *, tq, tk, pad):
    b, _, t = qnt.shape
    nq = t // tq
    cb, sb, wb = tables
    nc = kvc.shape[1]
    hh = NSA_HEADS
    m = hh * tq
    per_b = lambda bi, i: (bi, 0, 0)
    tile = lambda bi, i: (bi, 0, i)
    return pl.pallas_call(
        functools.partial(_nsa_body, tq=tq, tk=tk, pad=pad),
        grid=(b, nq),
        in_specs=[pl.BlockSpec((1, hh * LANE, tq), tile),
                  pl.BlockSpec((1, nc, LANE), per_b),
                  pl.BlockSpec((1, LANE, nc), per_b),
                  pl.BlockSpec((1,) + ks.shape[1:], per_b),
                  pl.BlockSpec((1,) + vst.shape[1:], per_b),
                  pl.BlockSpec((1,) + kw.shape[1:], per_b),
                  pl.BlockSpec((1,) + vwt.shape[1:], per_b),
                  pl.BlockSpec((1, 16, tq), tile),
                  pl.BlockSpec((hh, nc, tq), lambda bi, i: (0, 0, i)),
                  pl.BlockSpec(sb.shape, lambda bi, i: (0, 0)),
                  pl.BlockSpec(wb.shape, lambda bi, i: (0, 0))],
        out_specs=pl.BlockSpec((tq, hh * LANE), lambda bi, i: (bi * nq + i, 0)),
        out_shape=jax.ShapeDtypeStruct((b * t, hh * LANE), BF16),
        scratch_shapes=[pltpu.VMEM((1, m), F32), pltpu.VMEM((LANE, m), F32),
                        pltpu.VMEM((1, m), F32), pltpu.VMEM((LANE, m), F32)],
        compiler_params=_cparams(2),
        name="nsa_prompt",
    )(qnt, kvc, kvct, ks, vst, kw, vwt, gt, cb, sb, wb)


def _mix_body(x_ref, sc_ref, sh_ref, gt_ref, g_ref, on_ref, om_ref, of_ref, wmg_ref, wbn_ref,
              wuv_ref, wbm_ref, wbf_ref, wo_ref, o_ref):
    x = x_ref[...]
    d = x.shape[1]
    h = _rms(x, g_ref[...]) * (1.0 + sc_ref[0]) + sh_ref[0]
    mg = jax.nn.sigmoid(_dot(h.astype(BF16), wmg_ref[...]))
    o_mla = _dot(om_ref[...], wuv_ref[...]).astype(BF16)
    t = mg[:, :d] * _dot(on_ref[...], wbn_ref[...])
    t = t + mg[:, d:2 * d] * _dot(o_mla, wbm_ref[...])
    t = t + mg[:, 2 * d:] * _dot(of_ref[...], wbf_ref[...])
    o_ref[...] = x + gt_ref[0] * _dot(t.astype(BF16), wo_ref[...])


def _mix(x, sc, sh, gt, mod_index, lw, o_nsa, o_lat, o_fox, *, tm, value_lane):
    n, d = x.shape
    row = lambda i: (i, 0)
    const2 = lambda i: (0, 0)
    ws = [lw["w_mg"], lw["w_br_n"][value_lane], lw["w_uv"], lw["w_br_m"], lw["w_br_f"][value_lane],
          lw["w_o"]]
    return pl.pallas_call(
        _mix_body,
        grid=(n // tm,),
        in_specs=[pl.BlockSpec((tm, d), row)]
        + [pl.BlockSpec((1,) + a.shape[1:], mod_index) for a in (sc, sh, gt)]
        + [pl.BlockSpec((1, d), const2)]
        + [pl.BlockSpec((tm, a.shape[1]), row) for a in (o_nsa, o_lat, o_fox)]
        + [pl.BlockSpec(w.shape, const2) for w in ws],
        out_specs=pl.BlockSpec((tm, d), row),
        out_shape=jax.ShapeDtypeStruct((n, d), F32),
        compiler_params=_cparams(1),
        name="mix",
    )(x, sc, sh, gt, lw["g_attn"], o_nsa, o_lat, o_fox, *ws)


def _gelu_tanh(x):
    return 0.5 * x * (1.0 + jnp.tanh(math.sqrt(2.0 / math.pi) * (x + 0.044715 * (x * x * x))))


def _ffn_body(x_ref, sc_ref, sh_ref, gt_ref, g_ref, pre_ref, win_ref, cw_ref, cb_ref, wout_ref,
              gf_ref, o_ref, tail_ref, hist_ref, *, tiles_per_seq, time_major, final_norm, d_ff):
    x = x_ref[...]
    tm = x.shape[0]
    i = pl.program_id(0)
    h = _rms(x, g_ref[...]) * (1.0 + sc_ref[0]) + sh_ref[0]
    ab = _dot(h.astype(BF16), win_ref[...])
    a = ab[:, :d_ff]
    b = ab[:, d_ff:]
    cw = cw_ref[...]
    if time_major:
        @pl.when(i % tiles_per_seq == 0)
        def _():
            hist_ref[0] = pre_ref[0]
            hist_ref[1] = pre_ref[1]
        a2 = hist_ref[0]
        a1 = hist_ref[1]
        hist_ref[0] = a1
        hist_ref[1] = a
        tail_ref[0] = a
    else:
        @pl.when(i % tiles_per_seq == 0)
        def _():
            hist_ref[0, 6:8] = pre_ref[0]
        prev = hist_ref[0, 6:8]
        rid = lax.broadcasted_iota(jnp.int32, (tm, 1), 0)
        a1 = jnp.where(rid == 0, prev[1:2], pltpu.roll(a, 1, 0))
        a2 = jnp.where(rid == 0, prev[0:1], jnp.where(rid == 1, prev[1:2], pltpu.roll(a, 2, 0)))
        hist_ref[0] = a[tm - 8:]
        tail_ref[0] = a[tm - 2:]
    conv = a2 * cw[0:1] + a1 * cw[1:2] + a * cw[2:3] + cb_ref[...]
    y = _dot((_gelu_tanh(conv) * b).astype(BF16), wout_ref[...])
    out = x + gt_ref[0] * y
    if final_norm:
        out = _rms(out, gf_ref[...])
    o_ref[...] = out


def _ffn(x, sc, sh, gt, mod_index, lw, prefix, g_final, *, tm, tiles_per_seq, time_major, final_norm):
    n, d = x.shape
    d_ff = lw["w_ffn_out"].shape[0]
    row = lambda i: (i, 0)
    const2 = lambda i: (0, 0)
    n_tiles = n // tm
    if time_major:
        pre_spec = pl.BlockSpec(prefix.shape, lambda i: (0, 0, 0))
        tail_spec = pl.BlockSpec((1, tm, d_ff),
                                 lambda i: (jnp.maximum(i - (tiles_per_seq - 2), 0), 0, 0))
        tail_shape = jax.ShapeDtypeStruct((2, tm, d_ff), F32)
        hist = pltpu.VMEM((2, tm, d_ff), F32)
    else:
        pre_spec = pl.BlockSpec((1, 2, d_ff), lambda i: (i // tiles_per_seq, 0, 0))
        tail_spec = pl.BlockSpec((1, 2, d_ff), lambda i: (i // tiles_per_seq, 0, 0))
        tail_shape = jax.ShapeDtypeStruct((n_tiles // tiles_per_seq, 2, d_ff), F32)
        hist = pltpu.VMEM((1, 8, d_ff), F32)
    return pl.pallas_call(
        functools.partial(_ffn_body, tiles_per_seq=tiles_per_seq, time_major=time_major,
                          final_norm=final_norm, d_ff=d_ff),
        grid=(n_tiles,),
        in_specs=[pl.BlockSpec((tm, d), row)]
        + [pl.BlockSpec((1,) + a.shape[1:], mod_index) for a in (sc, sh, gt)]
        + [pl.BlockSpec((1, d), const2), pre_spec,
           pl.BlockSpec(lw["w_ffn_in"].shape, const2),
           pl.BlockSpec((CONV_W, d_ff), const2), pl.BlockSpec((1, d_ff), const2),
           pl.BlockSpec(lw["w_ffn_out"].shape, const2), pl.BlockSpec((1, d), const2)],
        out_specs=[pl.BlockSpec((tm, d), row), tail_spec],
        out_shape=[jax.ShapeDtypeStruct((n, d), F32), tail_shape],
        scratch_shapes=[hist],
        compiler_params=_cparams(1),
        name="conv_ffn",
    )(x, sc, sh, gt, lw["g_ffn"], prefix, lw["w_ffn_in"], lw["conv_w"], lw["conv_b"],
      lw["w_ffn_out"], g_final)


def _pad_heads(w, heads, dh, scale=1.0):
    k = w.shape[0]
    w = (w * scale).reshape(k, heads, dh)
    return jnp.pad(w, ((0, 0), (0, 0), (0, LANE - dh))).reshape(k, heads * LANE)


def _pad_head_rows(w, heads, dh, offset):
    n = w.shape[1]
    w = w.reshape(heads, dh, n)
    return jnp.pad(w, ((0, 0), (offset, LANE - dh - offset), (0, 0))).reshape(heads * LANE, n)


def _pad_cols(w, n=LANE):
    return jnp.pad(w, ((0, 0), (0, n - w.shape[1])))


def _prep_layer(l, p):
    d = p["w_in"].shape[1]
    w_in = p["w_in"][l]
    nsa_w = NSA_HEADS * NSA_DH
    fox_w = FOX_HEADS * FOX_DH
    q_rank = p["mla_g_q"].shape[1]
    kv_w = MLA_KV_RANK + MLA_ROPE
    splits = (nsa_w, 2 * NSA_DH, 2 * NSA_DH, 2 * NSA_DH, 3 * NSA_HEADS, q_rank, kv_w,
              fox_w, 2 * FOX_DH, FOX_HEADS, 3 * d)
    cuts = [int(c) for c in np.cumsum(splits)[:-1]]
    (w_nq, w_nc, w_ns, w_nw, w_ng, w_qd, w_kvd, w_fq, w_fkv, w_ff, w_mg) = jnp.split(w_in, cuts, axis=1)
    half = MLA_ROPE // 2
    w_kr = w_kvd[:, MLA_KV_RANK:]
    w_kr_sw = jnp.concatenate([w_kr[:, half:], w_kr[:, :half]], axis=1)
    w_t = jnp.concatenate([w_nc, w_ns, w_nw, w_fkv, w_kvd, w_kr_sw, w_ff, _pad_cols(w_ng, 16), w_qd,
                           _pad_heads(w_nq, NSA_HEADS, NSA_DH, NSA_SCALE * LOG2E),
                           _pad_heads(w_fq, FOX_HEADS, FOX_DH, FOX_SCALE * LOG2E)], axis=1)
    assert w_t.shape[1] == _T_END
    w_row = jnp.concatenate([_pad_cols(w_ns[:, :NSA_DH]), _pad_cols(w_nw[:, :NSA_DH]),
                             _pad_cols(w_fkv[:, :FOX_DH]), _pad_cols(w_kvd),
                             _pad_cols(jnp.pad(w_kr_sw, ((0, 0), (MLA_KV_RANK, 0)))),
                             w_nc, _pad_cols(w_ff)], axis=1)
    assert w_row.shape[1] == _R_END
    aug = np.zeros((FOX_HEADS, LANE), np.float32)
    for h in range(FOX_HEADS):
        for part in range(DECAY_PARTS):
            aug[h, FOX_DH + part * FOX_HEADS + h] = 1.0
    w_uq = p["mla_w_uq"][l].reshape(q_rank, MLA_HEADS, MLA_NOPE + MLA_ROPE)
    uq_nope = w_uq[:, :, :MLA_NOPE].reshape(q_rank, -1)
    uq_rope = w_uq[:, :, MLA_NOPE:]
    uq_rope_sw = jnp.concatenate([uq_rope[:, :, half:], uq_rope[:, :, :half]], axis=2)
    w_uq2 = jnp.concatenate([uq_nope, uq_rope.reshape(q_rank, -1), uq_rope_sw.reshape(q_rank, -1)],
                            axis=1)
    w_uk = p["mla_w_uk"][l]
    eye_h = jnp.eye(MLA_HEADS, dtype=F32)
    uk_bd = jnp.einsum("chd,hg->hdgc", w_uk, eye_h)
    uk_bd = jnp.pad(uk_bd, ((0, 0), (0, 0), (0, 0), (0, LANE - MLA_KV_RANK)))
    uk_bd = uk_bd.reshape(MLA_HEADS * MLA_NOPE, MLA_HEADS * LANE)
    rr = jnp.arange(MLA_HEADS * MLA_ROPE)
    pp = (jnp.arange(MLA_HEADS * LANE)[None, :]
          == ((rr // MLA_ROPE) * LANE + MLA_KV_RANK + rr % MLA_ROPE)[:, None]).astype(F32)
    w_uv = p["mla_w_uv"][l]
    uv_bd = jnp.einsum("chv,hg->hcgv", w_uv, eye_h)
    uv_bd = jnp.pad(uv_bd, ((0, 0), (0, LANE - MLA_KV_RANK), (0, 0), (0, 0)))
    uv_bd = uv_bd.reshape(MLA_HEADS * LANE, -1).astype(BF16)
    w_br = p["w_br"][l]
    mla_w = w_uv.shape[1] * w_uv.shape[2]
    br_n, br_f = w_br[:nsa_w], w_br[nsa_w + mla_w:]
    g_kv = p["mla_g_kv"][l]
    b_f = p["fox_b_f"][l]
    return {
        "g_attn": p["g_attn"][l][None, :], "g_ffn": p["g_ffn"][l][None, :],
        "w_row": w_row.astype(BF16), "w_t": w_t.T.astype(BF16), "g_q": p["mla_g_q"][l][:, None],
        "w_uq": w_uq2.T.astype(BF16), "w_uk": uk_bd.T.astype(BF16), "pp": pp.T.astype(BF16),
        "g_kv": g_kv[:, None], "g_kv_row": _pad_cols(g_kv[None, :]),
        "b_f": b_f[:, None], "b_f_row": _pad_cols(b_f[None, :]),
        "q_aug": jnp.asarray(aug.reshape(FOX_HEADS * LANE, 1)),
        "w_mg": w_mg.astype(BF16),
        "w_br_n": {o: _pad_head_rows(br_n, NSA_HEADS, NSA_DH, o).astype(BF16) for o in (0, NSA_DH)},
        "w_uv": uv_bd,
        "w_br_m": w_br[nsa_w:nsa_w + mla_w].astype(BF16),
        "w_br_f": {o: _pad_head_rows(br_f, FOX_HEADS, FOX_DH, o).astype(BF16) for o in (0, FOX_DH)},
        "w_o": p["w_o"][l].astype(BF16),
        "w_ffn_in": p["w_ffn_in"][l].astype(BF16), "conv_w": p["conv_w"][l],
        "conv_b": p["conv_b"][l][None, :], "w_ffn_out": p["w_ffn_out"][l].astype(BF16),
    }


def _rope_tables(pos):
    half = MLA_ROPE // 2
    inv = ROPE_BASE ** (-jnp.arange(half, dtype=F32) / half)
    ang = pos.astype(F32)[:, None] * inv[None, :]
    cos, sin = jnp.cos(ang), jnp.sin(ang)
    c1 = jnp.concatenate([cos, cos], axis=1)
    c2 = jnp.concatenate([-sin, sin], axis=1)
    padr = ((0, 0), (MLA_KV_RANK, LANE - MLA_KV_RANK - MLA_ROPE))
    return (c1.T, c2.T, jnp.tile(c1, (1, MLA_HEADS)).T, jnp.tile(c2, (1, MLA_HEADS)).T,
            jnp.pad(c1, padr), jnp.pad(c2, padr))


def _page_copy(cache_ref, layer, page, buf, slot, p, rows, sem):
    dst = buf.at[slot, pl.ds(0, rows), pl.ds(pl.multiple_of(p * LANE, LANE), LANE)]
    return pltpu.make_async_copy(cache_ref.at[layer, page], dst, sem)


def _wait_all(buf, sem):
    pltpu.make_async_copy(buf, buf, sem).wait()


def _merge_partials(parts):
    m = parts[0][0]
    for mc, _, _ in parts[1:]:
        m = jnp.maximum(m, mc)
    l = acc = None
    for mc, lc, ac in parts:
        w = jnp.exp2(mc - m)
        l = w * lc if l is None else l + w * lc
        acc = w * ac if acc is None else acc + w * ac
    return acc / l


def _softmax_parts(parts):
    m = parts[0].max(axis=-1, keepdims=True)
    for s in parts[1:]:
        m = jnp.maximum(m, s.max(axis=-1, keepdims=True))
    ps = [jnp.exp2(s - m) for s in parts]
    l = ps[0].sum(axis=-1, keepdims=True)
    for p in ps[1:]:
        l = l + p.sum(axis=-1, keepdims=True)
    return ps, l


def _paged_body(pt_ref, q_ref, knew_ref, *rest, layer, n_seq, n_pages, rows, heads, s_len, tk,
                with_decay):
    if with_decay:
        (lfn_ref, tri_s_ref, tri_ref, cache_ref, lcache_ref, o_ref, kbuf, lbuf, sem) = rest
    else:
        (cache_ref, o_ref, kbuf, sem) = rest
    b = pl.program_id(0)
    slot = b % 2
    past = n_pages * LANE
    sp = knew_ref.shape[1]

    def issue(seq, sl):
        def body(p, c):
            pg = pt_ref[seq, p]
            _page_copy(cache_ref, layer, pg, kbuf, sl, p, rows, sem.at[sl, 0]).start()
            if with_decay:
                pltpu.make_async_copy(lcache_ref.at[layer, pg], lbuf.at[sl, p], sem.at[sl, 1]).start()
            return c
        lax.fori_loop(0, n_pages, body, 0, unroll=4)

    def wait(sl):
        _wait_all(kbuf.at[sl, pl.ds(0, rows), :], sem.at[sl, 0])
        if with_decay:
            _wait_all(lbuf.at[sl], sem.at[sl, 1])

    @pl.when(b == 0)
    def _():
        if rows < LANE:
            kbuf[:, rows:, :] = jnp.zeros((2, LANE - rows, past), F32)
        issue(0, 0)

    @pl.when(b + 1 < n_seq)
    def _():
        issue(b + 1, 1 - slot)

    wait(slot)

    q = q_ref[0]
    m_rows = q.shape[0]
    knew = knew_ref[0].astype(BF16)
    s_new = _dot_nt(q, knew)
    if with_decay:
        cin = _dot_terms(lbuf[slot].reshape(n_pages * 8, LANE), tri_ref[...], 3)
        run = jnp.zeros((8, 1), F32)
        negc = []
        for p in range(n_pages):
            cp = cin[p * 8:(p + 1) * 8]
            negc.append(-(cp + run) * LOG2E)
            run = run + cp[:, LANE - 1:]
        cs_new = _dot_f32(lfn_ref[0], tri_s_ref[...])
        s_new = s_new + jnp.concatenate([-(run + cs_new) * LOG2E] * s_len, axis=0)
    sq = jnp.right_shift(lax.broadcasted_iota(jnp.int32, (m_rows, sp), 0), _log2(heads))
    jj = lax.broadcasted_iota(jnp.int32, (m_rows, sp), 1)
    s_new = jnp.where(jj <= sq, s_new, NEG)
    m = jnp.max(s_new, axis=-1, keepdims=True)
    p = jnp.exp2(s_new - m)
    parts = [(m, jnp.sum(p, axis=-1, keepdims=True), _dot(p.astype(BF16), knew))]
    ppc = tk // LANE
    for c in range(past // tk):
        kt = kbuf[slot, :, c * tk:(c + 1) * tk].astype(BF16)
        s = _dot(q, kt)
        if with_decay:
            nc = jnp.concatenate(negc[c * ppc:(c + 1) * ppc], axis=1)
            s = s + jnp.concatenate([nc] * s_len, axis=0)
        m = jnp.max(s, axis=-1, keepdims=True)
        p = jnp.exp2(s - m)
        parts.append((m, jnp.sum(p, axis=-1, keepdims=True), _dot_nt(p.astype(BF16), kt)))
    o_ref[0] = _merge_partials(parts).astype(BF16)


def _paged_attn(page_table, q, knew, cache_t, layer, *, heads, s_len, decay=None):
    n_seq, n_pages = page_table.shape
    rows = cache_t.shape[2]
    past = n_pages * LANE
    tk = past // 2
    m = q.shape[1]
    sp = knew.shape[1]
    with_decay = decay is not None
    per_b = lambda b, pt: (b, 0, 0)
    in_specs = [pl.BlockSpec((1, m, LANE), per_b), pl.BlockSpec((1, sp, LANE), per_b)]
    args = [q, knew]
    scratch = [pltpu.VMEM((2, LANE, past), F32)]
    if with_decay:
        lfn, lcache_t = decay
        tri_s = jnp.triu(jnp.ones((sp, sp), F32))
        tri = jnp.triu(jnp.ones((LANE, LANE), F32)).astype(BF16)
        in_specs += [pl.BlockSpec((1, 8, sp), per_b), pl.BlockSpec((sp, sp), lambda b, pt: (0, 0)),
                     pl.BlockSpec((LANE, LANE), lambda b, pt: (0, 0)),
                     pl.BlockSpec(memory_space=pl.ANY), pl.BlockSpec(memory_space=pl.ANY)]
        args += [lfn, tri_s, tri, cache_t, lcache_t]
        scratch.append(pltpu.VMEM((2, n_pages, 8, LANE), F32))
    else:
        in_specs.append(pl.BlockSpec(memory_space=pl.ANY))
        args.append(cache_t)
    scratch.append(pltpu.SemaphoreType.DMA((2, 2)))
    return pl.pallas_call(
        functools.partial(_paged_body, layer=layer, n_seq=n_seq, n_pages=n_pages, rows=rows,
                          heads=heads, s_len=s_len, tk=tk, with_decay=with_decay),
        grid_spec=pltpu.PrefetchScalarGridSpec(
            num_scalar_prefetch=1, grid=(n_seq,), in_specs=in_specs,
            out_specs=pl.BlockSpec((1, m, LANE), per_b), scratch_shapes=scratch),
        out_shape=jax.ShapeDtypeStruct((n_seq, m, LANE), BF16),
        compiler_params=_cparams(1),
        name="paged_attn_decay" if with_decay else "paged_attn",
    )(page_table, *args)


def _nsa_s1_body(pt_ref, q_ref, wnew_ref, win_ref, cb_ref, wb_ref, wnb_ref, amat_ref, pair_ref,
                 cache_ref, ocmp_ref, owin_ref, idx_ref, kbuf, sem, *, layer, n_seq, n_pages, s_len,
                 k_free):
    b = pl.program_id(0)
    slot = b % 2
    past = n_pages * LANE
    hh = NSA_HEADS

    def issue(seq, sl):
        def body(p, c):
            _page_copy(cache_ref, layer, pt_ref[seq, p], kbuf, sl, p, LANE, sem.at[sl]).start()
            return c
        lax.fori_loop(0, n_pages, body, 0, unroll=4)

    @pl.when(b == 0)
    def _():
        issue(0, 0)

    @pl.when(b + 1 < n_seq)
    def _():
        issue(b + 1, 1 - slot)

    _wait_all(kbuf.at[slot], sem.at[slot])
    q = q_ref[0]
    tc = amat_ref.shape[0]
    amat = amat_ref[...]
    means = []
    for c in range(past // tc):
        means.append(_dot_terms(kbuf[slot, :, c * tc:(c + 1) * tc], amat, 2))
    kvc = jnp.concatenate(means, axis=1).astype(BF16)
    s = _dot(q, kvc) + cb_ref[...]
    e = jnp.exp2(s - jnp.max(s, axis=-1, keepdims=True))
    p = e / jnp.sum(e, axis=-1, keepdims=True)
    ocmp_ref[0] = _dot_nt(p.astype(BF16), kvc)
    ps = _dot_f32(p, pair_ref[...])
    score = ps[0:s_len]
    for h in range(1, hh):
        score = score + ps[h * s_len:(h + 1) * s_len]
    n_past = score.shape[1]
    lane = lax.broadcasted_iota(jnp.int32, score.shape, 1).astype(F32)
    work = jnp.where((lane == 0.0) | (lane == n_past - 1.0), -jnp.inf, score)
    out_lane = lax.broadcasted_iota(jnp.int32, (s_len, LANE), 1)
    idx_out = jnp.zeros((s_len, LANE), F32)
    for r in range(k_free):
        mx = jnp.max(work, axis=-1, keepdims=True)
        idx = jnp.min(jnp.where(work == mx, lane, float(n_past)), axis=-1, keepdims=True)
        idx_out = jnp.where(out_lane == r, idx, idx_out)
        work = jnp.where(lane == idx, -jnp.inf, work)
    idx_ref[0] = idx_out.astype(jnp.int32)
    wst = win_ref[0, 0].astype(BF16)
    wnew = wnew_ref[0].astype(BF16)
    (p_w, p_n), l = _softmax_parts([_dot(q, wst) + wb_ref[...], _dot_nt(q, wnew) + wnb_ref[...]])
    owin_ref[0] = (_dot_nt(p_w.astype(BF16), wst) + _dot(p_n.astype(BF16), wnew)) / l


def _nsa_s1(page_table, q, wnew, win_t, cache_t, layer, tables, *, s_len, k_free):
    n_seq, n_pages = page_table.shape
    past = n_pages * LANE
    cb, wb, wnb = tables
    m = q.shape[1]
    sp = wnew.shape[1]
    wlen = win_t.shape[3]
    tc = min(past, CMP_BLOCK * LANE)
    amat = (jnp.arange(tc)[:, None] // CMP_BLOCK == jnp.arange(tc // CMP_BLOCK)[None, :]).astype(F32)
    amat = (amat / CMP_BLOCK).astype(BF16)
    nc = past // CMP_BLOCK
    ratio = SEL_BLOCK // CMP_BLOCK
    pair = (jnp.arange(nc)[:, None] // ratio == jnp.arange(nc // ratio)[None, :]).astype(F32)
    per_b = lambda b, pt: (b, 0, 0)
    c2 = lambda b, pt: (0, 0)
    return pl.pallas_call(
        functools.partial(_nsa_s1_body, layer=layer, n_seq=n_seq, n_pages=n_pages, s_len=s_len,
                          k_free=k_free),
        grid_spec=pltpu.PrefetchScalarGridSpec(
            num_scalar_prefetch=1, grid=(n_seq,),
            in_specs=[pl.BlockSpec((1, m, LANE), per_b), pl.BlockSpec((1, sp, LANE), per_b),
                      pl.BlockSpec((1, 1, LANE, wlen), lambda b, pt: (layer, b, 0, 0)),
                      pl.BlockSpec(cb.shape, c2), pl.BlockSpec(wb.shape, c2),
                      pl.BlockSpec(wnb.shape, c2), pl.BlockSpec(amat.shape, c2),
                      pl.BlockSpec(pair.shape, c2), pl.BlockSpec(memory_space=pl.ANY)],
            out_specs=[pl.BlockSpec((1, m, LANE), per_b), pl.BlockSpec((1, m, LANE), per_b),
                       pl.BlockSpec((1, s_len, LANE), per_b)],
            scratch_shapes=[pltpu.VMEM((2, LANE, past), F32), pltpu.SemaphoreType.DMA((2,))]),
        out_shape=[jax.ShapeDtypeStruct((n_seq, m, LANE), F32),
                   jax.ShapeDtypeStruct((n_seq, m, LANE), F32),
                   jax.ShapeDtypeStruct((n_seq, s_len, LANE), jnp.int32)],
        compiler_params=_cparams(1),
        name="nsa_sample_cmp",
    )(page_table, q, wnew, win_t, cb, wb, wnb, amat, pair, cache_t)


def _nsa_s2_body(pt_ref, idx_ref, q_ref, snew_ref, g_ref, ocmp_ref, owin_ref, lb_ref, nb_ref,
                 cache_ref, o_ref, kbuf, sem, *, layer, n_seq, n_pages, s_len, k_free):
    b = pl.program_id(0)
    slot = b % 2
    n_own = s_len * k_free
    n_slots = n_own + 2
    per_page = LANE // SEL_BLOCK

    def block_of(seq, j):
        if j < n_own:
            return idx_ref[(seq * s_len + j // k_free) * LANE + j % k_free]
        return 0 if j == n_own else n_pages * per_page - 1

    def issue(seq, sl):
        for j in range(n_slots):
            blk = block_of(seq, j)
            pg = pt_ref[seq, blk >> _log2(per_page)]
            _page_copy(cache_ref, layer, pg, kbuf, sl, j, LANE, sem.at[sl]).start()

    @pl.when(b == 0)
    def _():
        issue(0, 0)

    @pl.when(b + 1 < n_seq)
    def _():
        issue(b + 1, 1 - slot)

    _wait_all(kbuf.at[slot], sem.at[slot])
    q = q_ref[0]
    m_rows = q.shape[0]
    kt = kbuf[slot].astype(BF16)
    s = _dot(q, kt)
    assert s_len & (s_len - 1) == 0
    row_s = jnp.bitwise_and(lax.broadcasted_iota(jnp.int32, (m_rows, LANE), 0), s_len - 1)
    lane_half = jnp.right_shift(lax.broadcasted_iota(jnp.int32, (m_rows, LANE), 1), _log2(SEL_BLOCK))
    bias = []
    last_page = (n_pages - 1) * per_page
    for j in range(n_slots):
        blk = block_of(b, j)
        ok = lane_half == (blk & (per_page - 1))
        if j < n_own:
            ok = ok & (row_s == j // k_free)
            near = jnp.where(blk >= last_page, lb_ref[...], 0.0)
        else:
            near = lb_ref[...] if j == n_slots - 1 else 0.0
        bias.append(jnp.where(ok, near, NEG))
    s = s + jnp.concatenate(bias, axis=1)
    snew = snew_ref[0].astype(BF16)
    (p_s, p_n), l = _softmax_parts([s, _dot_nt(q, snew) + nb_ref[...]])
    o_sel = (_dot_nt(p_s.astype(BF16), kt) + _dot(p_n.astype(BF16), snew)) / l
    g = g_ref[0]
    o = g[:, 0:1] * ocmp_ref[0] + g[:, 1:2] * o_sel + g[:, 2:3] * owin_ref[0]
    o_ref[0] = o.astype(BF16)


def _nsa_s2(page_table, idx, q, snew, gates, o_cmp, o_win, cache_t, layer, tables, *, s_len, k_free):
    n_seq, n_pages = page_table.shape
    lb, nb = tables
    m = q.shape[1]
    sp = snew.shape[1]
    n_slots = s_len * k_free + 2
    per_b = lambda b, pt, ix: (b, 0, 0)
    c2 = lambda b, pt, ix: (0, 0)
    return pl.pallas_call(
        functools.partial(_nsa_s2_body, layer=layer, n_seq=n_seq, n_pages=n_pages, s_len=s_len,
                          k_free=k_free),
        grid_spec=pltpu.PrefetchScalarGridSpec(
            num_scalar_prefetch=2, grid=(n_seq,),
            in_specs=[pl.BlockSpec((1, m, LANE), per_b), pl.BlockSpec((1, sp, LANE), per_b),
                      pl.BlockSpec((1, m, LANE), per_b), pl.BlockSpec((1, m, LANE), per_b),
                      pl.BlockSpec((1, m, LANE), per_b), pl.BlockSpec(lb.shape, c2),
                      pl.BlockSpec(nb.shape, c2), pl.BlockSpec(memory_space=pl.ANY)],
            out_specs=pl.BlockSpec((1, m, LANE), per_b),
            scratch_shapes=[pltpu.VMEM((2, LANE, n_slots * LANE), F32),
                            pltpu.SemaphoreType.DMA((2,))]),
        out_shape=jax.ShapeDtypeStruct((n_seq, m, LANE), BF16),
        compiler_params=_cparams(1),
        name="nsa_sample_sel",
    )(page_table, idx.reshape(-1), q, snew, gates, o_cmp, o_win, lb, nb, cache_t)


def _nsa_sample_tables(rel_table, past, s_len, sp, wlen):
    hh = NSA_HEADS
    far = rel_table[REL_BUCKETS - 1].astype(F32)[:, None, None]
    qpos = past + jnp.arange(s_len)
    rows = lambda a: a.reshape(hh * s_len, a.shape[-1])
    c_end = jnp.arange(past // CMP_BLOCK) * CMP_BLOCK + CMP_BLOCK - 1
    cb = rows(_rel_bias_t(rel_table, qpos[:, None] - c_end[None, :]) * LOG2E)
    dist = qpos[:, None] - (past - wlen + jnp.arange(wlen))[None, :]
    wb = rows(jnp.where(dist < WINDOW, _rel_bias_t(rel_table, dist) * LOG2E, NEG))
    dist_new = jnp.arange(s_len)[:, None] - jnp.arange(sp)[None, :]
    newb = _rel_bias_t(rel_table, dist_new)
    wnb = rows(jnp.where(dist_new >= 0, newb * LOG2E, NEG))
    assert LANE >= REL_MAX_DIST
    dist = qpos[:, None] - (past - LANE + jnp.arange(LANE))[None, :]
    lb = rows((_rel_bias_t(rel_table, dist) - far) * LOG2E)
    nb = rows(jnp.where(dist_new >= 0, (newb - far) * LOG2E, NEG))
    return (cb, wb, wnb), (lb, nb)


TQ = 256
TQ_FOX = 128
TK = 512
TM_PROMPT = 256
NEW_PAD = 16
KEY_PAD = WINDOW


def _fm_to_rows(a, mid):
    lead = a.shape[:-2]
    n = a.shape[-1]
    a = a.reshape(lead + mid + (n,))
    return jnp.moveaxis(a, -1, len(lead))


def kernel(x_prompt, x_sample, cache_nsa_cmp, cache_nsa_sel, state_nsa_win, cache_mla, cache_fox_kv, cache_fox_logf, state_ffn_conv, page_table, c_prompt, c_sample, rel_table, w_ada, b_ada, g_attn, g_ffn, w_in, mla_g_q, mla_w_uq, mla_g_kv, mla_w_uk, mla_w_uv, fox_b_f, w_br, w_o, w_ffn_in, conv_w, conv_b, w_ffn_out, g_final):
    b, t, d = x_prompt.shape
    db, s_len, _ = x_sample.shape
    depth = w_in.shape[0]
    pool = cache_nsa_cmp.shape[1]
    n_pages = page_table.shape[1]
    past = n_pages * LANE
    d_ff = w_ffn_out.shape[1]
    wlen = state_nsa_win.shape[2]
    assert cache_nsa_cmp.shape[2] == LANE and TQ >= REL_MAX_DIST and t % TK == 0
    assert t // SEL_BLOCK <= LANE and TM_PROMPT == 8 * CMP_BLOCK
    params = dict(w_in=w_in, mla_g_q=mla_g_q, mla_w_uq=mla_w_uq, mla_g_kv=mla_g_kv, mla_w_uk=mla_w_uk,
                  mla_w_uv=mla_w_uv, fox_b_f=fox_b_f, w_br=w_br, w_o=w_o, w_ffn_in=w_ffn_in,
                  conv_w=conv_w, conv_b=conv_b, w_ffn_out=w_ffn_out, g_attn=g_attn, g_ffn=g_ffn)

    n_c = b + db
    c_all = jnp.pad(jnp.concatenate([c_prompt, c_sample], axis=0), ((0, -n_c % 8), (0, 0)))
    mods = _adaln(c_all, w_ada, b_ada)

    cmp_t = jnp.transpose(cache_nsa_cmp, (0, 1, 3, 4, 2)).reshape(depth, pool, LANE, LANE)
    sel_t = jnp.transpose(cache_nsa_sel, (0, 1, 3, 4, 2)).reshape(depth, pool, LANE, LANE)
    fkv_t = jnp.transpose(cache_fox_kv, (0, 1, 3, 4, 5, 2)).reshape(depth, pool, LANE, LANE)
    mla_t = jnp.transpose(cache_mla, (0, 1, 3, 2))
    lf_t = jnp.transpose(cache_fox_logf, (0, 1, 3, 2))
    win_t = jnp.transpose(state_nsa_win, (0, 1, 3, 4, 2)).reshape(depth, db, LANE, wlen)
    conv_pre_s = jnp.transpose(state_ffn_conv, (0, 2, 1, 3))
    conv_pre_p = jnp.zeros((b, CONV_W - 1, d_ff), F32)

    rope_p = _rope_tables(jnp.arange(t))
    rope_s = _rope_tables(jnp.repeat(past + jnp.arange(s_len), db))
    tabs_p = _nsa_prompt_tables(rel_table, t, TQ)
    tabs_s1, tabs_s2 = _nsa_sample_tables(rel_table, past, s_len, NEW_PAD, wlen)
    n_sel_s = -(-(past + s_len) // SEL_BLOCK)
    k_free = min(N_SEL, n_sel_s) - 3

    tpb = t // TM_PROMPT
    idx_p = lambda i: (i // tpb, 0, 0)
    idx_s = lambda i: (0, 0, 0)
    xp = x_prompt.reshape(b * t, d)
    xs = jnp.transpose(x_sample, (1, 0, 2)).reshape(s_len * db, d)

    def new_rows(a):
        return jnp.pad(jnp.transpose(a, (2, 0, 1)), ((0, 0), (0, NEW_PAD - s_len), (0, 0)))

    def q_rows(qt, heads, width, head_major):
        q = jnp.transpose(qt.reshape(s_len, heads, LANE, db)[:, :, :width],
                          (3, 1, 0, 2) if head_major else (3, 0, 1, 2))
        return jnp.pad(q.reshape(db, s_len * heads, width), ((0, 0), (0, 0), (0, LANE - width)))

    def o_rows(o, heads, head_major):
        o = o.reshape((db, heads, s_len, LANE) if head_major else (db, s_len, heads, LANE))
        o = jnp.transpose(o, (2, 0, 1, 3) if head_major else (1, 0, 2, 3))
        return o.reshape(s_len * db, heads * LANE)

    def front_pad(a, axis):
        pads = [(0, 0)] * a.ndim
        pads[axis] = (KEY_PAD, 0)
        return jnp.pad(a, pads)

    rows_p, rows_s = [], []
    for l in range(depth):
        lw = _prep_layer(l, params)
        m6 = [mods[l][:, k * d:(k + 1) * d] for k in range(6)]
        sh_a, sc_a, gt_a, sh_f, sc_f, gt_f = [a[:b][:, None, :] for a in m6]
        sh_as, sc_as, gt_as, sh_fs, sc_fs, gt_fs = [a[b:n_c][None] for a in m6]
        last = l == depth - 1

        pre = _pre_proj(xp, sc_a, sh_a, idx_p, lw, rope_p, n_groups=b, tiles_per_group=tpb,
                        n_pos_tiles=tpb, tm=TM_PROMPT, with_keys=True)
        nc = t // CMP_BLOCK
        kvc = pre["kvc"].reshape(b, nc // 2, 2, LANE)
        kvc = jnp.transpose(kvc, (0, 2, 1, 3)).reshape(b, nc, LANE)
        kvc_k = jnp.pad(kvc[:, :, :NSA_DH], ((0, 0), (0, 0), (0, LANE - NSA_DH))).astype(BF16)
        kvct = jnp.transpose(kvc, (0, 2, 1)).astype(BF16)
        rs = lambda a: a.reshape(b, t, a.shape[-1])
        o_nsa = _nsa_prompt(pre["qnt"], kvc_k, kvct, front_pad(rs(pre["ks"]), 1),
                            front_pad(pre["vst"], 2), front_pad(rs(pre["kw"]), 1),
                            front_pad(pre["vwt"], 2), pre["gt"], tabs_p, tq=TQ, tk=TK, pad=KEY_PAD)
        o_lat = _causal_attn(pre["qmt"], rs(pre["km"]), pre["vmt"], heads=MLA_HEADS, tq=TQ, tk=TK)
        o_fox = _causal_attn(pre["qft"], rs(pre["kf"]), pre["vft"], heads=FOX_HEADS, tq=TQ_FOX, tk=TK)
        x1 = _mix(xp, sc_a, sh_a, gt_a, idx_p, lw, o_nsa, o_lat, o_fox, tm=TM_PROMPT, value_lane=0)
        xp, tail_p = _ffn(x1, sc_f, sh_f, gt_f, idx_p, lw, conv_pre_p, g_final[None, :], tm=TM_PROMPT,
                          tiles_per_seq=tpb, time_major=False, final_norm=last)
        wkeep = min(WINDOW, t)
        rows_p.append((_fm_to_rows(pre["ct"], (2, NSA_DH)), _fm_to_rows(pre["st"], (2, NSA_DH)),
                       _fm_to_rows(pre["wt"][:, :, t - wkeep:], (2, NSA_DH)),
                       _fm_to_rows(pre["mt"][:, :MLA_KV_RANK + MLA_ROPE], (MLA_KV_RANK + MLA_ROPE,)),
                       _fm_to_rows(pre["ft"], (2, 1, FOX_DH)), _fm_to_rows(pre["lft"], (FOX_HEADS,)),
                       tail_p))

        pre_s = _pre_proj(xs, sc_as, sh_as, idx_s, lw, rope_s, n_groups=s_len, tiles_per_group=1,
                          n_pos_tiles=s_len, tm=db, with_keys=False)
        qn_s = q_rows(pre_s["qnt"], NSA_HEADS, NSA_DH, True)
        gates_s = jnp.transpose(pre_s["gt"][:, :3 * NSA_HEADS].reshape(s_len, NSA_HEADS, 3, db),
                                (3, 1, 0, 2)).reshape(db, NSA_HEADS * s_len, 3)
        gates_s = jnp.pad(gates_s, ((0, 0), (0, 0), (0, LANE - 3)))
        o_cmp, o_win, idx = _nsa_s1(page_table, qn_s, new_rows(pre_s["wt"]), win_t, cmp_t, l, tabs_s1,
                                    s_len=s_len, k_free=k_free)
        o_nsa_s = _nsa_s2(page_table, idx, qn_s, new_rows(pre_s["st"]), gates_s, o_cmp, o_win, sel_t, l,
                          tabs_s2, s_len=s_len, k_free=k_free)
        o_lat_s = _paged_attn(page_table, q_rows(pre_s["qmt"], MLA_HEADS, LANE, False),
                              new_rows(pre_s["mt"]), mla_t, l, heads=MLA_HEADS, s_len=s_len)
        lfn = jnp.pad(jnp.transpose(pre_s["lft"], (2, 1, 0)), ((0, 0), (0, 0), (0, NEW_PAD - s_len)))
        o_fox_s = _paged_attn(page_table, q_rows(pre_s["qft"], FOX_HEADS, FOX_DH, False),
                              new_rows(pre_s["ft"]), fkv_t, l, heads=FOX_HEADS, s_len=s_len,
                              decay=(lfn, lf_t))
        x1s = _mix(xs, sc_as, sh_as, gt_as, idx_s, lw, o_rows(o_nsa_s, NSA_HEADS, True),
                   o_rows(o_lat_s, MLA_HEADS, False), o_rows(o_fox_s, FOX_HEADS, False), tm=db,
                   value_lane=NSA_DH)
        xs, tail_s = _ffn(x1s, sc_fs, sh_fs, gt_fs, idx_s, lw, conv_pre_s[l], g_final[None, :], tm=db,
                          tiles_per_seq=s_len, time_major=True, final_norm=last)
        wfull = jnp.concatenate([win_t[l], jnp.transpose(pre_s["wt"], (2, 1, 0))], axis=2)
        wfull = wfull[:, :, wfull.shape[2] - min(WINDOW, past + s_len):]
        fm_s = lambda a, mid: jnp.swapaxes(_fm_to_rows(a, mid), 0, 1)
        rows_s.append((fm_s(pre_s["ct"], (2, NSA_DH)), fm_s(pre_s["st"], (2, NSA_DH)),
                       _fm_to_rows(wfull, (2, NSA_DH)),
                       fm_s(pre_s["mt"][:, :MLA_KV_RANK + MLA_ROPE], (MLA_KV_RANK + MLA_ROPE,)),
                       fm_s(pre_s["ft"], (2, 1, FOX_DH)), fm_s(pre_s["lft"], (FOX_HEADS,)),
                       jnp.transpose(tail_s, (1, 0, 2))))

    y_prompt = xp.reshape(b, t, d)
    y_sample = jnp.transpose(xs.reshape(s_len, db, d), (1, 0, 2))
    outs_p = [jnp.stack(a) for a in zip(*rows_p)]
    outs_s = [jnp.stack(a) for a in zip(*rows_s)]
    return (y_prompt, y_sample, *outs_p, *outs_s)
```

```python
import functools
import math

import numpy as np
import jax
import jax.numpy as jnp
from jax import lax
from jax.experimental import pallas as pl
from jax.experimental.pallas import tpu as pltpu

F32 = jnp.float32
BF16 = jnp.bfloat16

NSA_HEADS = 4
NSA_DH = 64
CMP_BLOCK = 32
SEL_BLOCK = 64
N_SEL = 16
WINDOW = 512
MLA_HEADS = 4
MLA_NOPE = 64
MLA_ROPE = 32
MLA_KV_RANK = 64
ROPE_BASE = 10000.0
FOX_HEADS = 8
FOX_DH = 64
REL_BUCKETS = 32
REL_MAX_DIST = 128
CONV_W = 3
EPS = 1e-6
NEG = -1e30
LOG2E = math.log2(math.e)
LANE = 128
VMEM_LIMIT = 56 * 1024 * 1024

NSA_SCALE = NSA_DH ** -0.5
MLA_SCALE = (MLA_NOPE + MLA_ROPE) ** -0.5
FOX_SCALE = FOX_DH ** -0.5

ONES_ROW = 64
ONES_ROWS = 8
DECAY_PARTS = 3

_NT = (((1,), (1,)), ((), ()))


def _cparams(n_axes):
    return pltpu.CompilerParams(dimension_semantics=("arbitrary",) * n_axes,
                                vmem_limit_bytes=VMEM_LIMIT)


def _dot(a, b):
    return jnp.dot(a, b, preferred_element_type=F32)


def _dot_nt(a, b):
    return lax.dot_general(a, b, _NT, preferred_element_type=F32)


def _dot_f32(a, b):
    return jnp.dot(a, b, preferred_element_type=F32, precision=lax.Precision.HIGHEST)


def _dot_terms(x, w, terms):
    out = None
    rest = x
    for _ in range(terms):
        piece = rest.astype(BF16)
        rest = rest - piece.astype(F32)
        d = _dot(piece, w)
        out = d if out is None else out + d
    return out


def _rms(x, g):
    return x * lax.rsqrt(jnp.mean(x * x, axis=-1, keepdims=True) + EPS) * g


def _log2(n):
    assert n & (n - 1) == 0, n
    return n.bit_length() - 1


def _ada_body(c_ref, w_ref, b_ref, o_ref):
    c = c_ref[...]
    sc = (c * jax.nn.sigmoid(c)).astype(BF16)
    o_ref[0] = _dot(sc, w_ref[0].astype(BF16)) + b_ref[0]


def _adaln(c_all, w_ada, b_ada):
    depth, d, n6 = w_ada.shape
    rows = c_all.shape[0]
    tn = 512
    return pl.pallas_call(
        _ada_body,
        grid=(depth, n6 // tn),
        in_specs=[pl.BlockSpec((rows, d), lambda l, n: (0, 0)),
                  pl.BlockSpec((1, d, tn), lambda l, n: (l, 0, n)),
                  pl.BlockSpec((1, 1, tn), lambda l, n: (l, 0, n))],
        out_specs=pl.BlockSpec((1, rows, tn), lambda l, n: (l, 0, n)),
        out_shape=jax.ShapeDtypeStruct((depth, rows, n6), F32),
        compiler_params=_cparams(2),
        name="adaln",
    )(c_all, w_ada, b_ada.reshape(depth, 1, n6))


_T_C, _T_S, _T_W, _T_F, _T_M, _T_LF, _T_G, _T_QD = 0, 128, 256, 384, 512, 640, 648, 664
_T_QN = _T_QD + 256
_T_QF = _T_QN + NSA_HEADS * LANE
_T_END = _T_QF + FOX_HEADS * LANE
_R_KS, _R_KW, _R_KF, _R_KM, _R_KMS, _R_C, _R_LF, _R_END = 0, 128, 256, 384, 512, 640, 768, 896


def _value_tile(vt):
    n = vt.shape[1]
    return jnp.concatenate([vt, jnp.ones((ONES_ROWS, n), F32),
                            jnp.zeros((LANE - ONES_ROW - ONES_ROWS, n), F32)], axis=0).astype(BF16)


def _pre_body(x_ref, sc_ref, sh_ref, g_ref, wrow_ref, wt_ref, gq_ref, wuq_ref, wuk_ref, pp_ref,
              gkv_ref, gkvr_ref, bf_ref, bfr_ref, aug_ref, cs1_ref, cs2_ref, csq1_ref, csq2_ref,
              cr1_ref, cr2_ref, tril_ref, avg_ref,
              ct_ref, st_ref, wtt_ref, ft_ref, mt_ref, lf_ref, gt_ref, qn_ref, qf_ref, qm_ref,
              vs_ref, vw_ref, vf_ref, vm_ref, ks_ref, kw_ref, kf_ref, km_ref, kvc_ref,
              carry_ref, *, tiles_per_seq, with_keys):
    x = x_ref[...]
    tm = x.shape[0]
    h = _rms(x, g_ref[...]) * (1.0 + sc_ref[0]) + sh_ref[0]
    hb = h.astype(BF16)

    pt = _dot_nt(wt_ref[...], hb)
    ct_ref[0] = pt[_T_C:_T_C + 128]
    st_ref[0] = pt[_T_S:_T_S + 128]
    wtt_ref[0] = pt[_T_W:_T_W + 128]
    ft_ref[0] = pt[_T_F:_T_F + 128]
    ckv = pt[_T_M:_T_M + 64]
    ckv = ckv * lax.rsqrt(jnp.mean(ckv * ckv, axis=0, keepdims=True) + EPS) * gkv_ref[...]
    krot = pt[_T_M + 64:_T_M + 96] * cs1_ref[...] + pt[_T_M + 96:_T_M + 128] * cs2_ref[...]
    mt_ref[0, 0:64] = ckv
    mt_ref[0, 64:96] = krot
    mt_ref[0, 96:128] = jnp.zeros_like(krot)
    lf_ref[0] = jax.nn.log_sigmoid(pt[_T_LF:_T_LF + 8] + bf_ref[...])
    gt_ref[0] = jax.nn.sigmoid(pt[_T_G:_T_G + 16])
    qn_ref[0] = pt[_T_QN:_T_QF].astype(BF16)
    qf_ref[0] = (pt[_T_QF:_T_END] + aug_ref[...]).astype(BF16)
    qd = pt[_T_QD:_T_QD + 256]
    qd = qd * lax.rsqrt(jnp.mean(qd * qd, axis=0, keepdims=True) + EPS) * gq_ref[...]
    q = _dot(wuq_ref[...], qd.astype(BF16))
    qrot = q[256:384] * csq1_ref[...] + q[384:512] * csq2_ref[...]
    qm = _dot(wuk_ref[...], q[:256].astype(BF16)) * (MLA_SCALE * LOG2E)
    qm = qm + _dot(pp_ref[...], (qrot * (MLA_SCALE * LOG2E)).astype(BF16))
    qm_ref[0] = qm.astype(BF16)
    vs_ref[0] = _value_tile(pt[_T_S + 64:_T_S + 128])
    vw_ref[0] = _value_tile(pt[_T_W + 64:_T_W + 128])
    vf_ref[0] = _value_tile(pt[_T_F + 64:_T_F + 128])
    vm_ref[0] = _value_tile(ckv)

    if with_keys:
        pr = _dot(hb, wrow_ref[...])
        lane = lax.broadcasted_iota(jnp.int32, (tm, LANE), 1)
        t_in_seq = (pl.program_id(0) % tiles_per_seq) * tm
        pos = t_in_seq + lax.broadcasted_iota(jnp.int32, (tm, LANE), 0)
        blk_ind = jnp.where(lane == jnp.right_shift(pos, _log2(SEL_BLOCK)), 1.0, 0.0)
        ks_ref[:, 0:LANE] = pr[:, _R_KS:_R_KS + LANE].astype(BF16)
        ks_ref[:, LANE:2 * LANE] = blk_ind.astype(BF16)
        kw_ref[...] = pr[:, _R_KW:_R_KW + LANE].astype(BF16)
        kvd = pr[:, _R_KM:_R_KM + LANE]
        is_c = lane < MLA_KV_RANK
        ms = jnp.sum(jnp.where(is_c, kvd * kvd, 0.0), axis=1, keepdims=True) / MLA_KV_RANK
        km = jnp.where(is_c, kvd * lax.rsqrt(ms + EPS) * gkvr_ref[...],
                       kvd * cr1_ref[...] + pr[:, _R_KMS:_R_KMS + LANE] * cr2_ref[...])
        km_ref[...] = km.astype(BF16)
        @pl.when(pl.program_id(0) % tiles_per_seq == 0)
        def _():
            carry_ref[...] = jnp.zeros_like(carry_ref)
        lfr = jnp.where(lane < FOX_HEADS, jax.nn.log_sigmoid(pr[:, _R_LF:_R_LF + LANE] + bfr_ref[...]),
                        0.0)
        csum = _dot_f32(tril_ref[...], lfr) + carry_ref[0:1]
        carry_ref[...] = jnp.broadcast_to(csum[tm - 1:], carry_ref.shape)
        kf = pr[:, _R_KF:_R_KF + LANE]
        rest = -csum * LOG2E
        for part in range(DECAY_PARTS):
            term = rest.astype(BF16).astype(F32)
            rest = rest - term
            kf = kf + pltpu.roll(term, FOX_DH + part * FOX_HEADS, 1)
        kf_ref[...] = kf.astype(BF16)
        kvc_ref[...] = _dot_f32(avg_ref[...], pr[:, _R_C:_R_C + LANE])
    else:
        ks_ref[...] = jnp.zeros(ks_ref.shape, BF16)
        kw_ref[...] = jnp.zeros(kw_ref.shape, BF16)
        kf_ref[...] = jnp.zeros(kf_ref.shape, BF16)
        km_ref[...] = jnp.zeros(km_ref.shape, BF16)
        kvc_ref[...] = jnp.zeros(kvc_ref.shape, F32)


def _pre_proj(x, sc, sh, mod_index, lw, pos_tabs, *, n_groups, tiles_per_group, n_pos_tiles, tm,
              with_keys):
    n, d = x.shape
    cs1, cs2, csq1, csq2, cr1, cr2 = pos_tabs
    n_tiles = n // tm
    tpg = tiles_per_group
    tril = jnp.tril(jnp.ones((tm, tm), F32))
    nmean = 8
    avg = (jnp.arange(tm)[None, :] // CMP_BLOCK == jnp.arange(nmean)[:, None]).astype(F32) / CMP_BLOCK
    row = lambda i: (i, 0)
    const2 = lambda i: (0, 0)
    tcol = lambda i: (i // tpg, 0, i % tpg)
    pcol = lambda i: (0, i % n_pos_tiles)
    prow = lambda i: (i % n_pos_tiles, 0)
    ncols = tpg * tm

    def tspec(r):
        return pl.BlockSpec((1, r, tm), tcol)

    def tshape(r, dt=F32):
        return jax.ShapeDtypeStruct((n_groups, r, ncols), dt)

    nq = NSA_HEADS * LANE
    nf = FOX_HEADS * LANE
    consts = [lw["g_attn"], lw["w_row"], lw["w_t"], lw["g_q"], lw["w_uq"], lw["w_uk"], lw["pp"],
              lw["g_kv"], lw["g_kv_row"], lw["b_f"], lw["b_f_row"], lw["q_aug"]]
    outs = pl.pallas_call(
        functools.partial(_pre_body, tiles_per_seq=tpg, with_keys=with_keys),
        grid=(n_tiles,),
        in_specs=[pl.BlockSpec((tm, d), row),
                  pl.BlockSpec((1,) + sc.shape[1:], mod_index),
                  pl.BlockSpec((1,) + sh.shape[1:], mod_index)]
        + [pl.BlockSpec(c.shape, const2) for c in consts]
        + [pl.BlockSpec((32, tm), pcol), pl.BlockSpec((32, tm), pcol),
           pl.BlockSpec((128, tm), pcol), pl.BlockSpec((128, tm), pcol),
           pl.BlockSpec((tm, 128), prow), pl.BlockSpec((tm, 128), prow),
           pl.BlockSpec((tm, tm), const2), pl.BlockSpec((nmean, tm), const2)],
        out_specs=[tspec(128), tspec(128), tspec(128), tspec(128), tspec(128), tspec(8), tspec(16),
                   tspec(nq), tspec(nf), tspec(nq), tspec(128), tspec(128), tspec(128), tspec(128),
                   pl.BlockSpec((tm, 2 * LANE), row), pl.BlockSpec((tm, LANE), row),
                   pl.BlockSpec((tm, LANE), row), pl.BlockSpec((tm, LANE), row),
                   pl.BlockSpec((nmean, LANE), row)],
        out_shape=[tshape(128), tshape(128), tshape(128), tshape(128), tshape(128), tshape(8),
                   tshape(16), tshape(nq, BF16), tshape(nf, BF16), tshape(nq, BF16),
                   tshape(128, BF16), tshape(128, BF16), tshape(128, BF16), tshape(128, BF16),
                   jax.ShapeDtypeStruct((n, 2 * LANE), BF16), jax.ShapeDtypeStruct((n, LANE), BF16),
                   jax.ShapeDtypeStruct((n, LANE), BF16), jax.ShapeDtypeStruct((n, LANE), BF16),
                   jax.ShapeDtypeStruct((n_tiles * nmean, LANE), F32)],
        scratch_shapes=[pltpu.VMEM((8, LANE), F32)],
        compiler_params=_cparams(1),
        name="pre_proj",
    )(x, sc, sh, *consts, cs1, cs2, csq1, csq2, cr1, cr2, tril, avg)
    keys = ("ct", "st", "wt", "ft", "mt", "lft", "gt", "qnt", "qft", "qmt", "vst", "vwt", "vft", "vmt",
            "ks", "kw", "kf", "km", "kvc")
    return dict(zip(keys, outs))


def _tflash_init(m_ref, acc_ref):
    m_ref[...] = jnp.full(m_ref.shape, NEG, F32)
    acc_ref[...] = jnp.zeros(acc_ref.shape, F32)


def _tflash_update(s, vt, m_ref, acc_ref):
    _tflash_update_many([(s, vt)], m_ref, acc_ref)


def _tflash_update_many(tiles, m_ref, acc_ref):
    m_old = m_ref[...]
    parts = []
    m_new = m_old
    for s, vt in tiles:
        mk = jnp.max(s, axis=0, keepdims=True)
        parts.append((mk, _dot(vt, jnp.exp2(s - mk).astype(BF16))))
        m_new = jnp.maximum(m_new, mk)
    acc = jnp.exp2(m_old - m_new) * acc_ref[...]
    for mk, pv in parts:
        acc = acc + jnp.exp2(mk - m_new) * pv
    acc_ref[...] = acc
    m_ref[...] = m_new


def _tflash_out(acc, heads, tq):
    o = acc / acc[ONES_ROW:ONES_ROW + 1]
    return jnp.concatenate([o[:, h * tq:(h + 1) * tq].T for h in range(heads)], axis=1).astype(BF16)


def _lane_stack(q_ref, heads):
    return jnp.concatenate([q_ref[0, h * LANE:(h + 1) * LANE, :] for h in range(heads)], axis=1)


def _causal_body(q_ref, k_ref, v_ref, o_ref, m_ref, acc_ref, *, heads, tq, tk, group):
    i = pl.program_id(1)
    q0 = i * tq
    qt = _lane_stack(q_ref, heads)
    _tflash_init(m_ref, acc_ref)

    def tile(c0):
        return _dot(k_ref[0, pl.ds(c0, tk), :], qt), v_ref[0, :, pl.ds(c0, tk)]

    def group_step(j, carry):
        c0 = pl.multiple_of(j * group * tk, group * tk)
        _tflash_update_many([tile(c0 + k * tk) for k in range(group)], m_ref, acc_ref)
        return carry

    def full_step(j, carry):
        s, vt = tile(pl.multiple_of(j * tk, tk))
        _tflash_update(s, vt, m_ref, acc_ref)
        return carry

    n_full = q0 // tk
    n_groups = n_full // group
    lax.fori_loop(0, n_groups, group_step, 0)
    lax.fori_loop(group * n_groups, n_full, full_step, 0)
    c0 = pl.multiple_of(n_full * tk, tk)
    s, vt = tile(c0)
    key = c0 + lax.broadcasted_iota(jnp.int32, s.shape, 0)
    qpos = q0 + jnp.bitwise_and(lax.broadcasted_iota(jnp.int32, s.shape, 1), tq - 1)
    _tflash_update(jnp.where(key <= qpos, s, NEG), vt, m_ref, acc_ref)
    o_ref[...] = _tflash_out(acc_ref[...], heads, tq)


def _causal_attn(qt, k, vt, *, heads, tq, tk, group):
    b, _, t = qt.shape
    nq = t // tq
    assert tq & (tq - 1) == 0 and tk % tq == 0 and t % tk == 0
    m = heads * tq
    return pl.pallas_call(
        functools.partial(_causal_body, heads=heads, tq=tq, tk=tk, group=group),
        grid=(b, nq),
        in_specs=[pl.BlockSpec((1, heads * LANE, tq), lambda bi, i: (bi, 0, i)),
                  pl.BlockSpec((1, t, LANE), lambda bi, i: (bi, 0, 0)),
                  pl.BlockSpec((1, LANE, t), lambda bi, i: (bi, 0, 0))],
        out_specs=pl.BlockSpec((tq, heads * LANE), lambda bi, i: (bi * nq + i, 0)),
        out_shape=jax.ShapeDtypeStruct((b * t, heads * LANE), BF16),
        scratch_shapes=[pltpu.VMEM((1, m), F32), pltpu.VMEM((LANE, m), F32)],
        compiler_params=_cparams(2),
        name="causal_attn_h%d" % heads,
    )(qt, k, vt)


def _top_k_neg_mask_t(score, k):
    n = score.shape[0]
    rowi = lax.broadcasted_iota(jnp.int32, score.shape, 0).astype(F32)
    out = jnp.full(score.shape, NEG, F32)
    work = score
    for _ in range(k):
        mx = jnp.max(work, axis=0, keepdims=True)
        idx = jnp.min(jnp.where(work == mx, rowi, float(n)), axis=0, keepdims=True)
        hit = rowi == idx
        out = jnp.where(hit, 0.0, out)
        work = jnp.where(hit, -jnp.inf, work)
    return out


def _nsa_body(q_ref, kvc_ref, kvct_ref, ks_ref, vs_ref, kw_ref, vw_ref, g_ref, cb_ref, sb_ref, wb_ref,
              o_ref, m_ref, acc_ref, m2_ref, acc2_ref, *, tq, tk, pad):
    hh = NSA_HEADS
    i = pl.program_id(1)
    q0 = i * tq
    qt = _lane_stack(q_ref, hh)
    cb = jnp.concatenate([cb_ref[h] for h in range(hh)], axis=1)
    s = _dot(kvc_ref[0], qt) + cb
    e = jnp.exp2(s - jnp.max(s, axis=0, keepdims=True))
    p = e / jnp.sum(e, axis=0, keepdims=True)
    p = jnp.where(cb > 0.5 * NEG, p, 0.0)
    o_cmp = _dot(kvct_ref[0], p.astype(BF16))
    nc = p.shape[0]
    pc = p[:, 0:tq]
    for h in range(1, hh):
        pc = pc + p[:, h * tq:(h + 1) * tq]
    n_sel = nc // 2
    score = pc[:n_sel] + pc[n_sel:]
    blk = lax.broadcasted_iota(jnp.int32, (n_sel, tq), 0)
    qpos = q0 + lax.broadcasted_iota(jnp.int32, (n_sel, tq), 1)
    cur = jnp.right_shift(qpos, _log2(SEL_BLOCK))
    forced = (blk == 0) | (blk == cur) | (blk == cur - 1)
    future = blk * SEL_BLOCK > qpos
    score = jnp.where(forced, 1e6, jnp.where(future, -1e6, score))
    selneg = _top_k_neg_mask_t(score, min(N_SEL, n_sel)).astype(BF16)
    if n_sel < LANE:
        selneg = jnp.concatenate([selneg, jnp.zeros((LANE - n_sel, tq), BF16)], axis=0)
    qa = jnp.concatenate([qt, jnp.concatenate([selneg] * hh, axis=1)], axis=0)

    _tflash_init(m_ref, acc_ref)

    def far_tile(c0, n):
        return _dot(ks_ref[0, pl.ds(pad + c0, n), :], qa), vs_ref[0, :, pl.ds(pad + c0, n)]

    def far(c0, n):
        _tflash_update_many([far_tile(c0 + k * tk, min(n, tk)) for k in range(max(n // tk, 1))],
                            m_ref, acc_ref)

    n_far = jnp.maximum(i - 1, 0)
    per = GROUP * tk // tq
    n_big = n_far // per

    def big_step(j, carry):
        far(pl.multiple_of(j * GROUP * tk, GROUP * tk), GROUP * tk)
        return carry

    def small_step(j, carry):
        far(pl.multiple_of(j * tq, tq), tq)
        return carry

    lax.fori_loop(0, n_big, big_step, 0)
    lax.fori_loop(n_big * per, n_far, small_step, 0)
    c0 = pl.multiple_of(q0 - tq, tq)
    s_n = _dot(ks_ref[0, pl.ds(pad + c0, 2 * tq), :], qa) + sb_ref[...]
    key = c0 + lax.broadcasted_iota(jnp.int32, s_n.shape, 0)
    _tflash_update(jnp.where(key >= 0, s_n, NEG), vs_ref[0, :, pl.ds(pad + c0, 2 * tq)], m_ref, acc_ref)
    acc_s = acc_ref[...]
    o_sel = acc_s / acc_s[ONES_ROW:ONES_ROW + 1]

    c0 = pl.multiple_of(q0 - WINDOW, tq)
    nw = WINDOW + tq
    s_w = _dot(kw_ref[0, pl.ds(pad + c0, nw), :], qt) + wb_ref[...]
    key = c0 + lax.broadcasted_iota(jnp.int32, s_w.shape, 0)
    _tflash_init(m2_ref, acc2_ref)
    _tflash_update(jnp.where(key >= 0, s_w, NEG), vw_ref[0, :, pl.ds(pad + c0, nw)], m2_ref, acc2_ref)
    acc_w = acc2_ref[...]
    o_win = acc_w / acc_w[ONES_ROW:ONES_ROW + 1]

    g = g_ref[0]
    outs = []
    for h in range(hh):
        sl = slice(h * tq, (h + 1) * tq)
        o_h = (g[3 * h:3 * h + 1] * o_cmp[NSA_DH:, sl] + g[3 * h + 1:3 * h + 2] * o_sel[:NSA_DH, sl]
               + g[3 * h + 2:3 * h + 3] * o_win[:NSA_DH, sl])
        outs.append(jnp.concatenate([o_h, jnp.zeros_like(o_h)], axis=0).T)
    o_ref[...] = jnp.concatenate(outs, axis=1).astype(BF16)


def _rel_bucket(dist):
    d = jnp.maximum(dist, 0)
    exact = REL_BUCKETS // 2
    scaled = jnp.log(jnp.maximum(d, 1).astype(F32) / exact) / math.log(REL_MAX_DIST / exact)
    large = jnp.minimum(exact + (scaled * (REL_BUCKETS - exact)).astype(jnp.int32), REL_BUCKETS - 1)
    return jnp.where(d < exact, d, large)


def _rel_bias_t(table, dist):
    bucket = _rel_bucket(dist)
    tab = table.astype(F32)
    out = jnp.zeros((tab.shape[1],) + dist.shape, F32)
    extra = (None,) * dist.ndim
    for k in range(REL_BUCKETS):
        out = jnp.where(bucket[None] == k, tab[k][(slice(None),) + extra], out)
    return out


def _head_lanes(a):
    return jnp.concatenate([a[h] for h in range(a.shape[0])], axis=1)


def _nsa_prompt_tables(rel_table, t, tq):
    far = rel_table[REL_BUCKETS - 1].astype(F32)[:, None, None]
    nc = t // CMP_BLOCK
    order = jnp.concatenate([jnp.arange(0, nc, 2), jnp.arange(1, nc, 2)])
    c_end = order * CMP_BLOCK + CMP_BLOCK - 1
    dist = jnp.arange(t)[None, :] - c_end[:, None]
    cb = jnp.where(dist >= 0, _rel_bias_t(rel_table, dist) * LOG2E, NEG)
    dist = jnp.arange(tq)[None, :] + tq - jnp.arange(2 * tq)[:, None]
    sb = jnp.where(dist >= 0, (_rel_bias_t(rel_table, dist) - far) * LOG2E, NEG)
    dist = jnp.arange(tq)[None, :] + WINDOW - jnp.arange(WINDOW + tq)[:, None]
    wb = jnp.where((dist >= 0) & (dist < WINDOW), _rel_bias_t(rel_table, dist) * LOG2E, NEG)
    return cb, _head_lanes(sb), _head_lanes(wb)


def _nsa_prompt(qnt, kvc, kvct, ks, vst, kw, vwt, gt, tables, *, tq, tk, pad):
    b, _, t = qnt.shape
    nq = t // tq
    cb, sb, wb = tables
    nc = kvc.shape[1]
    hh = NSA_HEADS
    m = hh * tq
    per_b = lambda bi, i: (bi, 0, 0)
    tile = lambda bi, i: (bi, 0, i)
    return pl.pallas_call(
        functools.partial(_nsa_body, tq=tq, tk=tk, pad=pad),
        grid=(b, nq),
        in_specs=[pl.BlockSpec((1, hh * LANE, tq), tile),
                  pl.BlockSpec((1, nc, LANE), per_b),
                  pl.BlockSpec((1, LANE, nc), per_b),
                  pl.BlockSpec((1,) + ks.shape[1:], per_b),
                  pl.BlockSpec((1,) + vst.shape[1:], per_b),
                  pl.BlockSpec((1,) + kw.shape[1:], per_b),
                  pl.BlockSpec((1,) + vwt.shape[1:], per_b),
                  pl.BlockSpec((1, 16, tq), tile),
                  pl.BlockSpec((hh, nc, tq), lambda bi, i: (0, 0, i)),
                  pl.BlockSpec(sb.shape, lambda bi, i: (0, 0)),
                  pl.BlockSpec(wb.shape, lambda bi, i: (0, 0))],
        out_specs=pl.BlockSpec((tq, hh * LANE), lambda bi, i: (bi * nq + i, 0)),
        out_shape=jax.ShapeDtypeStruct((b * t, hh * LANE), BF16),
        scratch_shapes=[pltpu.VMEM((1, m), F32), pltpu.VMEM((LANE, m), F32),
                        pltpu.VMEM((1, m), F32), pltpu.VMEM((LANE, m), F32)],
        compiler_params=_cparams(2),
        name="nsa_prompt",
    )(qnt, kvc, kvct, ks, vst, kw, vwt, gt, cb, sb, wb)


def _mix_body(x_ref, sc_ref, sh_ref, gt_ref, g_ref, on_ref, om_ref, of_ref, wmg_ref, wbn_ref,
              wuv_ref, wbm_ref, wbf_ref, wo_ref, o_ref):
    x = x_ref[...]
    d = x.shape[1]
    h = _rms(x, g_ref[...]) * (1.0 + sc_ref[0]) + sh_ref[0]
    mg = jax.nn.sigmoid(_dot(h.astype(BF16), wmg_ref[...]))
    o_mla = _dot(om_ref[...], wuv_ref[...]).astype(BF16)
    t = mg[:, :d] * _dot(on_ref[...], wbn_ref[...])
    t = t + mg[:, d:2 * d] * _dot(o_mla, wbm_ref[...])
    t = t + mg[:, 2 * d:] * _dot(of_ref[...], wbf_ref[...])
    o_ref[...] = x + gt_ref[0] * _dot(t.astype(BF16), wo_ref[...])


def _mix(x, sc, sh, gt, mod_index, lw, o_nsa, o_lat, o_fox, *, tm, value_lane):
    n, d = x.shape
    row = lambda i: (i, 0)
    const2 = lambda i: (0, 0)
    ws = [lw["w_mg"], lw["w_br_n"][value_lane], lw["w_uv"], lw["w_br_m"], lw["w_br_f"][value_lane],
          lw["w_o"]]
    return pl.pallas_call(
        _mix_body,
        grid=(n // tm,),
        in_specs=[pl.BlockSpec((tm, d), row)]
        + [pl.BlockSpec((1,) + a.shape[1:], mod_index) for a in (sc, sh, gt)]
        + [pl.BlockSpec((1, d), const2)]
        + [pl.BlockSpec((tm, a.shape[1]), row) for a in (o_nsa, o_lat, o_fox)]
        + [pl.BlockSpec(w.shape, const2) for w in ws],
        out_specs=pl.BlockSpec((tm, d), row),
        out_shape=jax.ShapeDtypeStruct((n, d), F32),
        compiler_params=_cparams(1),
        name="mix",
    )(x, sc, sh, gt, lw["g_attn"], o_nsa, o_lat, o_fox, *ws)


def _gelu_tanh(x):
    return 0.5 * x * (1.0 + jnp.tanh(math.sqrt(2.0 / math.pi) * (x + 0.044715 * (x * x * x))))


def _ffn_body(x_ref, sc_ref, sh_ref, gt_ref, g_ref, pre_ref, win_ref, cw_ref, cb_ref, wout_ref,
              gf_ref, o_ref, tail_ref, hist_ref, *, tiles_per_seq, time_major, final_norm, d_ff):
    x = x_ref[...]
    tm = x.shape[0]
    i = pl.program_id(0)
    h = _rms(x, g_ref[...]) * (1.0 + sc_ref[0]) + sh_ref[0]
    ab = _dot(h.astype(BF16), win_ref[...])
    a = ab[:, :d_ff]
    b = ab[:, d_ff:]
    cw = cw_ref[...]
    if time_major:
        @pl.when(i % tiles_per_seq == 0)
        def _():
            hist_ref[0] = pre_ref[0]
            hist_ref[1] = pre_ref[1]
        a2 = hist_ref[0]
        a1 = hist_ref[1]
        hist_ref[0] = a1
        hist_ref[1] = a
        tail_ref[0] = a
    else:
        @pl.when(i % tiles_per_seq == 0)
        def _():
            hist_ref[0, 6:8] = pre_ref[0]
        prev = hist_ref[0, 6:8]
        rid = lax.broadcasted_iota(jnp.int32, (tm, 1), 0)
        a1 = jnp.where(rid == 0, prev[1:2], pltpu.roll(a, 1, 0))
        a2 = jnp.where(rid == 0, prev[0:1], jnp.where(rid == 1, prev[1:2], pltpu.roll(a, 2, 0)))
        hist_ref[0] = a[tm - 8:]
        tail_ref[0] = a[tm - 2:]
    conv = a2 * cw[0:1] + a1 * cw[1:2] + a * cw[2:3] + cb_ref[...]
    y = _dot((_gelu_tanh(conv) * b).astype(BF16), wout_ref[...])
    out = x + gt_ref[0] * y
    if final_norm:
        out = _rms(out, gf_ref[...])
    o_ref[...] = out


def _ffn(x, sc, sh, gt, mod_index, lw, prefix, g_final, *, tm, tiles_per_seq, time_major, final_norm):
    n, d = x.shape
    d_ff = lw["w_ffn_out"].shape[0]
    row = lambda i: (i, 0)
    const2 = lambda i: (0, 0)
    n_tiles = n // tm
    if time_major:
        pre_spec = pl.BlockSpec(prefix.shape, lambda i: (0, 0, 0))
        tail_spec = pl.BlockSpec((1, tm, d_ff),
                                 lambda i: (jnp.maximum(i - (tiles_per_seq - 2), 0), 0, 0))
        tail_shape = jax.ShapeDtypeStruct((2, tm, d_ff), F32)
        hist = pltpu.VMEM((2, tm, d_ff), F32)
    else:
        pre_spec = pl.BlockSpec((1, 2, d_ff), lambda i: (i // tiles_per_seq, 0, 0))
        tail_spec = pl.BlockSpec((1, 2, d_ff), lambda i: (i // tiles_per_seq, 0, 0))
        tail_shape = jax.ShapeDtypeStruct((n_tiles // tiles_per_seq, 2, d_ff), F32)
        hist = pltpu.VMEM((1, 8, d_ff), F32)
    return pl.pallas_call(
        functools.partial(_ffn_body, tiles_per_seq=tiles_per_seq, time_major=time_major,
                          final_norm=final_norm, d_ff=d_ff),
        grid=(n_tiles,),
        in_specs=[pl.BlockSpec((tm, d), row)]
        + [pl.BlockSpec((1,) + a.shape[1:], mod_index) for a in (sc, sh, gt)]
        + [pl.BlockSpec((1, d), const2), pre_spec,
           pl.BlockSpec(lw["w_ffn_in"].shape, const2),
           pl.BlockSpec((CONV_W, d_ff), const2), pl.BlockSpec((1, d_ff), const2),
           pl.BlockSpec(lw["w_ffn_out"].shape, const2), pl.BlockSpec((1, d), const2)],
        out_specs=[pl.BlockSpec((tm, d), row), tail_spec],
        out_shape=[jax.ShapeDtypeStruct((n, d), F32), tail_shape],
        scratch_shapes=[hist],
        compiler_params=_cparams(1),
        name="conv_ffn",
    )(x, sc, sh, gt, lw["g_ffn"], prefix, lw["w_ffn_in"], lw["conv_w"], lw["conv_b"],
      lw["w_ffn_out"], g_final)


def _pad_heads(w, heads, dh, scale=1.0):
    k = w.shape[0]
    w = (w * scale).reshape(k, heads, dh)
    return jnp.pad(w, ((0, 0), (0, 0), (0, LANE - dh))).reshape(k, heads * LANE)


def _pad_head_rows(w, heads, dh, offset):
    n = w.shape[1]
    w = w.reshape(heads, dh, n)
    return jnp.pad(w, ((0, 0), (offset, LANE - dh - offset), (0, 0))).reshape(heads * LANE, n)


def _pad_cols(w, n=LANE):
    return jnp.pad(w, ((0, 0), (0, n - w.shape[1])))


def _prep_layer(l, p):
    d = p["w_in"].shape[1]
    w_in = p["w_in"][l]
    nsa_w = NSA_HEADS * NSA_DH
    fox_w = FOX_HEADS * FOX_DH
    q_rank = p["mla_g_q"].shape[1]
    kv_w = MLA_KV_RANK + MLA_ROPE
    splits = (nsa_w, 2 * NSA_DH, 2 * NSA_DH, 2 * NSA_DH, 3 * NSA_HEADS, q_rank, kv_w,
              fox_w, 2 * FOX_DH, FOX_HEADS, 3 * d)
    cuts = [int(c) for c in np.cumsum(splits)[:-1]]
    (w_nq, w_nc, w_ns, w_nw, w_ng, w_qd, w_kvd, w_fq, w_fkv, w_ff, w_mg) = jnp.split(w_in, cuts, axis=1)
    half = MLA_ROPE // 2
    w_kr = w_kvd[:, MLA_KV_RANK:]
    w_kr_sw = jnp.concatenate([w_kr[:, half:], w_kr[:, :half]], axis=1)
    w_t = jnp.concatenate([w_nc, w_ns, w_nw, w_fkv, w_kvd, w_kr_sw, w_ff, _pad_cols(w_ng, 16), w_qd,
                           _pad_heads(w_nq, NSA_HEADS, NSA_DH, NSA_SCALE * LOG2E),
                           _pad_heads(w_fq, FOX_HEADS, FOX_DH, FOX_SCALE * LOG2E)], axis=1)
    assert w_t.shape[1] == _T_END
    w_row = jnp.concatenate([_pad_cols(w_ns[:, :NSA_DH]), _pad_cols(w_nw[:, :NSA_DH]),
                             _pad_cols(w_fkv[:, :FOX_DH]), _pad_cols(w_kvd),
                             _pad_cols(jnp.pad(w_kr_sw, ((0, 0), (MLA_KV_RANK, 0)))),
                             w_nc, _pad_cols(w_ff)], axis=1)
    assert w_row.shape[1] == _R_END
    aug = np.zeros((FOX_HEADS, LANE), np.float32)
    for h in range(FOX_HEADS):
        for part in range(DECAY_PARTS):
            aug[h, FOX_DH + part * FOX_HEADS + h] = 1.0
    w_uq = p["mla_w_uq"][l].reshape(q_rank, MLA_HEADS, MLA_NOPE + MLA_ROPE)
    uq_nope = w_uq[:, :, :MLA_NOPE].reshape(q_rank, -1)
    uq_rope = w_uq[:, :, MLA_NOPE:]
    uq_rope_sw = jnp.concatenate([uq_rope[:, :, half:], uq_rope[:, :, :half]], axis=2)
    w_uq2 = jnp.concatenate([uq_nope, uq_rope.reshape(q_rank, -1), uq_rope_sw.reshape(q_rank, -1)],
                            axis=1)
    w_uk = p["mla_w_uk"][l]
    eye_h = jnp.eye(MLA_HEADS, dtype=F32)
    uk_bd = jnp.einsum("chd,hg->hdgc", w_uk, eye_h)
    uk_bd = jnp.pad(uk_bd, ((0, 0), (0, 0), (0, 0), (0, LANE - MLA_KV_RANK)))
    uk_bd = uk_bd.reshape(MLA_HEADS * MLA_NOPE, MLA_HEADS * LANE)
    rr = jnp.arange(MLA_HEADS * MLA_ROPE)
    pp = (jnp.arange(MLA_HEADS * LANE)[None, :]
          == ((rr // MLA_ROPE) * LANE + MLA_KV_RANK + rr % MLA_ROPE)[:, None]).astype(F32)
    w_uv = p["mla_w_uv"][l]
    uv_bd = jnp.einsum("chv,hg->hcgv", w_uv, eye_h)
    uv_bd = jnp.pad(uv_bd, ((0, 0), (0, LANE - MLA_KV_RANK), (0, 0), (0, 0)))
    uv_bd = uv_bd.reshape(MLA_HEADS * LANE, -1).astype(BF16)
    w_br = p["w_br"][l]
    mla_w = w_uv.shape[1] * w_uv.shape[2]
    br_n, br_f = w_br[:nsa_w], w_br[nsa_w + mla_w:]
    g_kv = p["mla_g_kv"][l]
    b_f = p["fox_b_f"][l]
    return {
        "g_attn": p["g_attn"][l][None, :], "g_ffn": p["g_ffn"][l][None, :],
        "w_row": w_row.astype(BF16), "w_t": w_t.T.astype(BF16), "g_q": p["mla_g_q"][l][:, None],
        "w_uq": w_uq2.T.astype(BF16), "w_uk": uk_bd.T.astype(BF16), "pp": pp.T.astype(BF16),
        "g_kv": g_kv[:, None], "g_kv_row": _pad_cols(g_kv[None, :]),
        "b_f": b_f[:, None], "b_f_row": _pad_cols(b_f[None, :]),
        "q_aug": jnp.asarray(aug.reshape(FOX_HEADS * LANE, 1)),
        "w_mg": w_mg.astype(BF16),
        "w_br_n": {o: _pad_head_rows(br_n, NSA_HEADS, NSA_DH, o).astype(BF16) for o in (0, NSA_DH)},
        "w_uv": uv_bd,
        "w_br_m": w_br[nsa_w:nsa_w + mla_w].astype(BF16),
        "w_br_f": {o: _pad_head_rows(br_f, FOX_HEADS, FOX_DH, o).astype(BF16) for o in (0, FOX_DH)},
        "w_o": p["w_o"][l].astype(BF16),
        "w_ffn_in": p["w_ffn_in"][l].astype(BF16), "conv_w": p["conv_w"][l],
        "conv_b": p["conv_b"][l][None, :], "w_ffn_out": p["w_ffn_out"][l].astype(BF16),
    }


def _rope_tables(pos):
    half = MLA_ROPE // 2
    inv = ROPE_BASE ** (-jnp.arange(half, dtype=F32) / half)
    ang = pos.astype(F32)[:, None] * inv[None, :]
    cos, sin = jnp.cos(ang), jnp.sin(ang)
    c1 = jnp.concatenate([cos, cos], axis=1)
    c2 = jnp.concatenate([-sin, sin], axis=1)
    padr = ((0, 0), (MLA_KV_RANK, LANE - MLA_KV_RANK - MLA_ROPE))
    return (c1.T, c2.T, jnp.tile(c1, (1, MLA_HEADS)).T, jnp.tile(c2, (1, MLA_HEADS)).T,
            jnp.pad(c1, padr), jnp.pad(c2, padr))


def _page_copy(cache_ref, layer, page, buf, slot, p, rows, sem):
    dst = buf.at[slot, pl.ds(0, rows), pl.ds(pl.multiple_of(p * LANE, LANE), LANE)]
    return pltpu.make_async_copy(cache_ref.at[layer, page], dst, sem)


def _wait_all(buf, sem):
    pltpu.make_async_copy(buf, buf, sem).wait()


def _merge_partials(parts):
    m = parts[0][0]
    for mc, _, _ in parts[1:]:
        m = jnp.maximum(m, mc)
    l = acc = None
    for mc, lc, ac in parts:
        w = jnp.exp2(mc - m)
        l = w * lc if l is None else l + w * lc
        acc = w * ac if acc is None else acc + w * ac
    return acc / l


def _softmax_parts(parts):
    m = parts[0].max(axis=-1, keepdims=True)
    for s in parts[1:]:
        m = jnp.maximum(m, s.max(axis=-1, keepdims=True))
    ps = [jnp.exp2(s - m) for s in parts]
    l = ps[0].sum(axis=-1, keepdims=True)
    for p in ps[1:]:
        l = l + p.sum(axis=-1, keepdims=True)
    return ps, l


def _paged_body(pt_ref, q_ref, knew_ref, *rest, layer, n_seq, n_pages, rows, heads, s_len, tk,
                with_decay):
    if with_decay:
        (lfn_ref, tri_s_ref, tri_ref, cache_ref, lcache_ref, o_ref, kbuf, lbuf, sem) = rest
    else:
        (cache_ref, o_ref, kbuf, sem) = rest
    b = pl.program_id(0)
    slot = b % 2
    past = n_pages * LANE
    sp = knew_ref.shape[1]

    def issue(seq, sl):
        def body(p, c):
            pg = pt_ref[seq, p]
            _page_copy(cache_ref, layer, pg, kbuf, sl, p, rows, sem.at[sl, 0]).start()
            if with_decay:
                pltpu.make_async_copy(lcache_ref.at[layer, pg], lbuf.at[sl, p], sem.at[sl, 1]).start()
            return c
        lax.fori_loop(0, n_pages, body, 0, unroll=4)

    def wait(sl):
        _wait_all(kbuf.at[sl, pl.ds(0, rows), :], sem.at[sl, 0])
        if with_decay:
            _wait_all(lbuf.at[sl], sem.at[sl, 1])

    @pl.when(b == 0)
    def _():
        if rows < LANE:
            kbuf[:, rows:, :] = jnp.zeros((2, LANE - rows, past), F32)
        issue(0, 0)

    @pl.when(b + 1 < n_seq)
    def _():
        issue(b + 1, 1 - slot)

    wait(slot)

    q = q_ref[0]
    m_rows = q.shape[0]
    knew = knew_ref[0].astype(BF16)
    s_new = _dot_nt(q, knew)
    if with_decay:
        cin = _dot_terms(lbuf[slot].reshape(n_pages * 8, LANE), tri_ref[...], 3)
        run = jnp.zeros((8, 1), F32)
        negc = []
        for p in range(n_pages):
            cp = cin[p * 8:(p + 1) * 8]
            negc.append(-(cp + run) * LOG2E)
            run = run + cp[:, LANE - 1:]
        cs_new = _dot_f32(lfn_ref[0], tri_s_ref[...])
        s_new = s_new + jnp.concatenate([-(run + cs_new) * LOG2E] * s_len, axis=0)
    sq = jnp.right_shift(lax.broadcasted_iota(jnp.int32, (m_rows, sp), 0), _log2(heads))
    jj = lax.broadcasted_iota(jnp.int32, (m_rows, sp), 1)
    s_new = jnp.where(jj <= sq, s_new, NEG)
    m = jnp.max(s_new, axis=-1, keepdims=True)
    p = jnp.exp2(s_new - m)
    parts = [(m, jnp.sum(p, axis=-1, keepdims=True), _dot(p.astype(BF16), knew))]
    ppc = tk // LANE
    for c in range(past // tk):
        kt = kbuf[slot, :, c * tk:(c + 1) * tk].astype(BF16)
        s = _dot(q, kt)
        if with_decay:
            nc = jnp.concatenate(negc[c * ppc:(c + 1) * ppc], axis=1)
            s = s + jnp.concatenate([nc] * s_len, axis=0)
        m = jnp.max(s, axis=-1, keepdims=True)
        p = jnp.exp2(s - m)
        parts.append((m, jnp.sum(p, axis=-1, keepdims=True), _dot_nt(p.astype(BF16), kt)))
    o_ref[0] = _merge_partials(parts).astype(BF16)


def _paged_attn(page_table, q, knew, cache_t, layer, *, heads, s_len, decay=None):
    n_seq, n_pages = page_table.shape
    rows = cache_t.shape[2]
    past = n_pages * LANE
    tk = past // 2
    m = q.shape[1]
    sp = knew.shape[1]
    with_decay = decay is not None
    per_b = lambda b, pt: (b, 0, 0)
    in_specs = [pl.BlockSpec((1, m, LANE), per_b), pl.BlockSpec((1, sp, LANE), per_b)]
    args = [q, knew]
    scratch = [pltpu.VMEM((2, LANE, past), F32)]
    if with_decay:
        lfn, lcache_t = decay
        tri_s = jnp.triu(jnp.ones((sp, sp), F32))
        tri = jnp.triu(jnp.ones((LANE, LANE), F32)).astype(BF16)
        in_specs += [pl.BlockSpec((1, 8, sp), per_b), pl.BlockSpec((sp, sp), lambda b, pt: (0, 0)),
                     pl.BlockSpec((LANE, LANE), lambda b, pt: (0, 0)),
                     pl.BlockSpec(memory_space=pl.ANY), pl.BlockSpec(memory_space=pl.ANY)]
        args += [lfn, tri_s, tri, cache_t, lcache_t]
        scratch.append(pltpu.VMEM((2, n_pages, 8, LANE), F32))
    else:
        in_specs.append(pl.BlockSpec(memory_space=pl.ANY))
        args.append(cache_t)
    scratch.append(pltpu.SemaphoreType.DMA((2, 2)))
    return pl.pallas_call(
        functools.partial(_paged_body, layer=layer, n_seq=n_seq, n_pages=n_pages, rows=rows,
                          heads=heads, s_len=s_len, tk=tk, with_decay=with_decay),
        grid_spec=pltpu.PrefetchScalarGridSpec(
            num_scalar_prefetch=1, grid=(n_seq,), in_specs=in_specs,
            out_specs=pl.BlockSpec((1, m, LANE), per_b), scratch_shapes=scratch),
        out_shape=jax.ShapeDtypeStruct((n_seq, m, LANE), BF16),
        compiler_params=_cparams(1),
        name="paged_attn_decay" if with_decay else "paged_attn",
    )(page_table, *args)


def _nsa_s1_body(pt_ref, q_ref, wnew_ref, win_ref, cb_ref, wb_ref, wnb_ref, amat_ref, pair_ref,
                 cache_ref, ocmp_ref, owin_ref, idx_ref, kbuf, sem, *, layer, n_seq, n_pages, s_len,
                 k_free):
    b = pl.program_id(0)
    slot = b % 2
    past = n_pages * LANE
    hh = NSA_HEADS

    def issue(seq, sl):
        def body(p, c):
            _page_copy(cache_ref, layer, pt_ref[seq, p], kbuf, sl, p, LANE, sem.at[sl]).start()
            return c
        lax.fori_loop(0, n_pages, body, 0, unroll=4)

    @pl.when(b == 0)
    def _():
        issue(0, 0)

    @pl.when(b + 1 < n_seq)
    def _():
        issue(b + 1, 1 - slot)

    _wait_all(kbuf.at[slot], sem.at[slot])
    q = q_ref[0]
    tc = amat_ref.shape[0]
    amat = amat_ref[...]
    means = []
    for c in range(past // tc):
        means.append(_dot_terms(kbuf[slot, :, c * tc:(c + 1) * tc], amat, 2))
    kvc = jnp.concatenate(means, axis=1).astype(BF16)
    s = _dot(q, kvc) + cb_ref[...]
    e = jnp.exp2(s - jnp.max(s, axis=-1, keepdims=True))
    p = e / jnp.sum(e, axis=-1, keepdims=True)
    ocmp_ref[0] = _dot_nt(p.astype(BF16), kvc)
    ps = _dot_f32(p, pair_ref[...])
    score = ps[0:s_len]
    for h in range(1, hh):
        score = score + ps[h * s_len:(h + 1) * s_len]
    n_past = score.shape[1]
    lane = lax.broadcasted_iota(jnp.int32, score.shape, 1).astype(F32)
    work = jnp.where((lane == 0.0) | (lane == n_past - 1.0), -jnp.inf, score)
    out_lane = lax.broadcasted_iota(jnp.int32, (s_len, LANE), 1)
    idx_out = jnp.zeros((s_len, LANE), F32)
    for r in range(k_free):
        mx = jnp.max(work, axis=-1, keepdims=True)
        idx = jnp.min(jnp.where(work == mx, lane, float(n_past)), axis=-1, keepdims=True)
        idx_out = jnp.where(out_lane == r, idx, idx_out)
        work = jnp.where(lane == idx, -jnp.inf, work)
    idx_ref[0] = idx_out.astype(jnp.int32)
    wst = win_ref[0, 0].astype(BF16)
    wnew = wnew_ref[0].astype(BF16)
    (p_w, p_n), l = _softmax_parts([_dot(q, wst) + wb_ref[...], _dot_nt(q, wnew) + wnb_ref[...]])
    owin_ref[0] = (_dot_nt(p_w.astype(BF16), wst) + _dot(p_n.astype(BF16), wnew)) / l


def _nsa_s1(page_table, q, wnew, win_t, cache_t, layer, tables, *, s_len, k_free):
    n_seq, n_pages = page_table.shape
    past = n_pages * LANE
    cb, wb, wnb = tables
    m = q.shape[1]
    sp = wnew.shape[1]
    wlen = win_t.shape[3]
    tc = min(past, CMP_BLOCK * LANE)
    amat = (jnp.arange(tc)[:, None] // CMP_BLOCK == jnp.arange(tc // CMP_BLOCK)[None, :]).astype(F32)
    amat = (amat / CMP_BLOCK).astype(BF16)
    nc = past // CMP_BLOCK
    ratio = SEL_BLOCK // CMP_BLOCK
    pair = (jnp.arange(nc)[:, None] // ratio == jnp.arange(nc // ratio)[None, :]).astype(F32)
    per_b = lambda b, pt: (b, 0, 0)
    c2 = lambda b, pt: (0, 0)
    return pl.pallas_call(
        functools.partial(_nsa_s1_body, layer=layer, n_seq=n_seq, n_pages=n_pages, s_len=s_len,
                          k_free=k_free),
        grid_spec=pltpu.PrefetchScalarGridSpec(
            num_scalar_prefetch=1, grid=(n_seq,),
            in_specs=[pl.BlockSpec((1, m, LANE), per_b), pl.BlockSpec((1, sp, LANE), per_b),
                      pl.BlockSpec((1, 1, LANE, wlen), lambda b, pt: (layer, b, 0, 0)),
                      pl.BlockSpec(cb.shape, c2), pl.BlockSpec(wb.shape, c2),
                      pl.BlockSpec(wnb.shape, c2), pl.BlockSpec(amat.shape, c2),
                      pl.BlockSpec(pair.shape, c2), pl.BlockSpec(memory_space=pl.ANY)],
            out_specs=[pl.BlockSpec((1, m, LANE), per_b), pl.BlockSpec((1, m, LANE), per_b),
                       pl.BlockSpec((1, s_len, LANE), per_b)],
            scratch_shapes=[pltpu.VMEM((2, LANE, past), F32), pltpu.SemaphoreType.DMA((2,))]),
        out_shape=[jax.ShapeDtypeStruct((n_seq, m, LANE), F32),
                   jax.ShapeDtypeStruct((n_seq, m, LANE), F32),
                   jax.ShapeDtypeStruct((n_seq, s_len, LANE), jnp.int32)],
        compiler_params=_cparams(1),
        name="nsa_sample_cmp",
    )(page_table, q, wnew, win_t, cb, wb, wnb, amat, pair, cache_t)


def _nsa_s2_body(pt_ref, idx_ref, q_ref, snew_ref, g_ref, ocmp_ref, owin_ref, lb_ref, nb_ref,
                 cache_ref, o_ref, kbuf, sem, *, layer, n_seq, n_pages, s_len, k_free):
    b = pl.program_id(0)
    slot = b % 2
    n_own = s_len * k_free
    n_slots = n_own + 2
    per_page = LANE // SEL_BLOCK

    def block_of(seq, j):
        if j < n_own:
            return idx_ref[(seq * s_len + j // k_free) * LANE + j % k_free]
        return 0 if j == n_own else n_pages * per_page - 1

    def issue(seq, sl):
        for j in range(n_slots):
            blk = block_of(seq, j)
            pg = pt_ref[seq, blk >> _log2(per_page)]
            _page_copy(cache_ref, layer, pg, kbuf, sl, j, LANE, sem.at[sl]).start()

    @pl.when(b == 0)
    def _():
        issue(0, 0)

    @pl.when(b + 1 < n_seq)
    def _():
        issue(b + 1, 1 - slot)

    _wait_all(kbuf.at[slot], sem.at[slot])
    q = q_ref[0]
    m_rows = q.shape[0]
    kt = kbuf[slot].astype(BF16)
    s = _dot(q, kt)
    assert s_len & (s_len - 1) == 0
    row_s = jnp.bitwise_and(lax.broadcasted_iota(jnp.int32, (m_rows, LANE), 0), s_len - 1)
    lane_half = jnp.right_shift(lax.broadcasted_iota(jnp.int32, (m_rows, LANE), 1), _log2(SEL_BLOCK))
    bias = []
    last_page = (n_pages - 1) * per_page
    for j in range(n_slots):
        blk = block_of(b, j)
        ok = lane_half == (blk & (per_page - 1))
        if j < n_own:
            ok = ok & (row_s == j // k_free)
            near = jnp.where(blk >= last_page, lb_ref[...], 0.0)
        else:
            near = lb_ref[...] if j == n_slots - 1 else 0.0
        bias.append(jnp.where(ok, near, NEG))
    s = s + jnp.concatenate(bias, axis=1)
    snew = snew_ref[0].astype(BF16)
    (p_s, p_n), l = _softmax_parts([s, _dot_nt(q, snew) + nb_ref[...]])
    o_sel = (_dot_nt(p_s.astype(BF16), kt) + _dot(p_n.astype(BF16), snew)) / l
    g = g_ref[0]
    o = g[:, 0:1] * ocmp_ref[0] + g[:, 1:2] * o_sel + g[:, 2:3] * owin_ref[0]
    o_ref[0] = o.astype(BF16)


def _nsa_s2(page_table, idx, q, snew, gates, o_cmp, o_win, cache_t, layer, tables, *, s_len, k_free):
    n_seq, n_pages = page_table.shape
    lb, nb = tables
    m = q.shape[1]
    sp = snew.shape[1]
    n_slots = s_len * k_free + 2
    per_b = lambda b, pt, ix: (b, 0, 0)
    c2 = lambda b, pt, ix: (0, 0)
    return pl.pallas_call(
        functools.partial(_nsa_s2_body, layer=layer, n_seq=n_seq, n_pages=n_pages, s_len=s_len,
                          k_free=k_free),
        grid_spec=pltpu.PrefetchScalarGridSpec(
            num_scalar_prefetch=2, grid=(n_seq,),
            in_specs=[pl.BlockSpec((1, m, LANE), per_b), pl.BlockSpec((1, sp, LANE), per_b),
                      pl.BlockSpec((1, m, LANE), per_b), pl.BlockSpec((1, m, LANE), per_b),
                      pl.BlockSpec((1, m, LANE), per_b), pl.BlockSpec(lb.shape, c2),
                      pl.BlockSpec(nb.shape, c2), pl.BlockSpec(memory_space=pl.ANY)],
            out_specs=pl.BlockSpec((1, m, LANE), per_b),
            scratch_shapes=[pltpu.VMEM((2, LANE, n_slots * LANE), F32),
                            pltpu.SemaphoreType.DMA((2,))]),
        out_shape=jax.ShapeDtypeStruct((n_seq, m, LANE), BF16),
        compiler_params=_cparams(1),
        name="nsa_sample_sel",
    )(page_table, idx.reshape(-1), q, snew, gates, o_cmp, o_win, lb, nb, cache_t)


def _nsa_sample_tables(rel_table, past, s_len, sp, wlen):
    hh = NSA_HEADS
    far = rel_table[REL_BUCKETS - 1].astype(F32)[:, None, None]
    qpos = past + jnp.arange(s_len)
    rows = lambda a: a.reshape(hh * s_len, a.shape[-1])
    c_end = jnp.arange(past // CMP_BLOCK) * CMP_BLOCK + CMP_BLOCK - 1
    cb = rows(_rel_bias_t(rel_table, qpos[:, None] - c_end[None, :]) * LOG2E)
    dist = qpos[:, None] - (past - wlen + jnp.arange(wlen))[None, :]
    wb = rows(jnp.where(dist < WINDOW, _rel_bias_t(rel_table, dist) * LOG2E, NEG))
    dist_new = jnp.arange(s_len)[:, None] - jnp.arange(sp)[None, :]
    newb = _rel_bias_t(rel_table, dist_new)
    wnb = rows(jnp.where(dist_new >= 0, newb * LOG2E, NEG))
    assert LANE >= REL_MAX_DIST
    dist = qpos[:, None] - (past - LANE + jnp.arange(LANE))[None, :]
    lb = rows((_rel_bias_t(rel_table, dist) - far) * LOG2E)
    nb = rows(jnp.where(dist_new >= 0, (newb - far) * LOG2E, NEG))
    return (cb, wb, wnb), (lb, nb)


TQ = 256
TQ_FOX = 128
TK = 256
GROUP = 4
TM_PROMPT = 256
NEW_PAD = 16
KEY_PAD = WINDOW


def _fm_to_rows(a, mid):
    lead = a.shape[:-2]
    n = a.shape[-1]
    a = a.reshape(lead + mid + (n,))
    return jnp.moveaxis(a, -1, len(lead))


def kernel(x_prompt, x_sample, cache_nsa_cmp, cache_nsa_sel, state_nsa_win, cache_mla, cache_fox_kv, cache_fox_logf, state_ffn_conv, page_table, c_prompt, c_sample, rel_table, w_ada, b_ada, g_attn, g_ffn, w_in, mla_g_q, mla_w_uq, mla_g_kv, mla_w_uk, mla_w_uv, fox_b_f, w_br, w_o, w_ffn_in, conv_w, conv_b, w_ffn_out, g_final):
    b, t, d = x_prompt.shape
    db, s_len, _ = x_sample.shape
    depth = w_in.shape[0]
    pool = cache_nsa_cmp.shape[1]
    n_pages = page_table.shape[1]
    past = n_pages * LANE
    d_ff = w_ffn_out.shape[1]
    wlen = state_nsa_win.shape[2]
    assert cache_nsa_cmp.shape[2] == LANE and TQ >= REL_MAX_DIST and t % TK == 0
    assert t // SEL_BLOCK <= LANE and TM_PROMPT == 8 * CMP_BLOCK
    params = dict(w_in=w_in, mla_g_q=mla_g_q, mla_w_uq=mla_w_uq, mla_g_kv=mla_g_kv, mla_w_uk=mla_w_uk,
                  mla_w_uv=mla_w_uv, fox_b_f=fox_b_f, w_br=w_br, w_o=w_o, w_ffn_in=w_ffn_in,
                  conv_w=conv_w, conv_b=conv_b, w_ffn_out=w_ffn_out, g_attn=g_attn, g_ffn=g_ffn)

    n_c = b + db
    c_all = jnp.pad(jnp.concatenate([c_prompt, c_sample], axis=0), ((0, -n_c % 8), (0, 0)))
    mods = _adaln(c_all, w_ada, b_ada)

    cmp_t = jnp.transpose(cache_nsa_cmp, (0, 1, 3, 4, 2)).reshape(depth, pool, LANE, LANE)
    sel_t = jnp.transpose(cache_nsa_sel, (0, 1, 3, 4, 2)).reshape(depth, pool, LANE, LANE)
    fkv_t = jnp.transpose(cache_fox_kv, (0, 1, 3, 4, 5, 2)).reshape(depth, pool, LANE, LANE)
    mla_t = jnp.transpose(cache_mla, (0, 1, 3, 2))
    lf_t = jnp.transpose(cache_fox_logf, (0, 1, 3, 2))
    win_t = jnp.transpose(state_nsa_win, (0, 1, 3, 4, 2)).reshape(depth, db, LANE, wlen)
    conv_pre_s = jnp.transpose(state_ffn_conv, (0, 2, 1, 3))
    conv_pre_p = jnp.zeros((b, CONV_W - 1, d_ff), F32)

    rope_p = _rope_tables(jnp.arange(t))
    rope_s = _rope_tables(jnp.repeat(past + jnp.arange(s_len), db))
    tabs_p = _nsa_prompt_tables(rel_table, t, TQ)
    tabs_s1, tabs_s2 = _nsa_sample_tables(rel_table, past, s_len, NEW_PAD, wlen)
    n_sel_s = -(-(past + s_len) // SEL_BLOCK)
    k_free = min(N_SEL, n_sel_s) - 3

    tpb = t // TM_PROMPT
    idx_p = lambda i: (i // tpb, 0, 0)
    idx_s = lambda i: (0, 0, 0)
    xp = x_prompt.reshape(b * t, d)
    xs = jnp.transpose(x_sample, (1, 0, 2)).reshape(s_len * db, d)

    def new_rows(a):
        return jnp.pad(jnp.transpose(a, (2, 0, 1)), ((0, 0), (0, NEW_PAD - s_len), (0, 0)))

    def q_rows(qt, heads, width, head_major):
        q = jnp.transpose(qt.reshape(s_len, heads, LANE, db)[:, :, :width],
                          (3, 1, 0, 2) if head_major else (3, 0, 1, 2))
        return jnp.pad(q.reshape(db, s_len * heads, width), ((0, 0), (0, 0), (0, LANE - width)))

    def o_rows(o, heads, head_major):
        o = o.reshape((db, heads, s_len, LANE) if head_major else (db, s_len, heads, LANE))
        o = jnp.transpose(o, (2, 0, 1, 3) if head_major else (1, 0, 2, 3))
        return o.reshape(s_len * db, heads * LANE)

    def front_pad(a, axis):
        pads = [(0, 0)] * a.ndim
        pads[axis] = (KEY_PAD, 0)
        return jnp.pad(a, pads)

    rows_p, rows_s = [], []
    for l in range(depth):
        lw = _prep_layer(l, params)
        m6 = [mods[l][:, k * d:(k + 1) * d] for k in range(6)]
        sh_a, sc_a, gt_a, sh_f, sc_f, gt_f = [a[:b][:, None, :] for a in m6]
        sh_as, sc_as, gt_as, sh_fs, sc_fs, gt_fs = [a[b:n_c][None] for a in m6]
        last = l == depth - 1

        pre = _pre_proj(xp, sc_a, sh_a, idx_p, lw, rope_p, n_groups=b, tiles_per_group=tpb,
                        n_pos_tiles=tpb, tm=TM_PROMPT, with_keys=True)
        nc = t // CMP_BLOCK
        kvc = pre["kvc"].reshape(b, nc // 2, 2, LANE)
        kvc = jnp.transpose(kvc, (0, 2, 1, 3)).reshape(b, nc, LANE)
        kvc_k = jnp.pad(kvc[:, :, :NSA_DH], ((0, 0), (0, 0), (0, LANE - NSA_DH))).astype(BF16)
        kvct = jnp.transpose(kvc, (0, 2, 1)).astype(BF16)
        rs = lambda a: a.reshape(b, t, a.shape[-1])
        o_nsa = _nsa_prompt(pre["qnt"], kvc_k, kvct, front_pad(rs(pre["ks"]), 1),
                            front_pad(pre["vst"], 2), front_pad(rs(pre["kw"]), 1),
                            front_pad(pre["vwt"], 2), pre["gt"], tabs_p, tq=TQ, tk=TK, pad=KEY_PAD)
        o_lat = _causal_attn(pre["qmt"], rs(pre["km"]), pre["vmt"], heads=MLA_HEADS, tq=TQ, tk=TK,
                             group=GROUP)
        o_fox = _causal_attn(pre["qft"], rs(pre["kf"]), pre["vft"], heads=FOX_HEADS, tq=TQ_FOX, tk=TK,
                             group=GROUP)
        x1 = _mix(xp, sc_a, sh_a, gt_a, idx_p, lw, o_nsa, o_lat, o_fox, tm=TM_PROMPT, value_lane=0)
        xp, tail_p = _ffn(x1, sc_f, sh_f, gt_f, idx_p, lw, conv_pre_p, g_final[None, :], tm=TM_PROMPT,
                          tiles_per_seq=tpb, time_major=False, final_norm=last)
        wkeep = min(WINDOW, t)
        rows_p.append((_fm_to_rows(pre["ct"], (2, NSA_DH)), _fm_to_rows(pre["st"], (2, NSA_DH)),
                       _fm_to_rows(pre["wt"][:, :, t - wkeep:], (2, NSA_DH)),
                       _fm_to_rows(pre["mt"][:, :MLA_KV_RANK + MLA_ROPE], (MLA_KV_RANK + MLA_ROPE,)),
                       _fm_to_rows(pre["ft"], (2, 1, FOX_DH)), _fm_to_rows(pre["lft"], (FOX_HEADS,)),
                       tail_p))

        pre_s = _pre_proj(xs, sc_as, sh_as, idx_s, lw, rope_s, n_groups=s_len, tiles_per_group=1,
                          n_pos_tiles=s_len, tm=db, with_keys=False)
        qn_s = q_rows(pre_s["qnt"], NSA_HEADS, NSA_DH, True)
        gates_s = jnp.transpose(pre_s["gt"][:, :3 * NSA_HEADS].reshape(s_len, NSA_HEADS, 3, db),
                                (3, 1, 0, 2)).reshape(db, NSA_HEADS * s_len, 3)
        gates_s = jnp.pad(gates_s, ((0, 0), (0, 0), (0, LANE - 3)))
        o_cmp, o_win, idx = _nsa_s1(page_table, qn_s, new_rows(pre_s["wt"]), win_t, cmp_t, l, tabs_s1,
                                    s_len=s_len, k_free=k_free)
        o_nsa_s = _nsa_s2(page_table, idx, qn_s, new_rows(pre_s["st"]), gates_s, o_cmp, o_win, sel_t, l,
                          tabs_s2, s_len=s_len, k_free=k_free)
        o_lat_s = _paged_attn(page_table, q_rows(pre_s["qmt"], MLA_HEADS, LANE, False),
                              new_rows(pre_s["mt"]), mla_t, l, heads=MLA_HEADS, s_len=s_len)
        lfn = jnp.pad(jnp.transpose(pre_s["lft"], (2, 1, 0)), ((0, 0), (0, 0), (0, NEW_PAD - s_len)))
        o_fox_s = _paged_attn(page_table, q_rows(pre_s["qft"], FOX_HEADS, FOX_DH, False),
                              new_rows(pre_s["ft"]), fkv_t, l, heads=FOX_HEADS, s_len=s_len,
                              decay=(lfn, lf_t))
        x1s = _mix(xs, sc_as, sh_as, gt_as, idx_s, lw, o_rows(o_nsa_s, NSA_HEADS, True),
                   o_rows(o_lat_s, MLA_HEADS, False), o_rows(o_fox_s, FOX_HEADS, False), tm=db,
                   value_lane=NSA_DH)
        xs, tail_s = _ffn(x1s, sc_fs, sh_fs, gt_fs, idx_s, lw, conv_pre_s[l], g_final[None, :], tm=db,
                          tiles_per_seq=s_len, time_major=True, final_norm=last)
        wfull = jnp.concatenate([win_t[l], jnp.transpose(pre_s["wt"], (2, 1, 0))], axis=2)
        wfull = wfull[:, :, wfull.shape[2] - min(WINDOW, past + s_len):]
        fm_s = lambda a, mid: jnp.swapaxes(_fm_to_rows(a, mid), 0, 1)
        rows_s.append((fm_s(pre_s["ct"], (2, NSA_DH)), fm_s(pre_s["st"], (2, NSA_DH)),
                       _fm_to_rows(wfull, (2, NSA_DH)),
                       fm_s(pre_s["mt"][:, :MLA_KV_RANK + MLA_ROPE], (MLA_KV_RANK + MLA_ROPE,)),
                       fm_s(pre_s["ft"], (2, 1, FOX_DH)), fm_s(pre_s["lft"], (FOX_HEADS,)),
                       jnp.transpose(tail_s, (1, 0, 2))))

    y_prompt = xp.reshape(b, t, d)
    y_sample = jnp.transpose(xs.reshape(s_len, db, d), (1, 0, 2))
    outs_p = [jnp.stack(a) for a in zip(*rows_p)]
    outs_s = [jnp.stack(a) for a in zip(*rows_s)]
    return (y_prompt, y_sample, *outs_p, *outs_s)
```

```python
import functools
import math

import numpy as np
import jax
import jax.numpy as jnp
from jax import lax
from jax.experimental import pallas as pl
from jax.experimental.pallas import tpu as pltpu

F32 = jnp.float32
BF16 = jnp.bfloat16

NSA_HEADS = 4
NSA_DH = 64
CMP_BLOCK = 32
SEL_BLOCK = 64
N_SEL = 16
WINDOW = 512
MLA_HEADS = 4
MLA_NOPE = 64
MLA_ROPE = 32
MLA_KV_RANK = 64
ROPE_BASE = 10000.0
FOX_HEADS = 8
FOX_DH = 64
REL_BUCKETS = 32
REL_MAX_DIST = 128
CONV_W = 3
EPS = 1e-6
NEG = -1e30
LOG2E = math.log2(math.e)
LANE = 128
VMEM_LIMIT = 56 * 1024 * 1024

NSA_SCALE = NSA_DH ** -0.5
MLA_SCALE = (MLA_NOPE + MLA_ROPE) ** -0.5
FOX_SCALE = FOX_DH ** -0.5

ONES_ROW = 64
VROWS = 80
DECAY_PARTS = 3

_NT = (((1,), (1,)), ((), ()))


def _cparams(n_axes):
    return pltpu.CompilerParams(dimension_semantics=("arbitrary",) * n_axes,
                                vmem_limit_bytes=VMEM_LIMIT)


def _dot(a, b):
    return jnp.dot(a, b, preferred_element_type=F32)


def _dot_nt(a, b):
    return lax.dot_general(a, b, _NT, preferred_element_type=F32)


def _dot_f32(a, b):
    return jnp.dot(a, b, preferred_element_type=F32, precision=lax.Precision.HIGHEST)


def _dot_terms(x, w, terms):
    out = None
    rest = x
    for _ in range(terms):
        piece = rest.astype(BF16)
        rest = rest - piece.astype(F32)
        d = _dot(piece, w)
        out = d if out is None else out + d
    return out


def _rms(x, g):
    return x * lax.rsqrt(jnp.mean(x * x, axis=-1, keepdims=True) + EPS) * g


def _log2(n):
    assert n & (n - 1) == 0, n
    return n.bit_length() - 1


def _ada_body(c_ref, w_ref, b_ref, o_ref):
    c = c_ref[...]
    sc = (c * jax.nn.sigmoid(c)).astype(BF16)
    o_ref[0] = _dot(sc, w_ref[0].astype(BF16)) + b_ref[0]


def _adaln(c_all, w_ada, b_ada):
    depth, d, n6 = w_ada.shape
    rows = c_all.shape[0]
    tn = 512
    return pl.pallas_call(
        _ada_body,
        grid=(depth, n6 // tn),
        in_specs=[pl.BlockSpec((rows, d), lambda l, n: (0, 0)),
                  pl.BlockSpec((1, d, tn), lambda l, n: (l, 0, n)),
                  pl.BlockSpec((1, 1, tn), lambda l, n: (l, 0, n))],
        out_specs=pl.BlockSpec((1, rows, tn), lambda l, n: (l, 0, n)),
        out_shape=jax.ShapeDtypeStruct((depth, rows, n6), F32),
        compiler_params=_cparams(2),
        name="adaln",
    )(c_all, w_ada, b_ada.reshape(depth, 1, n6))


_T_C, _T_S, _T_W, _T_F, _T_M, _T_LF, _T_G, _T_QD = 0, 128, 256, 384, 512, 640, 648, 664
_T_QN = _T_QD + 256
_T_QF = _T_QN + NSA_HEADS * LANE
_T_END = _T_QF + FOX_HEADS * LANE
_R_KS, _R_KW, _R_KF, _R_KM, _R_KMS, _R_C, _R_LF, _R_END = 0, 128, 256, 384, 512, 640, 768, 896


def _value_tile(vt):
    n = vt.shape[1]
    return jnp.concatenate([vt, jnp.ones((VROWS - ONES_ROW, n), F32)], axis=0).astype(BF16)


def _pre_body(x_ref, sc_ref, sh_ref, g_ref, wrow_ref, wt_ref, gq_ref, wuq_ref, wuk_ref, pp_ref,
              gkv_ref, gkvr_ref, bf_ref, bfr_ref, aug_ref, cs1_ref, cs2_ref, csq1_ref, csq2_ref,
              cr1_ref, cr2_ref, tril_ref, avg_ref,
              ct_ref, st_ref, wtt_ref, ft_ref, mt_ref, lf_ref, gt_ref, qn_ref, qf_ref, qm_ref,
              vs_ref, vw_ref, vf_ref, vm_ref, ks_ref, kw_ref, kf_ref, km_ref, kvc_ref,
              carry_ref, *, tiles_per_seq, with_keys):
    x = x_ref[...]
    tm = x.shape[0]
    h = _rms(x, g_ref[...]) * (1.0 + sc_ref[0]) + sh_ref[0]
    hb = h.astype(BF16)

    pt = _dot_nt(wt_ref[...], hb)
    ct_ref[0] = pt[_T_C:_T_C + 128]
    st_ref[0] = pt[_T_S:_T_S + 128]
    wtt_ref[0] = pt[_T_W:_T_W + 128]
    ft_ref[0] = pt[_T_F:_T_F + 128]
    ckv = pt[_T_M:_T_M + 64]
    ckv = ckv * lax.rsqrt(jnp.mean(ckv * ckv, axis=0, keepdims=True) + EPS) * gkv_ref[...]
    krot = pt[_T_M + 64:_T_M + 96] * cs1_ref[...] + pt[_T_M + 96:_T_M + 128] * cs2_ref[...]
    mt_ref[0, 0:64] = ckv
    mt_ref[0, 64:96] = krot
    mt_ref[0, 96:128] = jnp.zeros_like(krot)
    lf_ref[0] = jax.nn.log_sigmoid(pt[_T_LF:_T_LF + 8] + bf_ref[...])
    gt_ref[0] = jax.nn.sigmoid(pt[_T_G:_T_G + 16])
    qn_ref[0] = pt[_T_QN:_T_QF].astype(BF16)
    qf_ref[0] = (pt[_T_QF:_T_END] + aug_ref[...]).astype(BF16)
    qd = pt[_T_QD:_T_QD + 256]
    qd = qd * lax.rsqrt(jnp.mean(qd * qd, axis=0, keepdims=True) + EPS) * gq_ref[...]
    q = _dot(wuq_ref[...], qd.astype(BF16))
    qrot = q[256:384] * csq1_ref[...] + q[384:512] * csq2_ref[...]
    qm = _dot(wuk_ref[...], q[:256].astype(BF16)) * (MLA_SCALE * LOG2E)
    qm = qm + _dot(pp_ref[...], (qrot * (MLA_SCALE * LOG2E)).astype(BF16))
    qm_ref[0] = qm.astype(BF16)
    vs_ref[0] = _value_tile(pt[_T_S + 64:_T_S + 128])
    vw_ref[0] = _value_tile(pt[_T_W + 64:_T_W + 128])
    vf_ref[0] = _value_tile(pt[_T_F + 64:_T_F + 128])
    vm_ref[0] = _value_tile(ckv)

    if with_keys:
        pr = _dot(hb, wrow_ref[...])
        lane = lax.broadcasted_iota(jnp.int32, (tm, LANE), 1)
        t_in_seq = (pl.program_id(0) % tiles_per_seq) * tm
        pos = t_in_seq + lax.broadcasted_iota(jnp.int32, (tm, LANE), 0)
        blk_ind = jnp.where(lane == jnp.right_shift(pos, _log2(SEL_BLOCK)), 1.0, 0.0)
        ks_ref[:, 0:LANE] = pr[:, _R_KS:_R_KS + LANE].astype(BF16)
        ks_ref[:, LANE:2 * LANE] = blk_ind.astype(BF16)
        kw_ref[...] = pr[:, _R_KW:_R_KW + LANE].astype(BF16)
        kvd = pr[:, _R_KM:_R_KM + LANE]
        is_c = lane < MLA_KV_RANK
        ms = jnp.sum(jnp.where(is_c, kvd * kvd, 0.0), axis=1, keepdims=True) / MLA_KV_RANK
        km = jnp.where(is_c, kvd * lax.rsqrt(ms + EPS) * gkvr_ref[...],
                       kvd * cr1_ref[...] + pr[:, _R_KMS:_R_KMS + LANE] * cr2_ref[...])
        km_ref[...] = km.astype(BF16)
        @pl.when(pl.program_id(0) % tiles_per_seq == 0)
        def _():
            carry_ref[...] = jnp.zeros_like(carry_ref)
        lfr = jnp.where(lane < FOX_HEADS, jax.nn.log_sigmoid(pr[:, _R_LF:_R_LF + LANE] + bfr_ref[...]),
                        0.0)
        csum = _dot_f32(tril_ref[...], lfr) + carry_ref[0:1]
        carry_ref[...] = jnp.broadcast_to(csum[tm - 1:], carry_ref.shape)
        kf = pr[:, _R_KF:_R_KF + LANE]
        rest = -csum * LOG2E
        for part in range(DECAY_PARTS):
            term = rest.astype(BF16).astype(F32)
            rest = rest - term
            kf = kf + pltpu.roll(term, FOX_DH + part * FOX_HEADS, 1)
        kf_ref[...] = kf.astype(BF16)
        kvc_ref[...] = _dot_f32(avg_ref[...], pr[:, _R_C:_R_C + LANE])
    else:
        ks_ref[...] = jnp.zeros(ks_ref.shape, BF16)
        kw_ref[...] = jnp.zeros(kw_ref.shape, BF16)
        kf_ref[...] = jnp.zeros(kf_ref.shape, BF16)
        km_ref[...] = jnp.zeros(km_ref.shape, BF16)
        kvc_ref[...] = jnp.zeros(kvc_ref.shape, F32)


def _pre_proj(x, sc, sh, mod_index, lw, pos_tabs, *, n_groups, tiles_per_group, n_pos_tiles, tm,
              with_keys):
    n, d = x.shape
    cs1, cs2, csq1, csq2, cr1, cr2 = pos_tabs
    n_tiles = n // tm
    tpg = tiles_per_group
    tril = np.tril(np.ones((tm, tm), np.float32))
    nmean = 8
    avg = (np.arange(tm)[None, :] // CMP_BLOCK == np.arange(nmean)[:, None]).astype(np.float32) / CMP_BLOCK
    row = lambda i: (i, 0)
    const2 = lambda i: (0, 0)
    tcol = lambda i: (i // tpg, 0, i % tpg)
    pcol = lambda i: (0, i % n_pos_tiles)
    prow = lambda i: (i % n_pos_tiles, 0)
    ncols = tpg * tm

    def tspec(r):
        return pl.BlockSpec((1, r, tm), tcol)

    def tshape(r, dt=F32):
        return jax.ShapeDtypeStruct((n_groups, r, ncols), dt)

    nq = NSA_HEADS * LANE
    nf = FOX_HEADS * LANE
    consts = [lw["g_attn"], lw["w_row"], lw["w_t"], lw["g_q"], lw["w_uq"], lw["w_uk"], lw["pp"],
              lw["g_kv"], lw["g_kv_row"], lw["b_f"], lw["b_f_row"], lw["q_aug"]]
    outs = pl.pallas_call(
        functools.partial(_pre_body, tiles_per_seq=tpg, with_keys=with_keys),
        grid=(n_tiles,),
        in_specs=[pl.BlockSpec((tm, d), row),
                  pl.BlockSpec((1,) + sc.shape[1:], mod_index),
                  pl.BlockSpec((1,) + sh.shape[1:], mod_index)]
        + [pl.BlockSpec(c.shape, const2) for c in consts]
        + [pl.BlockSpec((32, tm), pcol), pl.BlockSpec((32, tm), pcol),
           pl.BlockSpec((128, tm), pcol), pl.BlockSpec((128, tm), pcol),
           pl.BlockSpec((tm, 128), prow), pl.BlockSpec((tm, 128), prow),
           pl.BlockSpec((tm, tm), const2), pl.BlockSpec((nmean, tm), const2)],
        out_specs=[tspec(128), tspec(128), tspec(128), tspec(128), tspec(128), tspec(8), tspec(16),
                   tspec(nq), tspec(nf), tspec(nq), tspec(VROWS), tspec(VROWS), tspec(VROWS), tspec(VROWS),
                   pl.BlockSpec((tm, 2 * LANE), row), pl.BlockSpec((tm, LANE), row),
                   pl.BlockSpec((tm, LANE), row), pl.BlockSpec((tm, LANE), row),
                   pl.BlockSpec((nmean, LANE), row)],
        out_shape=[tshape(128), tshape(128), tshape(128), tshape(128), tshape(128), tshape(8),
                   tshape(16), tshape(nq, BF16), tshape(nf, BF16), tshape(nq, BF16),
                   tshape(VROWS, BF16), tshape(VROWS, BF16), tshape(VROWS, BF16), tshape(VROWS, BF16),
                   jax.ShapeDtypeStruct((n, 2 * LANE), BF16), jax.ShapeDtypeStruct((n, LANE), BF16),
                   jax.ShapeDtypeStruct((n, LANE), BF16), jax.ShapeDtypeStruct((n, LANE), BF16),
                   jax.ShapeDtypeStruct((n_tiles * nmean, LANE), F32)],
        scratch_shapes=[pltpu.VMEM((8, LANE), F32)],
        compiler_params=_cparams(1),
        name="pre_proj",
    )(x, sc, sh, *consts, cs1, cs2, csq1, csq2, cr1, cr2, tril, avg)
    keys = ("ct", "st", "wt", "ft", "mt", "lft", "gt", "qnt", "qft", "qmt", "vst", "vwt", "vft", "vmt",
            "ks", "kw", "kf", "km", "kvc")
    return dict(zip(keys, outs))


def _tflash_init(m_ref, acc_ref):
    m_ref[...] = jnp.full(m_ref.shape, NEG, F32)
    acc_ref[...] = jnp.zeros(acc_ref.shape, F32)


def _tflash_update(s, vt, m_ref, acc_ref):
    _tflash_update_many([(s, vt)], m_ref, acc_ref)


def _tflash_update_many(tiles, m_ref, acc_ref):
    m_old = m_ref[...]
    parts = []
    m_new = m_old
    for s, vt in tiles:
        mk = jnp.max(s, axis=0, keepdims=True)
        parts.append((mk, _dot(vt, jnp.exp2(s - mk).astype(BF16))))
        m_new = jnp.maximum(m_new, mk)
    acc = jnp.exp2(m_old - m_new) * acc_ref[...]
    for mk, pv in parts:
        acc = acc + jnp.exp2(mk - m_new) * pv
    acc_ref[...] = acc
    m_ref[...] = m_new


def _tflash_out(acc, heads, tq):
    o = acc[:ONES_ROW] * (1.0 / acc[ONES_ROW:ONES_ROW + 1])
    o = jnp.concatenate([o, jnp.zeros_like(o)], axis=0)
    return jnp.concatenate([o[:, h * tq:(h + 1) * tq].T for h in range(heads)], axis=1).astype(BF16)


def _lane_stack(q_ref, heads):
    return jnp.concatenate([q_ref[0, h * LANE:(h + 1) * LANE, :] for h in range(heads)], axis=1)


def _causal_body(q_ref, k_ref, v_ref, o_ref, m_ref, acc_ref, *, heads, tq, tk, group):
    i = pl.program_id(1)
    q0 = i * tq
    qt = _lane_stack(q_ref, heads)
    _tflash_init(m_ref, acc_ref)

    def tile(c0):
        return _dot(k_ref[0, pl.ds(c0, tk), :], qt), v_ref[0, :, pl.ds(c0, tk)]

    def group_step(j, carry):
        c0 = pl.multiple_of(j * group * tk, group * tk)
        _tflash_update_many([tile(c0 + k * tk) for k in range(group)], m_ref, acc_ref)
        return carry

    def full_step(j, carry):
        s, vt = tile(pl.multiple_of(j * tk, tk))
        _tflash_update(s, vt, m_ref, acc_ref)
        return carry

    n_full = q0 // tk
    n_groups = n_full // group
    lax.fori_loop(0, n_groups, group_step, 0)
    lax.fori_loop(group * n_groups, n_full, full_step, 0)
    c0 = pl.multiple_of(n_full * tk, tk)
    s, vt = tile(c0)
    key = c0 + lax.broadcasted_iota(jnp.int32, s.shape, 0)
    qpos = q0 + jnp.bitwise_and(lax.broadcasted_iota(jnp.int32, s.shape, 1), tq - 1)
    _tflash_update(jnp.where(key <= qpos, s, NEG), vt, m_ref, acc_ref)
    o_ref[...] = _tflash_out(acc_ref[...], heads, tq)


def _causal_attn(qt, k, vt, *, heads, tq, tk, group):
    b, _, t = qt.shape
    nq = t // tq
    assert tq & (tq - 1) == 0 and tk % tq == 0 and t % tk == 0
    m = heads * tq
    return pl.pallas_call(
        functools.partial(_causal_body, heads=heads, tq=tq, tk=tk, group=group),
        grid=(b, nq),
        in_specs=[pl.BlockSpec((1, heads * LANE, tq), lambda bi, i: (bi, 0, i)),
                  pl.BlockSpec((1, t, LANE), lambda bi, i: (bi, 0, 0)),
                  pl.BlockSpec((1, VROWS, t), lambda bi, i: (bi, 0, 0))],
        out_specs=pl.BlockSpec((tq, heads * LANE), lambda bi, i: (bi * nq + i, 0)),
        out_shape=jax.ShapeDtypeStruct((b * t, heads * LANE), BF16),
        scratch_shapes=[pltpu.VMEM((1, m), F32), pltpu.VMEM((VROWS, m), F32)],
        compiler_params=_cparams(2),
        name="causal_attn_h%d" % heads,
    )(qt, k, vt)


def _top_k_neg_mask_t(score, k, out):
    n = score.shape[0]
    rowi = lax.broadcasted_iota(jnp.int32, score.shape, 0).astype(F32)
    work = score
    for _ in range(k):
        mx = jnp.max(work, axis=0, keepdims=True)
        idx = jnp.min(jnp.where(work == mx, rowi, float(n)), axis=0, keepdims=True)
        hit = rowi == idx
        out = jnp.where(hit, 0.0, out)
        work = jnp.where(hit, -jnp.inf, work)
    return out


def _nsa_body(q_ref, kvc_ref, kvct_ref, ks_ref, vs_ref, kw_ref, vw_ref, g_ref, cb_ref, sb_ref, wb_ref,
              o_ref, m_ref, acc_ref, m2_ref, acc2_ref, *, tq, tk, pad):
    hh = NSA_HEADS
    i = pl.program_id(1)
    q0 = i * tq
    qt = _lane_stack(q_ref, hh)
    cb = jnp.concatenate([cb_ref[h] for h in range(hh)], axis=1)
    s = _dot(kvc_ref[0], qt) + cb
    e = jnp.exp2(s - jnp.max(s, axis=0, keepdims=True))
    p = e * (1.0 / jnp.sum(e, axis=0, keepdims=True))
    p = jnp.where(cb > 0.5 * NEG, p, 0.0)
    o_cmp = _dot(kvct_ref[0], p.astype(BF16))
    nc = p.shape[0]
    pc = p[:, 0:tq]
    for h in range(1, hh):
        pc = pc + p[:, h * tq:(h + 1) * tq]
    n_sel = nc // 2
    score = pc[:n_sel] + pc[n_sel:]
    blk = lax.broadcasted_iota(jnp.int32, (n_sel, tq), 0)
    qpos = q0 + lax.broadcasted_iota(jnp.int32, (n_sel, tq), 1)
    cur = jnp.right_shift(qpos, _log2(SEL_BLOCK))
    forced = (blk == 0) | (blk == cur) | (blk == cur - 1)
    future = blk * SEL_BLOCK > qpos
    score = jnp.where(forced, -jnp.inf, jnp.where(future, -1e6, score))
    selneg = _top_k_neg_mask_t(score, min(N_SEL, n_sel) - 3, jnp.where(forced, 0.0, NEG)).astype(BF16)
    if n_sel < LANE:
        selneg = jnp.concatenate([selneg, jnp.zeros((LANE - n_sel, tq), BF16)], axis=0)
    qa = jnp.concatenate([qt, jnp.concatenate([selneg] * hh, axis=1)], axis=0)

    _tflash_init(m_ref, acc_ref)

    def far_tile(c0, n):
        return _dot(ks_ref[0, pl.ds(pad + c0, n), :], qa), vs_ref[0, :, pl.ds(pad + c0, n)]

    def far(c0, n):
        _tflash_update_many([far_tile(c0 + k * tk, min(n, tk)) for k in range(max(n // tk, 1))],
                            m_ref, acc_ref)

    n_far = jnp.maximum(i - 1, 0)
    per = GROUP * tk // tq
    n_big = n_far // per

    def big_step(j, carry):
        far(pl.multiple_of(j * GROUP * tk, GROUP * tk), GROUP * tk)
        return carry

    def small_step(j, carry):
        far(pl.multiple_of(j * tq, tq), tq)
        return carry

    lax.fori_loop(0, n_big, big_step, 0)
    lax.fori_loop(n_big * per, n_far, small_step, 0)
    c0 = pl.multiple_of(q0 - tq, tq)
    s_n = _dot(ks_ref[0, pl.ds(pad + c0, 2 * tq), :], qa) + sb_ref[...]
    key = c0 + lax.broadcasted_iota(jnp.int32, s_n.shape, 0)
    _tflash_update(jnp.where(key >= 0, s_n, NEG), vs_ref[0, :, pl.ds(pad + c0, 2 * tq)], m_ref, acc_ref)
    acc_s = acc_ref[...]
    o_sel = acc_s[:NSA_DH] * (1.0 / acc_s[ONES_ROW:ONES_ROW + 1])

    c0 = pl.multiple_of(q0 - WINDOW, tq)
    nw = WINDOW + tq
    s_w = _dot(kw_ref[0, pl.ds(pad + c0, nw), :], qt) + wb_ref[...]
    key = c0 + lax.broadcasted_iota(jnp.int32, s_w.shape, 0)
    _tflash_init(m2_ref, acc2_ref)
    _tflash_update(jnp.where(key >= 0, s_w, NEG), vw_ref[0, :, pl.ds(pad + c0, nw)], m2_ref, acc2_ref)
    acc_w = acc2_ref[...]
    o_win = acc_w[:NSA_DH] * (1.0 / acc_w[ONES_ROW:ONES_ROW + 1])

    g = g_ref[0]
    outs = []
    for h in range(hh):
        sl = slice(h * tq, (h + 1) * tq)
        o_h = (g[3 * h:3 * h + 1] * o_cmp[NSA_DH:, sl] + g[3 * h + 1:3 * h + 2] * o_sel[:, sl]
               + g[3 * h + 2:3 * h + 3] * o_win[:, sl])
        outs.append(jnp.concatenate([o_h, jnp.zeros_like(o_h)], axis=0).T)
    o_ref[...] = jnp.concatenate(outs, axis=1).astype(BF16)


def _bucket_lower_bounds():
    d = np.arange(REL_MAX_DIST + 1)
    exact = REL_BUCKETS // 2
    scaled = np.log(np.maximum(d, 1) / exact) / math.log(REL_MAX_DIST / exact)
    large = np.minimum(exact + (scaled * (REL_BUCKETS - exact)).astype(np.int64), REL_BUCKETS - 1)
    bucket = np.where(d < exact, d, large)
    assert np.all(np.diff(bucket) >= 0) and bucket[-1] == REL_BUCKETS - 1
    return [int(np.argmax(bucket >= k)) for k in range(1, REL_BUCKETS)]


_BUCKET_LO = _bucket_lower_bounds()


def _rel_bias_t(table, dist):
    tab = table.astype(F32)
    extra = (None,) * dist.ndim
    col = lambda k: tab[k][(slice(None),) + extra]
    out = jnp.broadcast_to(col(0), (tab.shape[1],) + dist.shape)
    for k, lo in enumerate(_BUCKET_LO, start=1):
        out = jnp.where((dist >= lo)[None], col(k), out)
    return out


def _head_lanes(a):
    return jnp.concatenate([a[h] for h in range(a.shape[0])], axis=1)


def _nsa_prompt_tables(rel_table, t, tq):
    far = rel_table[REL_BUCKETS - 1].astype(F32)[:, None, None]
    nc = t // CMP_BLOCK
    order = jnp.concatenate([jnp.arange(0, nc, 2), jnp.arange(1, nc, 2)])
    c_end = order * CMP_BLOCK + CMP_BLOCK - 1
    dist = jnp.arange(t)[None, :] - c_end[:, None]
    cb = jnp.where(dist >= 0, _rel_bias_t(rel_table, dist) * LOG2E, NEG)
    dist = jnp.arange(tq)[None, :] + tq - jnp.arange(2 * tq)[:, None]
    sb = jnp.where(dist >= 0, (_rel_bias_t(rel_table, dist) - far) * LOG2E, NEG)
    dist = jnp.arange(tq)[None, :] + WINDOW - jnp.arange(WINDOW + tq)[:, None]
    wb = jnp.where((dist >= 0) & (dist < WINDOW), _rel_bias_t(rel_table, dist) * LOG2E, NEG)
    return cb, _head_lanes(sb), _head_lanes(wb)


def _nsa_prompt(qnt, kvc, kvct, ks, vst, kw, vwt, gt, tables, *, tq, tk, pad):
    b, _, t = qnt.shape
    nq = t // tq
    cb, sb, wb = tables
    nc = kvc.shape[1]
    hh = NSA_HEADS
    m = hh * tq
    per_b = lambda bi, i: (bi, 0, 0)
    tile = lambda bi, i: (bi, 0, i)
    return pl.pallas_call(
        functools.partial(_nsa_body, tq=tq, tk=tk, pad=pad),
        grid=(b, nq),
        in_specs=[pl.BlockSpec((1, hh * LANE, tq), tile),
                  pl.BlockSpec((1, nc, LANE), per_b),
                  pl.BlockSpec((1, LANE, nc), per_b),
                  pl.BlockSpec((1,) + ks.shape[1:], per_b),
                  pl.BlockSpec((1,) + vst.shape[1:], per_b),
                  pl.BlockSpec((1,) + kw.shape[1:], per_b),
                  pl.BlockSpec((1,) + vwt.shape[1:], per_b),
                  pl.BlockSpec((1, 16, tq), tile),
                  pl.BlockSpec((hh, nc, tq), lambda bi, i: (0, 0, i)),
                  pl.BlockSpec(sb.shape, lambda bi, i: (0, 0)),
                  pl.BlockSpec(wb.shape, lambda bi, i: (0, 0))],
        out_specs=pl.BlockSpec((tq, hh * LANE), lambda bi, i: (bi * nq + i, 0)),
        out_shape=jax.ShapeDtypeStruct((b * t, hh * LANE), BF16),
        scratch_shapes=[pltpu.VMEM((1, m), F32), pltpu.VMEM((VROWS, m), F32),
                        pltpu.VMEM((1, m), F32), pltpu.VMEM((VROWS, m), F32)],
        compiler_params=_cparams(2),
        name="nsa_prompt",
    )(qnt, kvc, kvct, ks, vst, kw, vwt, gt, cb, sb, wb)


def _mix_body(x_ref, sc_ref, sh_ref, gt_ref, g_ref, on_ref, om_ref, of_ref, wmg_ref, wbn_ref,
              wuv_ref, wbm_ref, wbf_ref, wo_ref, o_ref):
    x = x_ref[...]
    d = x.shape[1]
    h = _rms(x, g_ref[...]) * (1.0 + sc_ref[0]) + sh_ref[0]
    mg = jax.nn.sigmoid(_dot(h.astype(BF16), wmg_ref[...]))
    o_mla = _dot(om_ref[...], wuv_ref[...]).astype(BF16)
    t = mg[:, :d] * _dot(on_ref[...], wbn_ref[...])
    t = t + mg[:, d:2 * d] * _dot(o_mla, wbm_ref[...])
    t = t + mg[:, 2 * d:] * _dot(of_ref[...], wbf_ref[...])
    o_ref[...] = x + gt_ref[0] * _dot(t.astype(BF16), wo_ref[...])


def _mix(x, sc, sh, gt, mod_index, lw, o_nsa, o_lat, o_fox, *, tm, value_lane):
    n, d = x.shape
    row = lambda i: (i, 0)
    const2 = lambda i: (0, 0)
    ws = [lw["w_mg"], lw["w_br_n"][value_lane], lw["w_uv"], lw["w_br_m"], lw["w_br_f"][value_lane],
          lw["w_o"]]
    return pl.pallas_call(
        _mix_body,
        grid=(n // tm,),
        in_specs=[pl.BlockSpec((tm, d), row)]
        + [pl.BlockSpec((1,) + a.shape[1:], mod_index) for a in (sc, sh, gt)]
        + [pl.BlockSpec((1, d), const2)]
        + [pl.BlockSpec((tm, a.shape[1]), row) for a in (o_nsa, o_lat, o_fox)]
        + [pl.BlockSpec(w.shape, const2) for w in ws],
        out_specs=pl.BlockSpec((tm, d), row),
        out_shape=jax.ShapeDtypeStruct((n, d), F32),
        compiler_params=_cparams(1),
        name="mix",
    )(x, sc, sh, gt, lw["g_attn"], o_nsa, o_lat, o_fox, *ws)


def _gelu_tanh(x):
    return 0.5 * x * (1.0 + jnp.tanh(math.sqrt(2.0 / math.pi) * (x + 0.044715 * (x * x * x))))


def _ffn_body(x_ref, sc_ref, sh_ref, gt_ref, g_ref, pre_ref, win_ref, cw_ref, cb_ref, wout_ref,
              gf_ref, o_ref, tail_ref, hist_ref, *, tiles_per_seq, time_major, final_norm, d_ff):
    x = x_ref[...]
    tm = x.shape[0]
    i = pl.program_id(0)
    h = _rms(x, g_ref[...]) * (1.0 + sc_ref[0]) + sh_ref[0]
    ab = _dot(h.astype(BF16), win_ref[...])
    a = ab[:, :d_ff]
    b = ab[:, d_ff:]
    cw = cw_ref[...]
    if time_major:
        @pl.when(i % tiles_per_seq == 0)
        def _():
            hist_ref[0] = pre_ref[0]
            hist_ref[1] = pre_ref[1]
        a2 = hist_ref[0]
        a1 = hist_ref[1]
        hist_ref[0] = a1
        hist_ref[1] = a
        tail_ref[0] = a
    else:
        @pl.when(i % tiles_per_seq == 0)
        def _():
            hist_ref[0, 6:8] = pre_ref[0]
        prev = hist_ref[0, 6:8]
        rid = lax.broadcasted_iota(jnp.int32, (tm, 1), 0)
        a1 = jnp.where(rid == 0, prev[1:2], pltpu.roll(a, 1, 0))
        a2 = jnp.where(rid == 0, prev[0:1], jnp.where(rid == 1, prev[1:2], pltpu.roll(a, 2, 0)))
        hist_ref[0] = a[tm - 8:]
        tail_ref[0] = a[tm - 2:]
    conv = a2 * cw[0:1] + a1 * cw[1:2] + a * cw[2:3] + cb_ref[...]
    y = _dot((_gelu_tanh(conv) * b).astype(BF16), wout_ref[...])
    out = x + gt_ref[0] * y
    if final_norm:
        out = _rms(out, gf_ref[...])
    o_ref[...] = out


def _ffn(x, sc, sh, gt, mod_index, lw, prefix, g_final, *, tm, tiles_per_seq, time_major, final_norm):
    n, d = x.shape
    d_ff = lw["w_ffn_out"].shape[0]
    row = lambda i: (i, 0)
    const2 = lambda i: (0, 0)
    n_tiles = n // tm
    if time_major:
        pre_spec = pl.BlockSpec(prefix.shape, lambda i: (0, 0, 0))
        tail_spec = pl.BlockSpec((1, tm, d_ff),
                                 lambda i: (jnp.maximum(i - (tiles_per_seq - 2), 0), 0, 0))
        tail_shape = jax.ShapeDtypeStruct((2, tm, d_ff), F32)
        hist = pltpu.VMEM((2, tm, d_ff), F32)
    else:
        pre_spec = pl.BlockSpec((1, 2, d_ff), lambda i: (i // tiles_per_seq, 0, 0))
        tail_spec = pl.BlockSpec((1, 2, d_ff), lambda i: (i // tiles_per_seq, 0, 0))
        tail_shape = jax.ShapeDtypeStruct((n_tiles // tiles_per_seq, 2, d_ff), F32)
        hist = pltpu.VMEM((1, 8, d_ff), F32)
    return pl.pallas_call(
        functools.partial(_ffn_body, tiles_per_seq=tiles_per_seq, time_major=time_major,
                          final_norm=final_norm, d_ff=d_ff),
        grid=(n_tiles,),
        in_specs=[pl.BlockSpec((tm, d), row)]
        + [pl.BlockSpec((1,) + a.shape[1:], mod_index) for a in (sc, sh, gt)]
        + [pl.BlockSpec((1, d), const2), pre_spec,
           pl.BlockSpec(lw["w_ffn_in"].shape, const2),
           pl.BlockSpec((CONV_W, d_ff), const2), pl.BlockSpec((1, d_ff), const2),
           pl.BlockSpec(lw["w_ffn_out"].shape, const2), pl.BlockSpec((1, d), const2)],
        out_specs=[pl.BlockSpec((tm, d), row), tail_spec],
        out_shape=[jax.ShapeDtypeStruct((n, d), F32), tail_shape],
        scratch_shapes=[hist],
        compiler_params=_cparams(1),
        name="conv_ffn",
    )(x, sc, sh, gt, lw["g_ffn"], prefix, lw["w_ffn_in"], lw["conv_w"], lw["conv_b"],
      lw["w_ffn_out"], g_final)


def _pad_heads(w, heads, dh, scale=1.0):
    k = w.shape[0]
    w = (w * scale).reshape(k, heads, dh)
    return jnp.pad(w, ((0, 0), (0, 0), (0, LANE - dh))).reshape(k, heads * LANE)


def _pad_head_rows(w, heads, dh, offset):
    n = w.shape[1]
    w = w.reshape(heads, dh, n)
    return jnp.pad(w, ((0, 0), (offset, LANE - dh - offset), (0, 0))).reshape(heads * LANE, n)


def _pad_cols(w, n=LANE):
    return jnp.pad(w, ((0, 0), (0, n - w.shape[1])))


def _prep_layer(l, p):
    d = p["w_in"].shape[1]
    w_in = p["w_in"][l]
    nsa_w = NSA_HEADS * NSA_DH
    fox_w = FOX_HEADS * FOX_DH
    q_rank = p["mla_g_q"].shape[1]
    kv_w = MLA_KV_RANK + MLA_ROPE
    splits = (nsa_w, 2 * NSA_DH, 2 * NSA_DH, 2 * NSA_DH, 3 * NSA_HEADS, q_rank, kv_w,
              fox_w, 2 * FOX_DH, FOX_HEADS, 3 * d)
    cuts = [int(c) for c in np.cumsum(splits)[:-1]]
    (w_nq, w_nc, w_ns, w_nw, w_ng, w_qd, w_kvd, w_fq, w_fkv, w_ff, w_mg) = jnp.split(w_in, cuts, axis=1)
    half = MLA_ROPE // 2
    w_kr = w_kvd[:, MLA_KV_RANK:]
    w_kr_sw = jnp.concatenate([w_kr[:, half:], w_kr[:, :half]], axis=1)
    w_t = jnp.concatenate([w_nc, w_ns, w_nw, w_fkv, w_kvd, w_kr_sw, w_ff, _pad_cols(w_ng, 16), w_qd,
                           _pad_heads(w_nq, NSA_HEADS, NSA_DH, NSA_SCALE * LOG2E),
                           _pad_heads(w_fq, FOX_HEADS, FOX_DH, FOX_SCALE * LOG2E)], axis=1)
    assert w_t.shape[1] == _T_END
    w_row = jnp.concatenate([_pad_cols(w_ns[:, :NSA_DH]), _pad_cols(w_nw[:, :NSA_DH]),
                             _pad_cols(w_fkv[:, :FOX_DH]), _pad_cols(w_kvd),
                             _pad_cols(jnp.pad(w_kr_sw, ((0, 0), (MLA_KV_RANK, 0)))),
                             w_nc, _pad_cols(w_ff)], axis=1)
    assert w_row.shape[1] == _R_END
    aug = np.zeros((FOX_HEADS, LANE), np.float32)
    for h in range(FOX_HEADS):
        for part in range(DECAY_PARTS):
            aug[h, FOX_DH + part * FOX_HEADS + h] = 1.0
    w_uq = p["mla_w_uq"][l].reshape(q_rank, MLA_HEADS, MLA_NOPE + MLA_ROPE)
    uq_nope = w_uq[:, :, :MLA_NOPE].reshape(q_rank, -1)
    uq_rope = w_uq[:, :, MLA_NOPE:]
    uq_rope_sw = jnp.concatenate([uq_rope[:, :, half:], uq_rope[:, :, :half]], axis=2)
    w_uq2 = jnp.concatenate([uq_nope, uq_rope.reshape(q_rank, -1), uq_rope_sw.reshape(q_rank, -1)],
                            axis=1)
    w_uk = p["mla_w_uk"][l]
    eye_h = np.eye(MLA_HEADS, dtype=np.float32)
    uk_bd = jnp.einsum("chd,hg->hdgc", w_uk, eye_h)
    uk_bd = jnp.pad(uk_bd, ((0, 0), (0, 0), (0, 0), (0, LANE - MLA_KV_RANK)))
    uk_bd = uk_bd.reshape(MLA_HEADS * MLA_NOPE, MLA_HEADS * LANE)
    rr = np.arange(MLA_HEADS * MLA_ROPE)
    pp = (np.arange(MLA_HEADS * LANE)[None, :]
          == ((rr // MLA_ROPE) * LANE + MLA_KV_RANK + rr % MLA_ROPE)[:, None]).astype(np.float32)
    w_uv = p["mla_w_uv"][l]
    uv_bd = jnp.einsum("chv,hg->hcgv", w_uv, eye_h)
    uv_bd = jnp.pad(uv_bd, ((0, 0), (0, LANE - MLA_KV_RANK), (0, 0), (0, 0)))
    uv_bd = uv_bd.reshape(MLA_HEADS * LANE, -1).astype(BF16)
    w_br = p["w_br"][l]
    mla_w = w_uv.shape[1] * w_uv.shape[2]
    br_n, br_f = w_br[:nsa_w], w_br[nsa_w + mla_w:]
    g_kv = p["mla_g_kv"][l]
    b_f = p["fox_b_f"][l]
    return {
        "g_attn": p["g_attn"][l][None, :], "g_ffn": p["g_ffn"][l][None, :],
        "w_row": w_row.astype(BF16), "w_t": w_t.T.astype(BF16), "g_q": p["mla_g_q"][l][:, None],
        "w_uq": w_uq2.T.astype(BF16), "w_uk": uk_bd.T.astype(BF16), "pp": pp.T.astype(BF16),
        "g_kv": g_kv[:, None], "g_kv_row": _pad_cols(g_kv[None, :]),
        "b_f": b_f[:, None], "b_f_row": _pad_cols(b_f[None, :]),
        "q_aug": jnp.asarray(aug.reshape(FOX_HEADS * LANE, 1)),
        "w_mg": w_mg.astype(BF16),
        "w_br_n": {o: _pad_head_rows(br_n, NSA_HEADS, NSA_DH, o).astype(BF16) for o in (0, NSA_DH)},
        "w_uv": uv_bd,
        "w_br_m": w_br[nsa_w:nsa_w + mla_w].astype(BF16),
        "w_br_f": {o: _pad_head_rows(br_f, FOX_HEADS, FOX_DH, o).astype(BF16) for o in (0, FOX_DH)},
        "w_o": p["w_o"][l].astype(BF16),
        "w_ffn_in": p["w_ffn_in"][l].astype(BF16), "conv_w": p["conv_w"][l],
        "conv_b": p["conv_b"][l][None, :], "w_ffn_out": p["w_ffn_out"][l].astype(BF16),
    }


def _rope_tables(pos):
    half = MLA_ROPE // 2
    inv = ROPE_BASE ** (-jnp.arange(half, dtype=F32) / half)
    ang = pos.astype(F32)[:, None] * inv[None, :]
    cos, sin = jnp.cos(ang), jnp.sin(ang)
    c1 = jnp.concatenate([cos, cos], axis=1)
    c2 = jnp.concatenate([-sin, sin], axis=1)
    padr = ((0, 0), (MLA_KV_RANK, LANE - MLA_KV_RANK - MLA_ROPE))
    return (c1.T, c2.T, jnp.tile(c1, (1, MLA_HEADS)).T, jnp.tile(c2, (1, MLA_HEADS)).T,
            jnp.pad(c1, padr), jnp.pad(c2, padr))


def _page_copy(cache_ref, layer, page, buf, slot, p, rows, sem):
    dst = buf.at[slot, pl.ds(0, rows), pl.ds(pl.multiple_of(p * LANE, LANE), LANE)]
    return pltpu.make_async_copy(cache_ref.at[layer, page], dst, sem)


def _wait_all(buf, sem):
    pltpu.make_async_copy(buf, buf, sem).wait()


def _merge_partials(parts):
    m = parts[0][0]
    for mc, _, _ in parts[1:]:
        m = jnp.maximum(m, mc)
    l = acc = None
    for mc, lc, ac in parts:
        w = jnp.exp2(mc - m)
        l = w * lc if l is None else l + w * lc
        acc = w * ac if acc is None else acc + w * ac
    return acc / l


def _softmax_parts(parts):
    m = parts[0].max(axis=-1, keepdims=True)
    for s in parts[1:]:
        m = jnp.maximum(m, s.max(axis=-1, keepdims=True))
    ps = [jnp.exp2(s - m) for s in parts]
    l = ps[0].sum(axis=-1, keepdims=True)
    for p in ps[1:]:
        l = l + p.sum(axis=-1, keepdims=True)
    return ps, l


def _paged_body(pt_ref, q_ref, knew_ref, *rest, layer, n_seq, n_pages, rows, heads, s_len, tk,
                with_decay):
    if with_decay:
        (lfn_ref, tri_s_ref, tri_ref, cache_ref, lcache_ref, o_ref, kbuf, lbuf, sem) = rest
    else:
        (cache_ref, o_ref, kbuf, sem) = rest
    b = pl.program_id(0)
    slot = b % 2
    past = n_pages * LANE
    sp = knew_ref.shape[1]

    def issue(seq, sl):
        def body(p, c):
            pg = pt_ref[seq, p]
            _page_copy(cache_ref, layer, pg, kbuf, sl, p, rows, sem.at[sl, 0]).start()
            if with_decay:
                pltpu.make_async_copy(lcache_ref.at[layer, pg], lbuf.at[sl, p], sem.at[sl, 1]).start()
            return c
        lax.fori_loop(0, n_pages, body, 0, unroll=4)

    def wait(sl):
        _wait_all(kbuf.at[sl, pl.ds(0, rows), :], sem.at[sl, 0])
        if with_decay:
            _wait_all(lbuf.at[sl], sem.at[sl, 1])

    @pl.when(b == 0)
    def _():
        if rows < LANE:
            kbuf[:, rows:, :] = jnp.zeros((2, LANE - rows, past), F32)
        issue(0, 0)

    @pl.when(b + 1 < n_seq)
    def _():
        issue(b + 1, 1 - slot)

    wait(slot)

    q = q_ref[0]
    m_rows = q.shape[0]
    knew = knew_ref[0].astype(BF16)
    s_new = _dot_nt(q, knew)
    if with_decay:
        cin = _dot_terms(lbuf[slot].reshape(n_pages * 8, LANE), tri_ref[...], 3)
        run = jnp.zeros((8, 1), F32)
        negc = []
        for p in range(n_pages):
            cp = cin[p * 8:(p + 1) * 8]
            negc.append(-(cp + run) * LOG2E)
            run = run + cp[:, LANE - 1:]
        cs_new = _dot_f32(lfn_ref[0], tri_s_ref[...])
        s_new = s_new + jnp.concatenate([-(run + cs_new) * LOG2E] * s_len, axis=0)
    sq = jnp.right_shift(lax.broadcasted_iota(jnp.int32, (m_rows, sp), 0), _log2(heads))
    jj = lax.broadcasted_iota(jnp.int32, (m_rows, sp), 1)
    s_new = jnp.where(jj <= sq, s_new, NEG)
    m = jnp.max(s_new, axis=-1, keepdims=True)
    p = jnp.exp2(s_new - m)
    parts = [(m, jnp.sum(p, axis=-1, keepdims=True), _dot(p.astype(BF16), knew))]
    ppc = tk // LANE
    for c in range(past // tk):
        kt = kbuf[slot, :, c * tk:(c + 1) * tk].astype(BF16)
        s = _dot(q, kt)
        if with_decay:
            nc = jnp.concatenate(negc[c * ppc:(c + 1) * ppc], axis=1)
            s = s + jnp.concatenate([nc] * s_len, axis=0)
        m = jnp.max(s, axis=-1, keepdims=True)
        p = jnp.exp2(s - m)
        parts.append((m, jnp.sum(p, axis=-1, keepdims=True), _dot_nt(p.astype(BF16), kt)))
    o_ref[0] = _merge_partials(parts).astype(BF16)


def _paged_attn(page_table, q, knew, cache_t, layer, *, heads, s_len, decay=None):
    n_seq, n_pages = page_table.shape
    rows = cache_t.shape[2]
    past = n_pages * LANE
    tk = past // 2
    m = q.shape[1]
    sp = knew.shape[1]
    with_decay = decay is not None
    per_b = lambda b, pt: (b, 0, 0)
    in_specs = [pl.BlockSpec((1, m, LANE), per_b), pl.BlockSpec((1, sp, LANE), per_b)]
    args = [q, knew]
    scratch = [pltpu.VMEM((2, LANE, past), F32)]
    if with_decay:
        lfn, lcache_t = decay
        tri_s = np.triu(np.ones((sp, sp), np.float32))
        tri = jnp.asarray(np.triu(np.ones((LANE, LANE), np.float32)), BF16)
        in_specs += [pl.BlockSpec((1, 8, sp), per_b), pl.BlockSpec((sp, sp), lambda b, pt: (0, 0)),
                     pl.BlockSpec((LANE, LANE), lambda b, pt: (0, 0)),
                     pl.BlockSpec(memory_space=pl.ANY), pl.BlockSpec(memory_space=pl.ANY)]
        args += [lfn, tri_s, tri, cache_t, lcache_t]
        scratch.append(pltpu.VMEM((2, n_pages, 8, LANE), F32))
    else:
        in_specs.append(pl.BlockSpec(memory_space=pl.ANY))
        args.append(cache_t)
    scratch.append(pltpu.SemaphoreType.DMA((2, 2)))
    return pl.pallas_call(
        functools.partial(_paged_body, layer=layer, n_seq=n_seq, n_pages=n_pages, rows=rows,
                          heads=heads, s_len=s_len, tk=tk, with_decay=with_decay),
        grid_spec=pltpu.PrefetchScalarGridSpec(
            num_scalar_prefetch=1, grid=(n_seq,), in_specs=in_specs,
            out_specs=pl.BlockSpec((1, m, LANE), per_b), scratch_shapes=scratch),
        out_shape=jax.ShapeDtypeStruct((n_seq, m, LANE), BF16),
        compiler_params=_cparams(1),
        name="paged_attn_decay" if with_decay else "paged_attn",
    )(page_table, *args)


def _nsa_s1_body(pt_ref, q_ref, wnew_ref, win_ref, cb_ref, wb_ref, wnb_ref, amat_ref, pair_ref,
                 cache_ref, ocmp_ref, owin_ref, idx_ref, kbuf, sem, *, layer, n_seq, n_pages, s_len,
                 k_free):
    b = pl.program_id(0)
    slot = b % 2
    past = n_pages * LANE
    hh = NSA_HEADS

    def issue(seq, sl):
        def body(p, c):
            _page_copy(cache_ref, layer, pt_ref[seq, p], kbuf, sl, p, LANE, sem.at[sl]).start()
            return c
        lax.fori_loop(0, n_pages, body, 0, unroll=4)

    @pl.when(b == 0)
    def _():
        issue(0, 0)

    @pl.when(b + 1 < n_seq)
    def _():
        issue(b + 1, 1 - slot)

    _wait_all(kbuf.at[slot], sem.at[slot])
    q = q_ref[0]
    tc = amat_ref.shape[0]
    amat = amat_ref[...]
    means = []
    for c in range(past // tc):
        means.append(_dot_terms(kbuf[slot, :, c * tc:(c + 1) * tc], amat, 2))
    kvc = jnp.concatenate(means, axis=1).astype(BF16)
    s = _dot(q, kvc) + cb_ref[...]
    e = jnp.exp2(s - jnp.max(s, axis=-1, keepdims=True))
    p = e / jnp.sum(e, axis=-1, keepdims=True)
    ocmp_ref[0] = _dot_nt(p.astype(BF16), kvc)
    ps = _dot_f32(p, pair_ref[...])
    score = ps[0:s_len]
    for h in range(1, hh):
        score = score + ps[h * s_len:(h + 1) * s_len]
    n_past = score.shape[1]
    lane = lax.broadcasted_iota(jnp.int32, score.shape, 1).astype(F32)
    work = jnp.where((lane == 0.0) | (lane == n_past - 1.0), -jnp.inf, score)
    out_lane = lax.broadcasted_iota(jnp.int32, (s_len, LANE), 1)
    idx_out = jnp.zeros((s_len, LANE), F32)
    for r in range(k_free):
        mx = jnp.max(work, axis=-1, keepdims=True)
        idx = jnp.min(jnp.where(work == mx, lane, float(n_past)), axis=-1, keepdims=True)
        idx_out = jnp.where(out_lane == r, idx, idx_out)
        work = jnp.where(lane == idx, -jnp.inf, work)
    idx_ref[0] = idx_out.astype(jnp.int32)
    wst = win_ref[0, 0].astype(BF16)
    wnew = wnew_ref[0].astype(BF16)
    (p_w, p_n), l = _softmax_parts([_dot(q, wst) + wb_ref[...], _dot_nt(q, wnew) + wnb_ref[...]])
    owin_ref[0] = (_dot_nt(p_w.astype(BF16), wst) + _dot(p_n.astype(BF16), wnew)) / l


def _nsa_s1(page_table, q, wnew, win_t, cache_t, layer, tables, *, s_len, k_free):
    n_seq, n_pages = page_table.shape
    past = n_pages * LANE
    cb, wb, wnb = tables
    m = q.shape[1]
    sp = wnew.shape[1]
    wlen = win_t.shape[3]
    tc = min(past, CMP_BLOCK * LANE)
    amat = (np.arange(tc)[:, None] // CMP_BLOCK == np.arange(tc // CMP_BLOCK)[None, :])
    amat = jnp.asarray(amat.astype(np.float32) / CMP_BLOCK, BF16)
    nc = past // CMP_BLOCK
    ratio = SEL_BLOCK // CMP_BLOCK
    pair = (np.arange(nc)[:, None] // ratio == np.arange(nc // ratio)[None, :]).astype(np.float32)
    per_b = lambda b, pt: (b, 0, 0)
    c2 = lambda b, pt: (0, 0)
    return pl.pallas_call(
        functools.partial(_nsa_s1_body, layer=layer, n_seq=n_seq, n_pages=n_pages, s_len=s_len,
                          k_free=k_free),
        grid_spec=pltpu.PrefetchScalarGridSpec(
            num_scalar_prefetch=1, grid=(n_seq,),
            in_specs=[pl.BlockSpec((1, m, LANE), per_b), pl.BlockSpec((1, sp, LANE), per_b),
                      pl.BlockSpec((1, 1, LANE, wlen), lambda b, pt: (layer, b, 0, 0)),
                      pl.BlockSpec(cb.shape, c2), pl.BlockSpec(wb.shape, c2),
                      pl.BlockSpec(wnb.shape, c2), pl.BlockSpec(amat.shape, c2),
                      pl.BlockSpec(pair.shape, c2), pl.BlockSpec(memory_space=pl.ANY)],
            out_specs=[pl.BlockSpec((1, m, LANE), per_b), pl.BlockSpec((1, m, LANE), per_b),
                       pl.BlockSpec((1, s_len, LANE), per_b)],
            scratch_shapes=[pltpu.VMEM((2, LANE, past), F32), pltpu.SemaphoreType.DMA((2,))]),
        out_shape=[jax.ShapeDtypeStruct((n_seq, m, LANE), F32),
                   jax.ShapeDtypeStruct((n_seq, m, LANE), F32),
                   jax.ShapeDtypeStruct((n_seq, s_len, LANE), jnp.int32)],
        compiler_params=_cparams(1),
        name="nsa_sample_cmp",
    )(page_table, q, wnew, win_t, cb, wb, wnb, amat, pair, cache_t)


def _nsa_s2_body(pt_ref, idx_ref, q_ref, snew_ref, g_ref, ocmp_ref, owin_ref, lb_ref, nb_ref,
                 cache_ref, o_ref, kbuf, sem, *, layer, n_seq, n_pages, s_len, k_free):
    b = pl.program_id(0)
    slot = b % 2
    n_own = s_len * k_free
    n_slots = n_own + 2
    per_page = LANE // SEL_BLOCK

    def block_of(seq, j):
        if j < n_own:
            return idx_ref[(seq * s_len + j // k_free) * LANE + j % k_free]
        return 0 if j == n_own else n_pages * per_page - 1

    def issue(seq, sl):
        for j in range(n_slots):
            blk = block_of(seq, j)
            pg = pt_ref[seq, blk >> _log2(per_page)]
            _page_copy(cache_ref, layer, pg, kbuf, sl, j, LANE, sem.at[sl]).start()

    @pl.when(b == 0)
    def _():
        issue(0, 0)

    @pl.when(b + 1 < n_seq)
    def _():
        issue(b + 1, 1 - slot)

    _wait_all(kbuf.at[slot], sem.at[slot])
    q = q_ref[0]
    m_rows = q.shape[0]
    kt = kbuf[slot].astype(BF16)
    s = _dot(q, kt)
    assert s_len & (s_len - 1) == 0
    row_s = jnp.bitwise_and(lax.broadcasted_iota(jnp.int32, (m_rows, LANE), 0), s_len - 1)
    lane_half = jnp.right_shift(lax.broadcasted_iota(jnp.int32, (m_rows, LANE), 1), _log2(SEL_BLOCK))
    bias = []
    last_page = (n_pages - 1) * per_page
    for j in range(n_slots):
        blk = block_of(b, j)
        ok = lane_half == (blk & (per_page - 1))
        if j < n_own:
            ok = ok & (row_s == j // k_free)
            near = jnp.where(blk >= last_page, lb_ref[...], 0.0)
        else:
            near = lb_ref[...] if j == n_slots - 1 else 0.0
        bias.append(jnp.where(ok, near, NEG))
    s = s + jnp.concatenate(bias, axis=1)
    snew = snew_ref[0].astype(BF16)
    (p_s, p_n), l = _softmax_parts([s, _dot_nt(q, snew) + nb_ref[...]])
    o_sel = (_dot_nt(p_s.astype(BF16), kt) + _dot(p_n.astype(BF16), snew)) / l
    g = g_ref[0]
    o = g[:, 0:1] * ocmp_ref[0] + g[:, 1:2] * o_sel + g[:, 2:3] * owin_ref[0]
    o_ref[0] = o.astype(BF16)


def _nsa_s2(page_table, idx, q, snew, gates, o_cmp, o_win, cache_t, layer, tables, *, s_len, k_free):
    n_seq, n_pages = page_table.shape
    lb, nb = tables
    m = q.shape[1]
    sp = snew.shape[1]
    n_slots = s_len * k_free + 2
    per_b = lambda b, pt, ix: (b, 0, 0)
    c2 = lambda b, pt, ix: (0, 0)
    return pl.pallas_call(
        functools.partial(_nsa_s2_body, layer=layer, n_seq=n_seq, n_pages=n_pages, s_len=s_len,
                          k_free=k_free),
        grid_spec=pltpu.PrefetchScalarGridSpec(
            num_scalar_prefetch=2, grid=(n_seq,),
            in_specs=[pl.BlockSpec((1, m, LANE), per_b), pl.BlockSpec((1, sp, LANE), per_b),
                      pl.BlockSpec((1, m, LANE), per_b), pl.BlockSpec((1, m, LANE), per_b),
                      pl.BlockSpec((1, m, LANE), per_b), pl.BlockSpec(lb.shape, c2),
                      pl.BlockSpec(nb.shape, c2), pl.BlockSpec(memory_space=pl.ANY)],
            out_specs=pl.BlockSpec((1, m, LANE), per_b),
            scratch_shapes=[pltpu.VMEM((2, LANE, n_slots * LANE), F32),
                            pltpu.SemaphoreType.DMA((2,))]),
        out_shape=jax.ShapeDtypeStruct((n_seq, m, LANE), BF16),
        compiler_params=_cparams(1),
        name="nsa_sample_sel",
    )(page_table, idx.reshape(-1), q, snew, gates, o_cmp, o_win, lb, nb, cache_t)


def _nsa_sample_tables(rel_table, past, s_len, sp, wlen):
    hh = NSA_HEADS
    far = rel_table[REL_BUCKETS - 1].astype(F32)[:, None, None]
    qpos = past + jnp.arange(s_len)
    rows = lambda a: a.reshape(hh * s_len, a.shape[-1])
    c_end = jnp.arange(past // CMP_BLOCK) * CMP_BLOCK + CMP_BLOCK - 1
    cb = rows(_rel_bias_t(rel_table, qpos[:, None] - c_end[None, :]) * LOG2E)
    dist = qpos[:, None] - (past - wlen + jnp.arange(wlen))[None, :]
    wb = rows(jnp.where(dist < WINDOW, _rel_bias_t(rel_table, dist) * LOG2E, NEG))
    dist_new = jnp.arange(s_len)[:, None] - jnp.arange(sp)[None, :]
    newb = _rel_bias_t(rel_table, dist_new)
    wnb = rows(jnp.where(dist_new >= 0, newb * LOG2E, NEG))
    assert LANE >= REL_MAX_DIST
    dist = qpos[:, None] - (past - LANE + jnp.arange(LANE))[None, :]
    lb = rows((_rel_bias_t(rel_table, dist) - far) * LOG2E)
    nb = rows(jnp.where(dist_new >= 0, (newb - far) * LOG2E, NEG))
    return (cb, wb, wnb), (lb, nb)


TQ = 256
TQ_FOX = 128
TK = 256
GROUP = 4
TM_PROMPT = 256
TM_FFN = 512
NEW_PAD = 16
KEY_PAD = WINDOW


def _fm_to_rows(a, mid):
    lead = a.shape[:-2]
    n = a.shape[-1]
    a = a.reshape(lead + mid + (n,))
    return jnp.moveaxis(a, -1, len(lead))


def kernel(x_prompt, x_sample, cache_nsa_cmp, cache_nsa_sel, state_nsa_win, cache_mla, cache_fox_kv, cache_fox_logf, state_ffn_conv, page_table, c_prompt, c_sample, rel_table, w_ada, b_ada, g_attn, g_ffn, w_in, mla_g_q, mla_w_uq, mla_g_kv, mla_w_uk, mla_w_uv, fox_b_f, w_br, w_o, w_ffn_in, conv_w, conv_b, w_ffn_out, g_final):
    b, t, d = x_prompt.shape
    db, s_len, _ = x_sample.shape
    depth = w_in.shape[0]
    pool = cache_nsa_cmp.shape[1]
    n_pages = page_table.shape[1]
    past = n_pages * LANE
    d_ff = w_ffn_out.shape[1]
    wlen = state_nsa_win.shape[2]
    assert cache_nsa_cmp.shape[2] == LANE and TQ >= REL_MAX_DIST and t % TK == 0
    assert t // SEL_BLOCK <= LANE and TM_PROMPT == 8 * CMP_BLOCK
    params = dict(w_in=w_in, mla_g_q=mla_g_q, mla_w_uq=mla_w_uq, mla_g_kv=mla_g_kv, mla_w_uk=mla_w_uk,
                  mla_w_uv=mla_w_uv, fox_b_f=fox_b_f, w_br=w_br, w_o=w_o, w_ffn_in=w_ffn_in,
                  conv_w=conv_w, conv_b=conv_b, w_ffn_out=w_ffn_out, g_attn=g_attn, g_ffn=g_ffn)

    n_c = b + db
    c_all = jnp.pad(jnp.concatenate([c_prompt, c_sample], axis=0), ((0, -n_c % 8), (0, 0)))
    mods = _adaln(c_all, w_ada, b_ada)

    cmp_t = jnp.transpose(cache_nsa_cmp, (0, 1, 3, 4, 2)).reshape(depth, pool, LANE, LANE)
    sel_t = jnp.transpose(cache_nsa_sel, (0, 1, 3, 4, 2)).reshape(depth, pool, LANE, LANE)
    fkv_t = jnp.transpose(cache_fox_kv, (0, 1, 3, 4, 5, 2)).reshape(depth, pool, LANE, LANE)
    mla_t = jnp.transpose(cache_mla, (0, 1, 3, 2))
    lf_t = jnp.transpose(cache_fox_logf, (0, 1, 3, 2))
    win_t = jnp.transpose(state_nsa_win, (0, 1, 3, 4, 2)).reshape(depth, db, LANE, wlen)
    conv_pre_s = jnp.transpose(state_ffn_conv, (0, 2, 1, 3))
    conv_pre_p = jnp.zeros((b, CONV_W - 1, d_ff), F32)

    rope_p = _rope_tables(jnp.arange(t))
    rope_s = _rope_tables(jnp.repeat(past + jnp.arange(s_len), db))
    tabs_p = _nsa_prompt_tables(rel_table, t, TQ)
    tabs_s1, tabs_s2 = _nsa_sample_tables(rel_table, past, s_len, NEW_PAD, wlen)
    n_sel_s = -(-(past + s_len) // SEL_BLOCK)
    k_free = min(N_SEL, n_sel_s) - 3

    tpb = t // TM_PROMPT
    tpb_ffn = t // TM_FFN
    idx_p = lambda i: (i // tpb, 0, 0)
    idx_pf = lambda i: (i // tpb_ffn, 0, 0)
    idx_s = lambda i: (0, 0, 0)
    xp = x_prompt.reshape(b * t, d)
    xs = jnp.transpose(x_sample, (1, 0, 2)).reshape(s_len * db, d)

    def new_rows(a):
        return jnp.pad(jnp.transpose(a, (2, 0, 1)), ((0, 0), (0, NEW_PAD - s_len), (0, 0)))

    def q_rows(qt, heads, width, head_major):
        q = jnp.transpose(qt.reshape(s_len, heads, LANE, db)[:, :, :width],
                          (3, 1, 0, 2) if head_major else (3, 0, 1, 2))
        return jnp.pad(q.reshape(db, s_len * heads, width), ((0, 0), (0, 0), (0, LANE - width)))

    def o_rows(o, heads, head_major):
        o = o.reshape((db, heads, s_len, LANE) if head_major else (db, s_len, heads, LANE))
        o = jnp.transpose(o, (2, 0, 1, 3) if head_major else (1, 0, 2, 3))
        return o.reshape(s_len * db, heads * LANE)

    def front_pad(a, axis):
        pads = [(0, 0)] * a.ndim
        pads[axis] = (KEY_PAD, 0)
        return jnp.pad(a, pads)

    rows_p, rows_s = [], []
    for l in range(depth):
        lw = _prep_layer(l, params)
        m6 = [mods[l][:, k * d:(k + 1) * d] for k in range(6)]
        sh_a, sc_a, gt_a, sh_f, sc_f, gt_f = [a[:b][:, None, :] for a in m6]
        sh_as, sc_as, gt_as, sh_fs, sc_fs, gt_fs = [a[b:n_c][None] for a in m6]
        last = l == depth - 1

        pre = _pre_proj(xp, sc_a, sh_a, idx_p, lw, rope_p, n_groups=b, tiles_per_group=tpb,
                        n_pos_tiles=tpb, tm=TM_PROMPT, with_keys=True)
        nc = t // CMP_BLOCK
        kvc = pre["kvc"].reshape(b, nc // 2, 2, LANE)
        kvc = jnp.transpose(kvc, (0, 2, 1, 3)).reshape(b, nc, LANE)
        kvc_k = jnp.pad(kvc[:, :, :NSA_DH], ((0, 0), (0, 0), (0, LANE - NSA_DH))).astype(BF16)
        kvct = jnp.transpose(kvc, (0, 2, 1)).astype(BF16)
        rs = lambda a: a.reshape(b, t, a.shape[-1])
        o_nsa = _nsa_prompt(pre["qnt"], kvc_k, kvct, front_pad(rs(pre["ks"]), 1),
                            front_pad(pre["vst"], 2), front_pad(rs(pre["kw"]), 1),
                            front_pad(pre["vwt"], 2), pre["gt"], tabs_p, tq=TQ, tk=TK, pad=KEY_PAD)
        o_lat = _causal_attn(pre["qmt"], rs(pre["km"]), pre["vmt"], heads=MLA_HEADS, tq=TQ, tk=TK,
                             group=GROUP)
        o_fox = _causal_attn(pre["qft"], rs(pre["kf"]), pre["vft"], heads=FOX_HEADS, tq=TQ_FOX, tk=TK,
                             group=GROUP)
        x1 = _mix(xp, sc_a, sh_a, gt_a, idx_pf, lw, o_nsa, o_lat, o_fox, tm=TM_FFN, value_lane=0)
        xp, tail_p = _ffn(x1, sc_f, sh_f, gt_f, idx_pf, lw, conv_pre_p, g_final[None, :], tm=TM_FFN,
                          tiles_per_seq=tpb_ffn, time_major=False, final_norm=last)
        wkeep = min(WINDOW, t)
        rows_p.append((_fm_to_rows(pre["ct"], (2, NSA_DH)), _fm_to_rows(pre["st"], (2, NSA_DH)),
                       _fm_to_rows(pre["wt"][:, :, t - wkeep:], (2, NSA_DH)),
                       _fm_to_rows(pre["mt"][:, :MLA_KV_RANK + MLA_ROPE], (MLA_KV_RANK + MLA_ROPE,)),
                       _fm_to_rows(pre["ft"], (2, 1, FOX_DH)), _fm_to_rows(pre["lft"], (FOX_HEADS,)),
                       tail_p))

        pre_s = _pre_proj(xs, sc_as, sh_as, idx_s, lw, rope_s, n_groups=s_len, tiles_per_group=1,
                          n_pos_tiles=s_len, tm=db, with_keys=False)
        qn_s = q_rows(pre_s["qnt"], NSA_HEADS, NSA_DH, True)
        gates_s = jnp.transpose(pre_s["gt"][:, :3 * NSA_HEADS].reshape(s_len, NSA_HEADS, 3, db),
                                (3, 1, 0, 2)).reshape(db, NSA_HEADS * s_len, 3)
        gates_s = jnp.pad(gates_s, ((0, 0), (0, 0), (0, LANE - 3)))
        o_cmp, o_win, idx = _nsa_s1(page_table, qn_s, new_rows(pre_s["wt"]), win_t, cmp_t, l, tabs_s1,
                                    s_len=s_len, k_free=k_free)
        o_nsa_s = _nsa_s2(page_table, idx, qn_s, new_rows(pre_s["st"]), gates_s, o_cmp, o_win, sel_t, l,
                          tabs_s2, s_len=s_len, k_free=k_free)
        o_lat_s = _paged_attn(page_table, q_rows(pre_s["qmt"], MLA_HEADS, LANE, False),
                              new_rows(pre_s["mt"]), mla_t, l, heads=MLA_HEADS, s_len=s_len)
        lfn = jnp.pad(jnp.transpose(pre_s["lft"], (2, 1, 0)), ((0, 0), (0, 0), (0, NEW_PAD - s_len)))
        o_fox_s = _paged_attn(page_table, q_rows(pre_s["qft"], FOX_HEADS, FOX_DH, False),
                              new_rows(pre_s["ft"]), fkv_t, l, heads=FOX_HEADS, s_len=s_len,
                              decay=(lfn, lf_t))
        x1s = _mix(xs, sc_as, sh_as, gt_as, idx_s, lw, o_rows(o_nsa_s, NSA_HEADS, True),
                   o_rows(o_lat_s, MLA_HEADS, False), o_rows(o_fox_s, FOX_HEADS, False), tm=db,
                   value_lane=NSA_DH)
        xs, tail_s = _ffn(x1s, sc_fs, sh_fs, gt_fs, idx_s, lw, conv_pre_s[l], g_final[None, :], tm=db,
                          tiles_per_seq=s_len, time_major=True, final_norm=last)
        wfull = jnp.concatenate([win_t[l], jnp.transpose(pre_s["wt"], (2, 1, 0))], axis=2)
        wfull = wfull[:, :, wfull.shape[2] - min(WINDOW, past + s_len):]
        fm_s = lambda a, mid: jnp.swapaxes(_fm_to_rows(a, mid), 0, 1)
        rows_s.append((fm_s(pre_s["ct"], (2, NSA_DH)), fm_s(pre_s["st"], (2, NSA_DH)),
                       _fm_to_rows(wfull, (2, NSA_DH)),
                       fm_s(pre_s["mt"][:, :MLA_KV_RANK + MLA_ROPE], (MLA_KV_RANK + MLA_ROPE,)),
                       fm_s(pre_s["ft"], (2, 1, FOX_DH)), fm_s(pre_s["lft"], (FOX_HEADS,)),
                       jnp.transpose(tail_s, (1, 0, 2))))

    y_prompt = xp.reshape(b, t, d)
    y_sample = jnp.transpose(xs.reshape(s_len, db, d), (1, 0, 2))
    outs_p = [jnp.stack(a) for a in zip(*rows_p)]
    outs_s = [jnp.stack(a) for a in zip(*rows_s)]
    return (y_prompt, y_sample, *outs_p, *outs_s)
```

```python
import functools
import math

import numpy as np
import jax
import jax.numpy as jnp
from jax import lax
from jax.experimental import pallas as pl
from jax.experimental.pallas import tpu as pltpu

F32 = jnp.float32
BF16 = jnp.bfloat16

NSA_HEADS = 4
NSA_DH = 64
CMP_BLOCK = 32
SEL_BLOCK = 64
N_SEL = 16
WINDOW = 512
MLA_HEADS = 4
MLA_NOPE = 64
MLA_ROPE = 32
MLA_KV_RANK = 64
ROPE_BASE = 10000.0
FOX_HEADS = 8
FOX_DH = 64
REL_BUCKETS = 32
REL_MAX_DIST = 128
CONV_W = 3
EPS = 1e-6
NEG = -1e30
LOG2E = math.log2(math.e)
LANE = 128
VMEM_LIMIT = 56 * 1024 * 1024

NSA_SCALE = NSA_DH ** -0.5
MLA_SCALE = (MLA_NOPE + MLA_ROPE) ** -0.5
FOX_SCALE = FOX_DH ** -0.5

ONES_ROW = 64
VROWS = 80
DECAY_PARTS = 3

_NT = (((1,), (1,)), ((), ()))


def _cparams(n_axes):
    return pltpu.CompilerParams(dimension_semantics=("arbitrary",) * n_axes,
                                vmem_limit_bytes=VMEM_LIMIT)


def _dot(a, b):
    return jnp.dot(a, b, preferred_element_type=F32)


def _dot_nt(a, b):
    return lax.dot_general(a, b, _NT, preferred_element_type=F32)


def _dot_f32(a, b):
    return jnp.dot(a, b, preferred_element_type=F32, precision=lax.Precision.HIGHEST)


def _dot_terms(x, w, terms):
    out = None
    rest = x
    for _ in range(terms):
        piece = rest.astype(BF16)
        rest = rest - piece.astype(F32)
        d = _dot(piece, w)
        out = d if out is None else out + d
    return out


def _rms(x, g):
    return x * lax.rsqrt(jnp.mean(x * x, axis=-1, keepdims=True) + EPS) * g


def _log2(n):
    assert n & (n - 1) == 0, n
    return n.bit_length() - 1


def _ada_body(c_ref, w_ref, b_ref, o_ref):
    c = c_ref[...]
    sc = (c * jax.nn.sigmoid(c)).astype(BF16)
    o_ref[0] = _dot(sc, w_ref[0].astype(BF16)) + b_ref[0]


def _adaln(c_all, w_ada, b_ada):
    depth, d, n6 = w_ada.shape
    rows = c_all.shape[0]
    tn = 512
    return pl.pallas_call(
        _ada_body,
        grid=(depth, n6 // tn),
        in_specs=[pl.BlockSpec((rows, d), lambda l, n: (0, 0)),
                  pl.BlockSpec((1, d, tn), lambda l, n: (l, 0, n)),
                  pl.BlockSpec((1, 1, tn), lambda l, n: (l, 0, n))],
        out_specs=pl.BlockSpec((1, rows, tn), lambda l, n: (l, 0, n)),
        out_shape=jax.ShapeDtypeStruct((depth, rows, n6), F32),
        compiler_params=_cparams(2),
        name="adaln",
    )(c_all, w_ada, b_ada.reshape(depth, 1, n6))


_T_C, _T_S, _T_W, _T_F, _T_M, _T_LF, _T_G, _T_QD = 0, 128, 256, 384, 512, 640, 648, 664
_T_QN = _T_QD + 256
_T_QF = _T_QN + NSA_HEADS * LANE
_T_END = _T_QF + FOX_HEADS * LANE
_R_KS, _R_KW, _R_KF, _R_KM, _R_KMS, _R_C, _R_LF, _R_END = 0, 128, 256, 384, 512, 640, 768, 896


def _value_tile(vt):
    n = vt.shape[1]
    return jnp.concatenate([vt, jnp.ones((VROWS - ONES_ROW, n), F32)], axis=0).astype(BF16)


def _pre_body(x_ref, sc_ref, sh_ref, g_ref, wrow_ref, wt_ref, gq_ref, wuq_ref, wuk_ref, pp_ref,
              gkv_ref, gkvr_ref, bf_ref, bfr_ref, aug_ref, cs1_ref, cs2_ref, csq1_ref, csq2_ref,
              cr1_ref, cr2_ref, tril_ref, avg_ref,
              ct_ref, st_ref, wtt_ref, ft_ref, mt_ref, lf_ref, gt_ref, qn_ref, qf_ref, qm_ref,
              vs_ref, vw_ref, vf_ref, vm_ref, ks_ref, kw_ref, kf_ref, km_ref, kvc_ref,
              carry_ref, *, tiles_per_seq, with_keys):
    x = x_ref[...]
    tm = x.shape[0]
    h = _rms(x, g_ref[...]) * (1.0 + sc_ref[0]) + sh_ref[0]
    hb = h.astype(BF16)

    pt = _dot_nt(wt_ref[...], hb)
    ct_ref[0] = pt[_T_C:_T_C + 128]
    st_ref[0] = pt[_T_S:_T_S + 128]
    wtt_ref[0] = pt[_T_W:_T_W + 128]
    ft_ref[0] = pt[_T_F:_T_F + 128]
    ckv = pt[_T_M:_T_M + 64]
    ckv = ckv * lax.rsqrt(jnp.mean(ckv * ckv, axis=0, keepdims=True) + EPS) * gkv_ref[...]
    krot = pt[_T_M + 64:_T_M + 96] * cs1_ref[...] + pt[_T_M + 96:_T_M + 128] * cs2_ref[...]
    mt_ref[0, 0:64] = ckv
    mt_ref[0, 64:96] = krot
    mt_ref[0, 96:128] = jnp.zeros_like(krot)
    lf_ref[0] = jax.nn.log_sigmoid(pt[_T_LF:_T_LF + 8] + bf_ref[...])
    gt_ref[0] = jax.nn.sigmoid(pt[_T_G:_T_G + 16])
    qn_ref[0] = pt[_T_QN:_T_QF].astype(BF16)
    qf_ref[0] = (pt[_T_QF:_T_END] + aug_ref[...]).astype(BF16)
    qd = pt[_T_QD:_T_QD + 256]
    qd = qd * lax.rsqrt(jnp.mean(qd * qd, axis=0, keepdims=True) + EPS) * gq_ref[...]
    q = _dot(wuq_ref[...], qd.astype(BF16))
    qrot = q[256:384] * csq1_ref[...] + q[384:512] * csq2_ref[...]
    qm = _dot(wuk_ref[...], q[:256].astype(BF16)) * (MLA_SCALE * LOG2E)
    qm = qm + _dot(pp_ref[...], (qrot * (MLA_SCALE * LOG2E)).astype(BF16))
    qm_ref[0] = qm.astype(BF16)
    vs_ref[0] = _value_tile(pt[_T_S + 64:_T_S + 128])
    vw_ref[0] = _value_tile(pt[_T_W + 64:_T_W + 128])
    vf_ref[0] = _value_tile(pt[_T_F + 64:_T_F + 128])
    vm_ref[0] = _value_tile(ckv)

    if with_keys:
        pr = _dot(hb, wrow_ref[...])
        lane = lax.broadcasted_iota(jnp.int32, (tm, LANE), 1)
        t_in_seq = (pl.program_id(0) % tiles_per_seq) * tm
        pos = t_in_seq + lax.broadcasted_iota(jnp.int32, (tm, LANE), 0)
        blk_ind = jnp.where(lane == jnp.right_shift(pos, _log2(SEL_BLOCK)), 1.0, 0.0)
        ks_ref[:, 0:LANE] = pr[:, _R_KS:_R_KS + LANE].astype(BF16)
        ks_ref[:, LANE:2 * LANE] = blk_ind.astype(BF16)
        kw_ref[...] = pr[:, _R_KW:_R_KW + LANE].astype(BF16)
        kvd = pr[:, _R_KM:_R_KM + LANE]
        is_c = lane < MLA_KV_RANK
        ms = jnp.sum(jnp.where(is_c, kvd * kvd, 0.0), axis=1, keepdims=True) / MLA_KV_RANK
        km = jnp.where(is_c, kvd * lax.rsqrt(ms + EPS) * gkvr_ref[...],
                       kvd * cr1_ref[...] + pr[:, _R_KMS:_R_KMS + LANE] * cr2_ref[...])
        km_ref[...] = km.astype(BF16)
        @pl.when(pl.program_id(0) % tiles_per_seq == 0)
        def _():
            carry_ref[...] = jnp.zeros_like(carry_ref)
        lfr = jnp.where(lane < FOX_HEADS, jax.nn.log_sigmoid(pr[:, _R_LF:_R_LF + LANE] + bfr_ref[...]),
                        0.0)
        csum = _dot_f32(tril_ref[...], lfr) + carry_ref[0:1]
        carry_ref[...] = jnp.broadcast_to(csum[tm - 1:], carry_ref.shape)
        kf = pr[:, _R_KF:_R_KF + LANE]
        rest = -csum * LOG2E
        for part in range(DECAY_PARTS):
            term = rest.astype(BF16).astype(F32)
            rest = rest - term
            kf = kf + pltpu.roll(term, FOX_DH + part * FOX_HEADS, 1)
        kf_ref[...] = kf.astype(BF16)
        kvc_ref[...] = _dot_f32(avg_ref[...], pr[:, _R_C:_R_C + LANE])
    else:
        ks_ref[...] = jnp.zeros(ks_ref.shape, BF16)
        kw_ref[...] = jnp.zeros(kw_ref.shape, BF16)
        kf_ref[...] = jnp.zeros(kf_ref.shape, BF16)
        km_ref[...] = jnp.zeros(km_ref.shape, BF16)
        kvc_ref[...] = jnp.zeros(kvc_ref.shape, F32)


def _pre_proj(x, sc, sh, mod_index, lw, pos_tabs, *, n_groups, tiles_per_group, n_pos_tiles, tm,
              with_keys):
    n, d = x.shape
    cs1, cs2, csq1, csq2, cr1, cr2 = pos_tabs
    n_tiles = n // tm
    tpg = tiles_per_group
    tril = np.tril(np.ones((tm, tm), np.float32))
    nmean = 8
    avg = (np.arange(tm)[None, :] // CMP_BLOCK == np.arange(nmean)[:, None]).astype(np.float32) / CMP_BLOCK
    row = lambda i: (i, 0)
    const2 = lambda i: (0, 0)
    tcol = lambda i: (i // tpg, 0, i % tpg)
    pcol = lambda i: (0, i % n_pos_tiles)
    prow = lambda i: (i % n_pos_tiles, 0)
    ncols = tpg * tm

    def tspec(r):
        return pl.BlockSpec((1, r, tm), tcol)

    def tshape(r, dt=F32):
        return jax.ShapeDtypeStruct((n_groups, r, ncols), dt)

    nq = NSA_HEADS * LANE
    nf = FOX_HEADS * LANE
    consts = [lw["g_attn"], lw["w_row"], lw["w_t"], lw["g_q"], lw["w_uq"], lw["w_uk"], lw["pp"],
              lw["g_kv"], lw["g_kv_row"], lw["b_f"], lw["b_f_row"], lw["q_aug"]]
    outs = pl.pallas_call(
        functools.partial(_pre_body, tiles_per_seq=tpg, with_keys=with_keys),
        grid=(n_tiles,),
        in_specs=[pl.BlockSpec((tm, d), row),
                  pl.BlockSpec((1,) + sc.shape[1:], mod_index),
                  pl.BlockSpec((1,) + sh.shape[1:], mod_index)]
        + [pl.BlockSpec(c.shape, const2) for c in consts]
        + [pl.BlockSpec((32, tm), pcol), pl.BlockSpec((32, tm), pcol),
           pl.BlockSpec((128, tm), pcol), pl.BlockSpec((128, tm), pcol),
           pl.BlockSpec((tm, 128), prow), pl.BlockSpec((tm, 128), prow),
           pl.BlockSpec((tm, tm), const2), pl.BlockSpec((nmean, tm), const2)],
        out_specs=[tspec(128), tspec(128), tspec(128), tspec(128), tspec(128), tspec(8), tspec(16),
                   tspec(nq), tspec(nf), tspec(nq), tspec(VROWS), tspec(VROWS), tspec(VROWS), tspec(VROWS),
                   pl.BlockSpec((tm, 2 * LANE), row), pl.BlockSpec((tm, LANE), row),
                   pl.BlockSpec((tm, LANE), row), pl.BlockSpec((tm, LANE), row),
                   pl.BlockSpec((nmean, LANE), row)],
        out_shape=[tshape(128), tshape(128), tshape(128), tshape(128), tshape(128), tshape(8),
                   tshape(16), tshape(nq, BF16), tshape(nf, BF16), tshape(nq, BF16),
                   tshape(VROWS, BF16), tshape(VROWS, BF16), tshape(VROWS, BF16), tshape(VROWS, BF16),
                   jax.ShapeDtypeStruct((n, 2 * LANE), BF16), jax.ShapeDtypeStruct((n, LANE), BF16),
                   jax.ShapeDtypeStruct((n, LANE), BF16), jax.ShapeDtypeStruct((n, LANE), BF16),
                   jax.ShapeDtypeStruct((n_tiles * nmean, LANE), F32)],
        scratch_shapes=[pltpu.VMEM((8, LANE), F32)],
        compiler_params=_cparams(1),
        name="pre_proj",
    )(x, sc, sh, *consts, cs1, cs2, csq1, csq2, cr1, cr2, tril, avg)
    keys = ("ct", "st", "wt", "ft", "mt", "lft", "gt", "qnt", "qft", "qmt", "vst", "vwt", "vft", "vmt",
            "ks", "kw", "kf", "km", "kvc")
    return dict(zip(keys, outs))


def _tflash_init(m_ref, acc_ref):
    m_ref[...] = jnp.full(m_ref.shape, NEG, F32)
    acc_ref[...] = jnp.zeros(acc_ref.shape, F32)


def _tflash_update(s, vt, m_ref, acc_ref):
    _tflash_update_many([(s, vt)], m_ref, acc_ref)


def _tflash_update_many(tiles, m_ref, acc_ref):
    m_old = m_ref[...]
    parts = []
    m_new = m_old
    for s, vt in tiles:
        mk = jnp.max(s, axis=0, keepdims=True)
        parts.append((mk, _dot(vt, jnp.exp2(s - mk).astype(BF16))))
        m_new = jnp.maximum(m_new, mk)
    acc = jnp.exp2(m_old - m_new) * acc_ref[...]
    for mk, pv in parts:
        acc = acc + jnp.exp2(mk - m_new) * pv
    acc_ref[...] = acc
    m_ref[...] = m_new


def _tflash_out(acc, heads, tq):
    o = acc[:ONES_ROW] * (1.0 / acc[ONES_ROW:ONES_ROW + 1])
    o = jnp.concatenate([o, jnp.zeros_like(o)], axis=0)
    return jnp.concatenate([o[:, h * tq:(h + 1) * tq].T for h in range(heads)], axis=1).astype(BF16)


def _lane_stack(q_ref, heads):
    return jnp.concatenate([q_ref[0, h * LANE:(h + 1) * LANE, :] for h in range(heads)], axis=1)


def _causal_body(q_ref, k_ref, v_ref, o_ref, m_ref, acc_ref, *, heads, tq, tk, group):
    i = pl.program_id(1)
    q0 = i * tq
    qt = _lane_stack(q_ref, heads)
    _tflash_init(m_ref, acc_ref)

    def tile(c0):
        return _dot(k_ref[0, pl.ds(c0, tk), :], qt), v_ref[0, :, pl.ds(c0, tk)]

    def group_step(j, carry):
        c0 = pl.multiple_of(j * group * tk, group * tk)
        _tflash_update_many([tile(c0 + k * tk) for k in range(group)], m_ref, acc_ref)
        return carry

    def full_step(j, carry):
        s, vt = tile(pl.multiple_of(j * tk, tk))
        _tflash_update(s, vt, m_ref, acc_ref)
        return carry

    n_full = q0 // tk
    n_groups = n_full // group
    lax.fori_loop(0, n_groups, group_step, 0)
    done = group * n_groups
    n_pairs = (n_full - done) // 2

    def pair_step(j, carry):
        c0 = pl.multiple_of((done + 2 * j) * tk, tk)
        _tflash_update_many([tile(c0), tile(c0 + tk)], m_ref, acc_ref)
        return carry

    lax.fori_loop(0, n_pairs, pair_step, 0)
    lax.fori_loop(done + 2 * n_pairs, n_full, full_step, 0)
    c0 = pl.multiple_of(n_full * tk, tk)
    s, vt = tile(c0)
    key = c0 + lax.broadcasted_iota(jnp.int32, s.shape, 0)
    qpos = q0 + jnp.bitwise_and(lax.broadcasted_iota(jnp.int32, s.shape, 1), tq - 1)
    _tflash_update(jnp.where(key <= qpos, s, NEG), vt, m_ref, acc_ref)
    o_ref[...] = _tflash_out(acc_ref[...], heads, tq)


def _causal_attn(qt, k, vt, *, heads, tq, tk, group):
    b, _, t = qt.shape
    nq = t // tq
    assert tq & (tq - 1) == 0 and tk % tq == 0 and t % tk == 0
    m = heads * tq
    return pl.pallas_call(
        functools.partial(_causal_body, heads=heads, tq=tq, tk=tk, group=group),
        grid=(b, nq),
        in_specs=[pl.BlockSpec((1, heads * LANE, tq), lambda bi, i: (bi, 0, i)),
                  pl.BlockSpec((1, t, LANE), lambda bi, i: (bi, 0, 0)),
                  pl.BlockSpec((1, VROWS, t), lambda bi, i: (bi, 0, 0))],
        out_specs=pl.BlockSpec((tq, heads * LANE), lambda bi, i: (bi * nq + i, 0)),
        out_shape=jax.ShapeDtypeStruct((b * t, heads * LANE), BF16),
        scratch_shapes=[pltpu.VMEM((1, m), F32), pltpu.VMEM((VROWS, m), F32)],
        compiler_params=_cparams(2),
        name="causal_attn_h%d" % heads,
    )(qt, k, vt)


def _top_k_neg_mask_t(score, k, out):
    n = score.shape[0]
    rowi = lax.broadcasted_iota(jnp.int32, score.shape, 0).astype(F32)
    work = score
    for _ in range(k):
        mx = jnp.max(work, axis=0, keepdims=True)
        idx = jnp.min(jnp.where(work == mx, rowi, float(n)), axis=0, keepdims=True)
        hit = rowi == idx
        out = jnp.where(hit, 0.0, out)
        work = jnp.where(hit, -jnp.inf, work)
    return out


def _nsa_body(q_ref, kvc_ref, kvct_ref, ks_ref, vs_ref, kw_ref, vw_ref, g_ref, cb_ref, sb_ref, wb_ref,
              o_ref, m_ref, acc_ref, m2_ref, acc2_ref, *, tq, tk, pad):
    hh = NSA_HEADS
    i = pl.program_id(1)
    q0 = i * tq
    qt = _lane_stack(q_ref, hh)
    cb = jnp.concatenate([cb_ref[h] for h in range(hh)], axis=1)
    s = _dot(kvc_ref[0], qt) + cb
    e = jnp.exp2(s - jnp.max(s, axis=0, keepdims=True))
    p = e * (1.0 / jnp.sum(e, axis=0, keepdims=True))
    p = jnp.where(cb > 0.5 * NEG, p, 0.0)
    o_cmp = _dot(kvct_ref[0], p.astype(BF16))
    nc = p.shape[0]
    pc = p[:, 0:tq]
    for h in range(1, hh):
        pc = pc + p[:, h * tq:(h + 1) * tq]
    n_sel = nc // 2
    score = pc[:n_sel] + pc[n_sel:]
    blk = lax.broadcasted_iota(jnp.int32, (n_sel, tq), 0)
    qpos = q0 + lax.broadcasted_iota(jnp.int32, (n_sel, tq), 1)
    cur = jnp.right_shift(qpos, _log2(SEL_BLOCK))
    forced = (blk == 0) | (blk == cur) | (blk == cur - 1)
    future = blk * SEL_BLOCK > qpos
    score = jnp.where(forced, -jnp.inf, jnp.where(future, -1e6, score))
    selneg = _top_k_neg_mask_t(score, min(N_SEL, n_sel) - 3, jnp.where(forced, 0.0, NEG)).astype(BF16)
    if n_sel < LANE:
        selneg = jnp.concatenate([selneg, jnp.zeros((LANE - n_sel, tq), BF16)], axis=0)
    qa = jnp.concatenate([qt, jnp.concatenate([selneg] * hh, axis=1)], axis=0)

    _tflash_init(m_ref, acc_ref)

    def far_tile(c0, n):
        return _dot(ks_ref[0, pl.ds(pad + c0, n), :], qa), vs_ref[0, :, pl.ds(pad + c0, n)]

    def far(c0, n):
        _tflash_update_many([far_tile(c0 + k * tk, min(n, tk)) for k in range(max(n // tk, 1))],
                            m_ref, acc_ref)

    n_far = jnp.maximum(i - 1, 0)
    per = GROUP * tk // tq
    n_big = n_far // per

    def big_step(j, carry):
        far(pl.multiple_of(j * GROUP * tk, GROUP * tk), GROUP * tk)
        return carry

    def small_step(j, carry):
        far(pl.multiple_of(j * tq, tq), tq)
        return carry

    lax.fori_loop(0, n_big, big_step, 0)
    lax.fori_loop(n_big * per, n_far, small_step, 0)
    c0 = pl.multiple_of(q0 - tq, tq)
    s_n = _dot(ks_ref[0, pl.ds(pad + c0, 2 * tq), :], qa) + sb_ref[...]
    key = c0 + lax.broadcasted_iota(jnp.int32, s_n.shape, 0)
    _tflash_update(jnp.where(key >= 0, s_n, NEG), vs_ref[0, :, pl.ds(pad + c0, 2 * tq)], m_ref, acc_ref)
    acc_s = acc_ref[...]
    o_sel = acc_s[:NSA_DH] * (1.0 / acc_s[ONES_ROW:ONES_ROW + 1])

    c0 = pl.multiple_of(q0 - WINDOW, tq)
    nw = WINDOW + tq
    s_w = _dot(kw_ref[0, pl.ds(pad + c0, nw), :], qt) + wb_ref[...]
    key = c0 + lax.broadcasted_iota(jnp.int32, s_w.shape, 0)
    _tflash_init(m2_ref, acc2_ref)
    _tflash_update(jnp.where(key >= 0, s_w, NEG), vw_ref[0, :, pl.ds(pad + c0, nw)], m2_ref, acc2_ref)
    acc_w = acc2_ref[...]
    o_win = acc_w[:NSA_DH] * (1.0 / acc_w[ONES_ROW:ONES_ROW + 1])

    g = g_ref[0]
    outs = []
    for h in range(hh):
        sl = slice(h * tq, (h + 1) * tq)
        o_h = (g[3 * h:3 * h + 1] * o_cmp[NSA_DH:, sl] + g[3 * h + 1:3 * h + 2] * o_sel[:, sl]
               + g[3 * h + 2:3 * h + 3] * o_win[:, sl])
        outs.append(jnp.concatenate([o_h, jnp.zeros_like(o_h)], axis=0).T)
    o_ref[...] = jnp.concatenate(outs, axis=1).astype(BF16)


def _bucket_lower_bounds():
    d = np.arange(REL_MAX_DIST + 1)
    exact = REL_BUCKETS // 2
    scaled = np.log(np.maximum(d, 1) / exact) / math.log(REL_MAX_DIST / exact)
    large = np.minimum(exact + (scaled * (REL_BUCKETS - exact)).astype(np.int64), REL_BUCKETS - 1)
    bucket = np.where(d < exact, d, large)
    assert np.all(np.diff(bucket) >= 0) and bucket[-1] == REL_BUCKETS - 1
    return [int(np.argmax(bucket >= k)) for k in range(1, REL_BUCKETS)]


_BUCKET_LO = _bucket_lower_bounds()


def _rel_bias_t(table, dist):
    tab = table.astype(F32)
    extra = (None,) * dist.ndim
    col = lambda k: tab[k][(slice(None),) + extra]
    out = jnp.broadcast_to(col(0), (tab.shape[1],) + dist.shape)
    for k, lo in enumerate(_BUCKET_LO, start=1):
        out = jnp.where((dist >= lo)[None], col(k), out)
    return out


def _head_lanes(a):
    return jnp.concatenate([a[h] for h in range(a.shape[0])], axis=1)


def _nsa_prompt_tables(rel_table, t, tq):
    far = rel_table[REL_BUCKETS - 1].astype(F32)[:, None, None]
    nc = t // CMP_BLOCK
    order = jnp.concatenate([jnp.arange(0, nc, 2), jnp.arange(1, nc, 2)])
    c_end = order * CMP_BLOCK + CMP_BLOCK - 1
    dist = jnp.arange(t)[None, :] - c_end[:, None]
    cb = jnp.where(dist >= 0, _rel_bias_t(rel_table, dist) * LOG2E, NEG)
    dist = jnp.arange(tq)[None, :] + tq - jnp.arange(2 * tq)[:, None]
    sb = jnp.where(dist >= 0, (_rel_bias_t(rel_table, dist) - far) * LOG2E, NEG)
    dist = jnp.arange(tq)[None, :] + WINDOW - jnp.arange(WINDOW + tq)[:, None]
    wb = jnp.where((dist >= 0) & (dist < WINDOW), _rel_bias_t(rel_table, dist) * LOG2E, NEG)
    return cb, _head_lanes(sb), _head_lanes(wb)


def _nsa_prompt(qnt, kvc, kvct, ks, vst, kw, vwt, gt, tables, *, tq, tk, pad):
    b, _, t = qnt.shape
    nq = t // tq
    cb, sb, wb = tables
    nc = kvc.shape[1]
    hh = NSA_HEADS
    m = hh * tq
    per_b = lambda bi, i: (bi, 0, 0)
    tile = lambda bi, i: (bi, 0, i)
    return pl.pallas_call(
        functools.partial(_nsa_body, tq=tq, tk=tk, pad=pad),
        grid=(b, nq),
        in_specs=[pl.BlockSpec((1, hh * LANE, tq), tile),
                  pl.BlockSpec((1, nc, LANE), per_b),
                  pl.BlockSpec((1, LANE, nc), per_b),
                  pl.BlockSpec((1,) + ks.shape[1:], per_b),
                  pl.BlockSpec((1,) + vst.shape[1:], per_b),
                  pl.BlockSpec((1,) + kw.shape[1:], per_b),
                  pl.BlockSpec((1,) + vwt.shape[1:], per_b),
                  pl.BlockSpec((1, 16, tq), tile),
                  pl.BlockSpec((hh, nc, tq), lambda bi, i: (0, 0, i)),
                  pl.BlockSpec(sb.shape, lambda bi, i: (0, 0)),
                  pl.BlockSpec(wb.shape, lambda bi, i: (0, 0))],
        out_specs=pl.BlockSpec((tq, hh * LANE), lambda bi, i: (bi * nq + i, 0)),
        out_shape=jax.ShapeDtypeStruct((b * t, hh * LANE), BF16),
        scratch_shapes=[pltpu.VMEM((1, m), F32), pltpu.VMEM((VROWS, m), F32),
                        pltpu.VMEM((1, m), F32), pltpu.VMEM((VROWS, m), F32)],
        compiler_params=_cparams(2),
        name="nsa_prompt",
    )(qnt, kvc, kvct, ks, vst, kw, vwt, gt, cb, sb, wb)


def _mix_body(x_ref, sc_ref, sh_ref, gt_ref, g_ref, on_ref, om_ref, of_ref, wmg_ref, wbn_ref,
              wuv_ref, wbm_ref, wbf_ref, wo_ref, o_ref):
    x = x_ref[...]
    d = x.shape[1]
    h = _rms(x, g_ref[...]) * (1.0 + sc_ref[0]) + sh_ref[0]
    mg = jax.nn.sigmoid(_dot(h.astype(BF16), wmg_ref[...]))
    o_mla = _dot(om_ref[...], wuv_ref[...]).astype(BF16)
    t = mg[:, :d] * _dot(on_ref[...], wbn_ref[...])
    t = t + mg[:, d:2 * d] * _dot(o_mla, wbm_ref[...])
    t = t + mg[:, 2 * d:] * _dot(of_ref[...], wbf_ref[...])
    o_ref[...] = x + gt_ref[0] * _dot(t.astype(BF16), wo_ref[...])


def _mix(x, sc, sh, gt, mod_index, lw, o_nsa, o_lat, o_fox, *, tm, value_lane):
    n, d = x.shape
    row = lambda i: (i, 0)
    const2 = lambda i: (0, 0)
    ws = [lw["w_mg"], lw["w_br_n"][value_lane], lw["w_uv"], lw["w_br_m"], lw["w_br_f"][value_lane],
          lw["w_o"]]
    return pl.pallas_call(
        _mix_body,
        grid=(n // tm,),
        in_specs=[pl.BlockSpec((tm, d), row)]
        + [pl.BlockSpec((1,) + a.shape[1:], mod_index) for a in (sc, sh, gt)]
        + [pl.BlockSpec((1, d), const2)]
        + [pl.BlockSpec((tm, a.shape[1]), row) for a in (o_nsa, o_lat, o_fox)]
        + [pl.BlockSpec(w.shape, const2) for w in ws],
        out_specs=pl.BlockSpec((tm, d), row),
        out_shape=jax.ShapeDtypeStruct((n, d), F32),
        compiler_params=_cparams(1),
        name="mix",
    )(x, sc, sh, gt, lw["g_attn"], o_nsa, o_lat, o_fox, *ws)


def _gelu_tanh(x):
    return 0.5 * x * (1.0 + jnp.tanh(math.sqrt(2.0 / math.pi) * (x + 0.044715 * (x * x * x))))


def _ffn_body(x_ref, sc_ref, sh_ref, gt_ref, g_ref, pre_ref, win_ref, cw_ref, cb_ref, wout_ref,
              gf_ref, o_ref, tail_ref, hist_ref, *, tiles_per_seq, time_major, final_norm, d_ff):
    x = x_ref[...]
    tm = x.shape[0]
    i = pl.program_id(0)
    h = _rms(x, g_ref[...]) * (1.0 + sc_ref[0]) + sh_ref[0]
    ab = _dot(h.astype(BF16), win_ref[...])
    a = ab[:, :d_ff]
    b = ab[:, d_ff:]
    cw = cw_ref[...]
    if time_major:
        @pl.when(i % tiles_per_seq == 0)
        def _():
            hist_ref[0] = pre_ref[0]
            hist_ref[1] = pre_ref[1]
        a2 = hist_ref[0]
        a1 = hist_ref[1]
        hist_ref[0] = a1
        hist_ref[1] = a
        tail_ref[0] = a
    else:
        @pl.when(i % tiles_per_seq == 0)
        def _():
            hist_ref[0, 6:8] = pre_ref[0]
        prev = hist_ref[0, 6:8]
        rid = lax.broadcasted_iota(jnp.int32, (tm, 1), 0)
        a1 = jnp.where(rid == 0, prev[1:2], pltpu.roll(a, 1, 0))
        a2 = jnp.where(rid == 0, prev[0:1], jnp.where(rid == 1, prev[1:2], pltpu.roll(a, 2, 0)))
        hist_ref[0] = a[tm - 8:]
        tail_ref[0] = a[tm - 2:]
    conv = a2 * cw[0:1] + a1 * cw[1:2] + a * cw[2:3] + cb_ref[...]
    y = _dot((_gelu_tanh(conv) * b).astype(BF16), wout_ref[...])
    out = x + gt_ref[0] * y
    if final_norm:
        out = _rms(out, gf_ref[...])
    o_ref[...] = out


def _ffn(x, sc, sh, gt, mod_index, lw, prefix, g_final, *, tm, tiles_per_seq, time_major, final_norm):
    n, d = x.shape
    d_ff = lw["w_ffn_out"].shape[0]
    row = lambda i: (i, 0)
    const2 = lambda i: (0, 0)
    n_tiles = n // tm
    if time_major:
        pre_spec = pl.BlockSpec(prefix.shape, lambda i: (0, 0, 0))
        tail_spec = pl.BlockSpec((1, tm, d_ff),
                                 lambda i: (jnp.maximum(i - (tiles_per_seq - 2), 0), 0, 0))
        tail_shape = jax.ShapeDtypeStruct((2, tm, d_ff), F32)
        hist = pltpu.VMEM((2, tm, d_ff), F32)
    else:
        pre_spec = pl.BlockSpec((1, 2, d_ff), lambda i: (i // tiles_per_seq, 0, 0))
        tail_spec = pl.BlockSpec((1, 2, d_ff), lambda i: (i // tiles_per_seq, 0, 0))
        tail_shape = jax.ShapeDtypeStruct((n_tiles // tiles_per_seq, 2, d_ff), F32)
        hist = pltpu.VMEM((1, 8, d_ff), F32)
    return pl.pallas_call(
        functools.partial(_ffn_body, tiles_per_seq=tiles_per_seq, time_major=time_major,
                          final_norm=final_norm, d_ff=d_ff),
        grid=(n_tiles,),
        in_specs=[pl.BlockSpec((tm, d), row)]
        + [pl.BlockSpec((1,) + a.shape[1:], mod_index) for a in (sc, sh, gt)]
        + [pl.BlockSpec((1, d), const2), pre_spec,
           pl.BlockSpec(lw["w_ffn_in"].shape, const2),
           pl.BlockSpec((CONV_W, d_ff), const2), pl.BlockSpec((1, d_ff), const2),
           pl.BlockSpec(lw["w_ffn_out"].shape, const2), pl.BlockSpec((1, d), const2)],
        out_specs=[pl.BlockSpec((tm, d), row), tail_spec],
        out_shape=[jax.ShapeDtypeStruct((n, d), F32), tail_shape],
        scratch_shapes=[hist],
        compiler_params=_cparams(1),
        name="conv_ffn",
    )(x, sc, sh, gt, lw["g_ffn"], prefix, lw["w_ffn_in"], lw["conv_w"], lw["conv_b"],
      lw["w_ffn_out"], g_final)


def _pad_heads(w, heads, dh, scale=1.0):
    k = w.shape[0]
    w = (w * scale).reshape(k, heads, dh)
    return jnp.pad(w, ((0, 0), (0, 0), (0, LANE - dh))).reshape(k, heads * LANE)


def _pad_head_rows(w, heads, dh, offset):
    n = w.shape[1]
    w = w.reshape(heads, dh, n)
    return jnp.pad(w, ((0, 0), (offset, LANE - dh - offset), (0, 0))).reshape(heads * LANE, n)


def _pad_cols(w, n=LANE):
    return jnp.pad(w, ((0, 0), (0, n - w.shape[1])))


def _prep_layer(l, p):
    d = p["w_in"].shape[1]
    w_in = p["w_in"][l]
    nsa_w = NSA_HEADS * NSA_DH
    fox_w = FOX_HEADS * FOX_DH
    q_rank = p["mla_g_q"].shape[1]
    kv_w = MLA_KV_RANK + MLA_ROPE
    splits = (nsa_w, 2 * NSA_DH, 2 * NSA_DH, 2 * NSA_DH, 3 * NSA_HEADS, q_rank, kv_w,
              fox_w, 2 * FOX_DH, FOX_HEADS, 3 * d)
    cuts = [int(c) for c in np.cumsum(splits)[:-1]]
    (w_nq, w_nc, w_ns, w_nw, w_ng, w_qd, w_kvd, w_fq, w_fkv, w_ff, w_mg) = jnp.split(w_in, cuts, axis=1)
    half = MLA_ROPE // 2
    w_kr = w_kvd[:, MLA_KV_RANK:]
    w_kr_sw = jnp.concatenate([w_kr[:, half:], w_kr[:, :half]], axis=1)
    w_t = jnp.concatenate([w_nc, w_ns, w_nw, w_fkv, w_kvd, w_kr_sw, w_ff, _pad_cols(w_ng, 16), w_qd,
                           _pad_heads(w_nq, NSA_HEADS, NSA_DH, NSA_SCALE * LOG2E),
                           _pad_heads(w_fq, FOX_HEADS, FOX_DH, FOX_SCALE * LOG2E)], axis=1)
    assert w_t.shape[1] == _T_END
    w_row = jnp.concatenate([_pad_cols(w_ns[:, :NSA_DH]), _pad_cols(w_nw[:, :NSA_DH]),
                             _pad_cols(w_fkv[:, :FOX_DH]), _pad_cols(w_kvd),
                             _pad_cols(jnp.pad(w_kr_sw, ((0, 0), (MLA_KV_RANK, 0)))),
                             w_nc, _pad_cols(w_ff)], axis=1)
    assert w_row.shape[1] == _R_END
    aug = np.zeros((FOX_HEADS, LANE), np.float32)
    for h in range(FOX_HEADS):
        for part in range(DECAY_PARTS):
            aug[h, FOX_DH + part * FOX_HEADS + h] = 1.0
    w_uq = p["mla_w_uq"][l].reshape(q_rank, MLA_HEADS, MLA_NOPE + MLA_ROPE)
    uq_nope = w_uq[:, :, :MLA_NOPE].reshape(q_rank, -1)
    uq_rope = w_uq[:, :, MLA_NOPE:]
    uq_rope_sw = jnp.concatenate([uq_rope[:, :, half:], uq_rope[:, :, :half]], axis=2)
    w_uq2 = jnp.concatenate([uq_nope, uq_rope.reshape(q_rank, -1), uq_rope_sw.reshape(q_rank, -1)],
                            axis=1)
    w_uk = p["mla_w_uk"][l]
    eye_h = np.eye(MLA_HEADS, dtype=np.float32)
    uk_bd = jnp.einsum("chd,hg->hdgc", w_uk, eye_h)
    uk_bd = jnp.pad(uk_bd, ((0, 0), (0, 0), (0, 0), (0, LANE - MLA_KV_RANK)))
    uk_bd = uk_bd.reshape(MLA_HEADS * MLA_NOPE, MLA_HEADS * LANE)
    rr = np.arange(MLA_HEADS * MLA_ROPE)
    pp = (np.arange(MLA_HEADS * LANE)[None, :]
          == ((rr // MLA_ROPE) * LANE + MLA_KV_RANK + rr % MLA_ROPE)[:, None]).astype(np.float32)
    w_uv = p["mla_w_uv"][l]
    uv_bd = jnp.einsum("chv,hg->hcgv", w_uv, eye_h)
    uv_bd = jnp.pad(uv_bd, ((0, 0), (0, LANE - MLA_KV_RANK), (0, 0), (0, 0)))
    uv_bd = uv_bd.reshape(MLA_HEADS * LANE, -1).astype(BF16)
    w_br = p["w_br"][l]
    mla_w = w_uv.shape[1] * w_uv.shape[2]
    br_n, br_f = w_br[:nsa_w], w_br[nsa_w + mla_w:]
    g_kv = p["mla_g_kv"][l]
    b_f = p["fox_b_f"][l]
    return {
        "g_attn": p["g_attn"][l][None, :], "g_ffn": p["g_ffn"][l][None, :],
        "w_row": w_row.astype(BF16), "w_t": w_t.T.astype(BF16), "g_q": p["mla_g_q"][l][:, None],
        "w_uq": w_uq2.T.astype(BF16), "w_uk": uk_bd.T.astype(BF16), "pp": pp.T.astype(BF16),
        "g_kv": g_kv[:, None], "g_kv_row": _pad_cols(g_kv[None, :]),
        "b_f": b_f[:, None], "b_f_row": _pad_cols(b_f[None, :]),
        "q_aug": jnp.asarray(aug.reshape(FOX_HEADS * LANE, 1)),
        "w_mg": w_mg.astype(BF16),
        "w_br_n": {o: _pad_head_rows(br_n, NSA_HEADS, NSA_DH, o).astype(BF16) for o in (0, NSA_DH)},
        "w_uv": uv_bd,
        "w_br_m": w_br[nsa_w:nsa_w + mla_w].astype(BF16),
        "w_br_f": {o: _pad_head_rows(br_f, FOX_HEADS, FOX_DH, o).astype(BF16) for o in (0, FOX_DH)},
        "w_o": p["w_o"][l].astype(BF16),
        "w_ffn_in": p["w_ffn_in"][l].astype(BF16), "conv_w": p["conv_w"][l],
        "conv_b": p["conv_b"][l][None, :], "w_ffn_out": p["w_ffn_out"][l].astype(BF16),
    }


def _rope_tables(pos):
    half = MLA_ROPE // 2
    inv = ROPE_BASE ** (-jnp.arange(half, dtype=F32) / half)
    ang = pos.astype(F32)[:, None] * inv[None, :]
    cos, sin = jnp.cos(ang), jnp.sin(ang)
    c1 = jnp.concatenate([cos, cos], axis=1)
    c2 = jnp.concatenate([-sin, sin], axis=1)
    padr = ((0, 0), (MLA_KV_RANK, LANE - MLA_KV_RANK - MLA_ROPE))
    return (c1.T, c2.T, jnp.tile(c1, (1, MLA_HEADS)).T, jnp.tile(c2, (1, MLA_HEADS)).T,
            jnp.pad(c1, padr), jnp.pad(c2, padr))


def _page_copy(cache_ref, layer, page, buf, slot, p, rows, sem):
    dst = buf.at[slot, pl.ds(0, rows), pl.ds(pl.multiple_of(p * LANE, LANE), LANE)]
    return pltpu.make_async_copy(cache_ref.at[layer, page], dst, sem)


def _wait_all(buf, sem):
    pltpu.make_async_copy(buf, buf, sem).wait()


def _merge_partials(parts):
    m = parts[0][0]
    for mc, _, _ in parts[1:]:
        m = jnp.maximum(m, mc)
    l = acc = None
    for mc, lc, ac in parts:
        w = jnp.exp2(mc - m)
        l = w * lc if l is None else l + w * lc
        acc = w * ac if acc is None else acc + w * ac
    return acc / l


def _softmax_parts(parts):
    m = parts[0].max(axis=-1, keepdims=True)
    for s in parts[1:]:
        m = jnp.maximum(m, s.max(axis=-1, keepdims=True))
    ps = [jnp.exp2(s - m) for s in parts]
    l = ps[0].sum(axis=-1, keepdims=True)
    for p in ps[1:]:
        l = l + p.sum(axis=-1, keepdims=True)
    return ps, l


def _paged_body(pt_ref, q_ref, knew_ref, *rest, layer, n_seq, n_pages, rows, heads, s_len, tk,
                with_decay):
    if with_decay:
        (lfn_ref, tri_s_ref, tri_ref, cache_ref, lcache_ref, o_ref, kbuf, lbuf, sem) = rest
    else:
        (cache_ref, o_ref, kbuf, sem) = rest
    b = pl.program_id(0)
    slot = b % 2
    past = n_pages * LANE
    sp = knew_ref.shape[1]

    def issue(seq, sl):
        def body(p, c):
            pg = pt_ref[seq, p]
            _page_copy(cache_ref, layer, pg, kbuf, sl, p, rows, sem.at[sl, 0]).start()
            if with_decay:
                pltpu.make_async_copy(lcache_ref.at[layer, pg], lbuf.at[sl, p], sem.at[sl, 1]).start()
            return c
        lax.fori_loop(0, n_pages, body, 0, unroll=4)

    def wait(sl):
        _wait_all(kbuf.at[sl, pl.ds(0, rows), :], sem.at[sl, 0])
        if with_decay:
            _wait_all(lbuf.at[sl], sem.at[sl, 1])

    @pl.when(b == 0)
    def _():
        if rows < LANE:
            kbuf[:, rows:, :] = jnp.zeros((2, LANE - rows, past), F32)
        issue(0, 0)

    @pl.when(b + 1 < n_seq)
    def _():
        issue(b + 1, 1 - slot)

    wait(slot)

    q = q_ref[0]
    m_rows = q.shape[0]
    knew = knew_ref[0].astype(BF16)
    s_new = _dot_nt(q, knew)
    if with_decay:
        cin = _dot_terms(lbuf[slot].reshape(n_pages * 8, LANE), tri_ref[...], 3)
        run = jnp.zeros((8, 1), F32)
        negc = []
        for p in range(n_pages):
            cp = cin[p * 8:(p + 1) * 8]
            negc.append(-(cp + run) * LOG2E)
            run = run + cp[:, LANE - 1:]
        cs_new = _dot_f32(lfn_ref[0], tri_s_ref[...])
        s_new = s_new + jnp.concatenate([-(run + cs_new) * LOG2E] * s_len, axis=0)
    sq = jnp.right_shift(lax.broadcasted_iota(jnp.int32, (m_rows, sp), 0), _log2(heads))
    jj = lax.broadcasted_iota(jnp.int32, (m_rows, sp), 1)
    s_new = jnp.where(jj <= sq, s_new, NEG)
    m = jnp.max(s_new, axis=-1, keepdims=True)
    p = jnp.exp2(s_new - m)
    parts = [(m, jnp.sum(p, axis=-1, keepdims=True), _dot(p.astype(BF16), knew))]
    ppc = tk // LANE
    for c in range(past // tk):
        kt = kbuf[slot, :, c * tk:(c + 1) * tk].astype(BF16)
        s = _dot(q, kt)
        if with_decay:
            nc = jnp.concatenate(negc[c * ppc:(c + 1) * ppc], axis=1)
            s = s + jnp.concatenate([nc] * s_len, axis=0)
        m = jnp.max(s, axis=-1, keepdims=True)
        p = jnp.exp2(s - m)
        parts.append((m, jnp.sum(p, axis=-1, keepdims=True), _dot_nt(p.astype(BF16), kt)))
    o_ref[0] = _merge_partials(parts).astype(BF16)


def _paged_attn(page_table, q, knew, cache_t, layer, *, heads, s_len, decay=None):
    n_seq, n_pages = page_table.shape
    rows = cache_t.shape[2]
    past = n_pages * LANE
    tk = past // 2
    m = q.shape[1]
    sp = knew.shape[1]
    with_decay = decay is not None
    per_b = lambda b, pt: (b, 0, 0)
    in_specs = [pl.BlockSpec((1, m, LANE), per_b), pl.BlockSpec((1, sp, LANE), per_b)]
    args = [q, knew]
    scratch = [pltpu.VMEM((2, LANE, past), F32)]
    if with_decay:
        lfn, lcache_t = decay
        tri_s = np.triu(np.ones((sp, sp), np.float32))
        tri = jnp.asarray(np.triu(np.ones((LANE, LANE), np.float32)), BF16)
        in_specs += [pl.BlockSpec((1, 8, sp), per_b), pl.BlockSpec((sp, sp), lambda b, pt: (0, 0)),
                     pl.BlockSpec((LANE, LANE), lambda b, pt: (0, 0)),
                     pl.BlockSpec(memory_space=pl.ANY), pl.BlockSpec(memory_space=pl.ANY)]
        args += [lfn, tri_s, tri, cache_t, lcache_t]
        scratch.append(pltpu.VMEM((2, n_pages, 8, LANE), F32))
    else:
        in_specs.append(pl.BlockSpec(memory_space=pl.ANY))
        args.append(cache_t)
    scratch.append(pltpu.SemaphoreType.DMA((2, 2)))
    return pl.pallas_call(
        functools.partial(_paged_body, layer=layer, n_seq=n_seq, n_pages=n_pages, rows=rows,
                          heads=heads, s_len=s_len, tk=tk, with_decay=with_decay),
        grid_spec=pltpu.PrefetchScalarGridSpec(
            num_scalar_prefetch=1, grid=(n_seq,), in_specs=in_specs,
            out_specs=pl.BlockSpec((1, m, LANE), per_b), scratch_shapes=scratch),
        out_shape=jax.ShapeDtypeStruct((n_seq, m, LANE), BF16),
        compiler_params=_cparams(1),
        name="paged_attn_decay" if with_decay else "paged_attn",
    )(page_table, *args)


def _nsa_s1_body(pt_ref, q_ref, wnew_ref, win_ref, cb_ref, wb_ref, wnb_ref, amat_ref, pair_ref,
                 cache_ref, ocmp_ref, owin_ref, idx_ref, kbuf, sem, *, layer, n_seq, n_pages, s_len,
                 k_free):
    b = pl.program_id(0)
    slot = b % 2
    past = n_pages * LANE
    hh = NSA_HEADS

    def issue(seq, sl):
        def body(p, c):
            _page_copy(cache_ref, layer, pt_ref[seq, p], kbuf, sl, p, LANE, sem.at[sl]).start()
            return c
        lax.fori_loop(0, n_pages, body, 0, unroll=4)

    @pl.when(b == 0)
    def _():
        issue(0, 0)

    @pl.when(b + 1 < n_seq)
    def _():
        issue(b + 1, 1 - slot)

    _wait_all(kbuf.at[slot], sem.at[slot])
    q = q_ref[0]
    tc = amat_ref.shape[0]
    amat = amat_ref[...]
    means = []
    for c in range(past // tc):
        means.append(_dot_terms(kbuf[slot, :, c * tc:(c + 1) * tc], amat, 2))
    kvc = jnp.concatenate(means, axis=1).astype(BF16)
    s = _dot(q, kvc) + cb_ref[...]
    e = jnp.exp2(s - jnp.max(s, axis=-1, keepdims=True))
    p = e / jnp.sum(e, axis=-1, keepdims=True)
    ocmp_ref[0] = _dot_nt(p.astype(BF16), kvc)
    ps = _dot_f32(p, pair_ref[...])
    score = ps[0:s_len]
    for h in range(1, hh):
        score = score + ps[h * s_len:(h + 1) * s_len]
    n_past = score.shape[1]
    lane = lax.broadcasted_iota(jnp.int32, score.shape, 1).astype(F32)
    work = jnp.where((lane == 0.0) | (lane == n_past - 1.0), -jnp.inf, score)
    out_lane = lax.broadcasted_iota(jnp.int32, (s_len, LANE), 1)
    idx_out = jnp.zeros((s_len, LANE), F32)
    for r in range(k_free):
        mx = jnp.max(work, axis=-1, keepdims=True)
        idx = jnp.min(jnp.where(work == mx, lane, float(n_past)), axis=-1, keepdims=True)
        idx_out = jnp.where(out_lane == r, idx, idx_out)
        work = jnp.where(lane == idx, -jnp.inf, work)
    idx_ref[0] = idx_out.astype(jnp.int32)
    wst = win_ref[0, 0].astype(BF16)
    wnew = wnew_ref[0].astype(BF16)
    (p_w, p_n), l = _softmax_parts([_dot(q, wst) + wb_ref[...], _dot_nt(q, wnew) + wnb_ref[...]])
    owin_ref[0] = (_dot_nt(p_w.astype(BF16), wst) + _dot(p_n.astype(BF16), wnew)) / l


def _nsa_s1(page_table, q, wnew, win_t, cache_t, layer, tables, *, s_len, k_free):
    n_seq, n_pages = page_table.shape
    past = n_pages * LANE
    cb, wb, wnb = tables
    m = q.shape[1]
    sp = wnew.shape[1]
    wlen = win_t.shape[3]
    tc = min(past, CMP_BLOCK * LANE)
    amat = (np.arange(tc)[:, None] // CMP_BLOCK == np.arange(tc // CMP_BLOCK)[None, :])
    amat = jnp.asarray(amat.astype(np.float32) / CMP_BLOCK, BF16)
    nc = past // CMP_BLOCK
    ratio = SEL_BLOCK // CMP_BLOCK
    pair = (np.arange(nc)[:, None] // ratio == np.arange(nc // ratio)[None, :]).astype(np.float32)
    per_b = lambda b, pt: (b, 0, 0)
    c2 = lambda b, pt: (0, 0)
    return pl.pallas_call(
        functools.partial(_nsa_s1_body, layer=layer, n_seq=n_seq, n_pages=n_pages, s_len=s_len,
                          k_free=k_free),
        grid_spec=pltpu.PrefetchScalarGridSpec(
            num_scalar_prefetch=1, grid=(n_seq,),
            in_specs=[pl.BlockSpec((1, m, LANE), per_b), pl.BlockSpec((1, sp, LANE), per_b),
                      pl.BlockSpec((1, 1, LANE, wlen), lambda b, pt: (layer, b, 0, 0)),
                      pl.BlockSpec(cb.shape, c2), pl.BlockSpec(wb.shape, c2),
                      pl.BlockSpec(wnb.shape, c2), pl.BlockSpec(amat.shape, c2),
                      pl.BlockSpec(pair.shape, c2), pl.BlockSpec(memory_space=pl.ANY)],
            out_specs=[pl.BlockSpec((1, m, LANE), per_b), pl.BlockSpec((1, m, LANE), per_b),
                       pl.BlockSpec((1, s_len, LANE), per_b)],
            scratch_shapes=[pltpu.VMEM((2, LANE, past), F32), pltpu.SemaphoreType.DMA((2,))]),
        out_shape=[jax.ShapeDtypeStruct((n_seq, m, LANE), F32),
                   jax.ShapeDtypeStruct((n_seq, m, LANE), F32),
                   jax.ShapeDtypeStruct((n_seq, s_len, LANE), jnp.int32)],
        compiler_params=_cparams(1),
        name="nsa_sample_cmp",
    )(page_table, q, wnew, win_t, cb, wb, wnb, amat, pair, cache_t)


def _nsa_s2_body(pt_ref, idx_ref, q_ref, snew_ref, g_ref, ocmp_ref, owin_ref, lb_ref, nb_ref,
                 cache_ref, o_ref, kbuf, sem, *, layer, n_seq, n_pages, s_len, k_free):
    b = pl.program_id(0)
    slot = b % 2
    n_own = s_len * k_free
    n_slots = n_own + 2
    per_page = LANE // SEL_BLOCK

    def block_of(seq, j):
        if j < n_own:
            return idx_ref[(seq * s_len + j // k_free) * LANE + j % k_free]
        return 0 if j == n_own else n_pages * per_page - 1

    def issue(seq, sl):
        for j in range(n_slots):
            blk = block_of(seq, j)
            pg = pt_ref[seq, blk >> _log2(per_page)]
            _page_copy(cache_ref, layer, pg, kbuf, sl, j, LANE, sem.at[sl]).start()

    @pl.when(b == 0)
    def _():
        issue(0, 0)

    @pl.when(b + 1 < n_seq)
    def _():
        issue(b + 1, 1 - slot)

    _wait_all(kbuf.at[slot], sem.at[slot])
    q = q_ref[0]
    m_rows = q.shape[0]
    kt = kbuf[slot].astype(BF16)
    s = _dot(q, kt)
    assert s_len & (s_len - 1) == 0
    row_s = jnp.bitwise_and(lax.broadcasted_iota(jnp.int32, (m_rows, LANE), 0), s_len - 1)
    lane_half = jnp.right_shift(lax.broadcasted_iota(jnp.int32, (m_rows, LANE), 1), _log2(SEL_BLOCK))
    bias = []
    last_page = (n_pages - 1) * per_page
    for j in range(n_slots):
        blk = block_of(b, j)
        ok = lane_half == (blk & (per_page - 1))
        if j < n_own:
            ok = ok & (row_s == j // k_free)
            near = jnp.where(blk >= last_page, lb_ref[...], 0.0)
        else:
            near = lb_ref[...] if j == n_slots - 1 else 0.0
        bias.append(jnp.where(ok, near, NEG))
    s = s + jnp.concatenate(bias, axis=1)
    snew = snew_ref[0].astype(BF16)
    (p_s, p_n), l = _softmax_parts([s, _dot_nt(q, snew) + nb_ref[...]])
    o_sel = (_dot_nt(p_s.astype(BF16), kt) + _dot(p_n.astype(BF16), snew)) / l
    g = g_ref[0]
    o = g[:, 0:1] * ocmp_ref[0] + g[:, 1:2] * o_sel + g[:, 2:3] * owin_ref[0]
    o_ref[0] = o.astype(BF16)


def _nsa_s2(page_table, idx, q, snew, gates, o_cmp, o_win, cache_t, layer, tables, *, s_len, k_free):
    n_seq, n_pages = page_table.shape
    lb, nb = tables
    m = q.shape[1]
    sp = snew.shape[1]
    n_slots = s_len * k_free + 2
    per_b = lambda b, pt, ix: (b, 0, 0)
    c2 = lambda b, pt, ix: (0, 0)
    return pl.pallas_call(
        functools.partial(_nsa_s2_body, layer=layer, n_seq=n_seq, n_pages=n_pages, s_len=s_len,
                          k_free=k_free),
        grid_spec=pltpu.PrefetchScalarGridSpec(
            num_scalar_prefetch=2, grid=(n_seq,),
            in_specs=[pl.BlockSpec((1, m, LANE), per_b), pl.BlockSpec((1, sp, LANE), per_b),
                      pl.BlockSpec((1, m, LANE), per_b), pl.BlockSpec((1, m, LANE), per_b),
                      pl.BlockSpec((1, m, LANE), per_b), pl.BlockSpec(lb.shape, c2),
                      pl.BlockSpec(nb.shape, c2), pl.BlockSpec(memory_space=pl.ANY)],
            out_specs=pl.BlockSpec((1, m, LANE), per_b),
            scratch_shapes=[pltpu.VMEM((2, LANE, n_slots * LANE), F32),
                            pltpu.SemaphoreType.DMA((2,))]),
        out_shape=jax.ShapeDtypeStruct((n_seq, m, LANE), BF16),
        compiler_params=_cparams(1),
        name="nsa_sample_sel",
    )(page_table, idx.reshape(-1), q, snew, gates, o_cmp, o_win, lb, nb, cache_t)


def _nsa_sample_tables(rel_table, past, s_len, sp, wlen):
    hh = NSA_HEADS
    far = rel_table[REL_BUCKETS - 1].astype(F32)[:, None, None]
    qpos = past + np.arange(s_len)
    rows = lambda a: a.reshape(hh * s_len, a.shape[-1])

    def bias(dist):
        bucket = np.searchsorted(np.asarray(_BUCKET_LO), np.maximum(dist, 0), side="right")
        return jnp.moveaxis(rel_table.astype(F32)[bucket], -1, 0)

    c_end = np.arange(past // CMP_BLOCK) * CMP_BLOCK + CMP_BLOCK - 1
    cb = rows(bias(qpos[:, None] - c_end[None, :]) * LOG2E)
    dist = qpos[:, None] - (past - wlen + np.arange(wlen))[None, :]
    wb = rows(jnp.where(dist < WINDOW, bias(dist) * LOG2E, NEG))
    dist_new = np.arange(s_len)[:, None] - np.arange(sp)[None, :]
    newb = bias(dist_new)
    wnb = rows(jnp.where(dist_new >= 0, newb * LOG2E, NEG))
    assert LANE >= REL_MAX_DIST
    dist = qpos[:, None] - (past - LANE + np.arange(LANE))[None, :]
    lb = rows((bias(dist) - far) * LOG2E)
    nb = rows(jnp.where(dist_new >= 0, (newb - far) * LOG2E, NEG))
    return (cb, wb, wnb), (lb, nb)


TQ = 256
TQ_FOX = 128
TK = 256
GROUP = 4
TM_PROMPT = 256
TM_FFN = 512
NEW_PAD = 16
KEY_PAD = WINDOW


def _fm_to_rows(a, mid):
    lead = a.shape[:-2]
    n = a.shape[-1]
    a = a.reshape(lead + mid + (n,))
    return jnp.moveaxis(a, -1, len(lead))


def kernel(x_prompt, x_sample, cache_nsa_cmp, cache_nsa_sel, state_nsa_win, cache_mla, cache_fox_kv, cache_fox_logf, state_ffn_conv, page_table, c_prompt, c_sample, rel_table, w_ada, b_ada, g_attn, g_ffn, w_in, mla_g_q, mla_w_uq, mla_g_kv, mla_w_uk, mla_w_uv, fox_b_f, w_br, w_o, w_ffn_in, conv_w, conv_b, w_ffn_out, g_final):
    b, t, d = x_prompt.shape
    db, s_len, _ = x_sample.shape
    depth = w_in.shape[0]
    pool = cache_nsa_cmp.shape[1]
    n_pages = page_table.shape[1]
    past = n_pages * LANE
    d_ff = w_ffn_out.shape[1]
    wlen = state_nsa_win.shape[2]
    assert cache_nsa_cmp.shape[2] == LANE and TQ >= REL_MAX_DIST and t % TK == 0
    assert t // SEL_BLOCK <= LANE and TM_PROMPT == 8 * CMP_BLOCK
    params = dict(w_in=w_in, mla_g_q=mla_g_q, mla_w_uq=mla_w_uq, mla_g_kv=mla_g_kv, mla_w_uk=mla_w_uk,
                  mla_w_uv=mla_w_uv, fox_b_f=fox_b_f, w_br=w_br, w_o=w_o, w_ffn_in=w_ffn_in,
                  conv_w=conv_w, conv_b=conv_b, w_ffn_out=w_ffn_out, g_attn=g_attn, g_ffn=g_ffn)

    n_c = b + db
    c_all = jnp.pad(jnp.concatenate([c_prompt, c_sample], axis=0), ((0, -n_c % 8), (0, 0)))
    mods = _adaln(c_all, w_ada, b_ada)

    cmp_t = jnp.transpose(cache_nsa_cmp, (0, 1, 3, 4, 2)).reshape(depth, pool, LANE, LANE)
    sel_t = jnp.transpose(cache_nsa_sel, (0, 1, 3, 4, 2)).reshape(depth, pool, LANE, LANE)
    fkv_t = jnp.transpose(cache_fox_kv, (0, 1, 3, 4, 5, 2)).reshape(depth, pool, LANE, LANE)
    mla_t = jnp.transpose(cache_mla, (0, 1, 3, 2))
    lf_t = jnp.transpose(cache_fox_logf, (0, 1, 3, 2))
    win_t = jnp.transpose(state_nsa_win, (0, 1, 3, 4, 2)).reshape(depth, db, LANE, wlen)
    conv_pre_s = jnp.transpose(state_ffn_conv, (0, 2, 1, 3))
    conv_pre_p = jnp.zeros((b, CONV_W - 1, d_ff), F32)

    rope_p = _rope_tables(jnp.arange(t))
    rope_s = _rope_tables(jnp.repeat(past + jnp.arange(s_len), db))
    tabs_p = _nsa_prompt_tables(rel_table, t, TQ)
    tabs_s1, tabs_s2 = _nsa_sample_tables(rel_table, past, s_len, NEW_PAD, wlen)
    n_sel_s = -(-(past + s_len) // SEL_BLOCK)
    k_free = min(N_SEL, n_sel_s) - 3

    tpb = t // TM_PROMPT
    tpb_ffn = t // TM_FFN
    idx_p = lambda i: (i // tpb, 0, 0)
    idx_pf = lambda i: (i // tpb_ffn, 0, 0)
    idx_s = lambda i: (0, 0, 0)
    xp = x_prompt.reshape(b * t, d)
    xs = jnp.transpose(x_sample, (1, 0, 2)).reshape(s_len * db, d)

    def new_rows(a):
        return jnp.pad(jnp.transpose(a, (2, 0, 1)), ((0, 0), (0, NEW_PAD - s_len), (0, 0)))

    def q_rows(qt, heads, width, head_major):
        q = jnp.transpose(qt.reshape(s_len, heads, LANE, db)[:, :, :width],
                          (3, 1, 0, 2) if head_major else (3, 0, 1, 2))
        return jnp.pad(q.reshape(db, s_len * heads, width), ((0, 0), (0, 0), (0, LANE - width)))

    def o_rows(o, heads, head_major):
        o = o.reshape((db, heads, s_len, LANE) if head_major else (db, s_len, heads, LANE))
        o = jnp.transpose(o, (2, 0, 1, 3) if head_major else (1, 0, 2, 3))
        return o.reshape(s_len * db, heads * LANE)

    def front_pad(a, axis):
        pads = [(0, 0)] * a.ndim
        pads[axis] = (KEY_PAD, 0)
        return jnp.pad(a, pads)

    rows_p, rows_s = [], []
    for l in range(depth):
        lw = _prep_layer(l, params)
        m6 = [mods[l][:, k * d:(k + 1) * d] for k in range(6)]
        sh_a, sc_a, gt_a, sh_f, sc_f, gt_f = [a[:b][:, None, :] for a in m6]
        sh_as, sc_as, gt_as, sh_fs, sc_fs, gt_fs = [a[b:n_c][None] for a in m6]
        last = l == depth - 1

        pre = _pre_proj(xp, sc_a, sh_a, idx_p, lw, rope_p, n_groups=b, tiles_per_group=tpb,
                        n_pos_tiles=tpb, tm=TM_PROMPT, with_keys=True)
        nc = t // CMP_BLOCK
        kvc = pre["kvc"].reshape(b, nc // 2, 2, LANE)
        kvc = jnp.transpose(kvc, (0, 2, 1, 3)).reshape(b, nc, LANE)
        kvc_k = jnp.pad(kvc[:, :, :NSA_DH], ((0, 0), (0, 0), (0, LANE - NSA_DH))).astype(BF16)
        kvct = jnp.transpose(kvc, (0, 2, 1)).astype(BF16)
        rs = lambda a: a.reshape(b, t, a.shape[-1])
        o_nsa = _nsa_prompt(pre["qnt"], kvc_k, kvct, front_pad(rs(pre["ks"]), 1),
                            front_pad(pre["vst"], 2), front_pad(rs(pre["kw"]), 1),
                            front_pad(pre["vwt"], 2), pre["gt"], tabs_p, tq=TQ, tk=TK, pad=KEY_PAD)
        o_lat = _causal_attn(pre["qmt"], rs(pre["km"]), pre["vmt"], heads=MLA_HEADS, tq=TQ, tk=TK,
                             group=GROUP)
        o_fox = _causal_attn(pre["qft"], rs(pre["kf"]), pre["vft"], heads=FOX_HEADS, tq=TQ_FOX, tk=TK,
                             group=GROUP)
        x1 = _mix(xp, sc_a, sh_a, gt_a, idx_pf, lw, o_nsa, o_lat, o_fox, tm=TM_FFN, value_lane=0)
        xp, tail_p = _ffn(x1, sc_f, sh_f, gt_f, idx_pf, lw, conv_pre_p, g_final[None, :], tm=TM_FFN,
                          tiles_per_seq=tpb_ffn, time_major=False, final_norm=last)
        wkeep = min(WINDOW, t)
        rows_p.append((_fm_to_rows(pre["ct"], (2, NSA_DH)), _fm_to_rows(pre["st"], (2, NSA_DH)),
                       _fm_to_rows(pre["wt"][:, :, t - wkeep:], (2, NSA_DH)),
                       _fm_to_rows(pre["mt"][:, :MLA_KV_RANK + MLA_ROPE], (MLA_KV_RANK + MLA_ROPE,)),
                       _fm_to_rows(pre["ft"], (2, 1, FOX_DH)), _fm_to_rows(pre["lft"], (FOX_HEADS,)),
                       tail_p))

        pre_s = _pre_proj(xs, sc_as, sh_as, idx_s, lw, rope_s, n_groups=s_len, tiles_per_group=1,
                          n_pos_tiles=s_len, tm=db, with_keys=False)
        qn_s = q_rows(pre_s["qnt"], NSA_HEADS, NSA_DH, True)
        gates_s = jnp.transpose(pre_s["gt"][:, :3 * NSA_HEADS].reshape(s_len, NSA_HEADS, 3, db),
                                (3, 1, 0, 2)).reshape(db, NSA_HEADS * s_len, 3)
        gates_s = jnp.pad(gates_s, ((0, 0), (0, 0), (0, LANE - 3)))
        o_cmp, o_win, idx = _nsa_s1(page_table, qn_s, new_rows(pre_s["wt"]), win_t, cmp_t, l, tabs_s1,
                                    s_len=s_len, k_free=k_free)
        o_nsa_s = _nsa_s2(page_table, idx, qn_s, new_rows(pre_s["st"]), gates_s, o_cmp, o_win, sel_t, l,
                          tabs_s2, s_len=s_len, k_free=k_free)
        o_lat_s = _paged_attn(page_table, q_rows(pre_s["qmt"], MLA_HEADS, LANE, False),
                              new_rows(pre_s["mt"]), mla_t, l, heads=MLA_HEADS, s_len=s_len)
        lfn = jnp.pad(jnp.transpose(pre_s["lft"], (2, 1, 0)), ((0, 0), (0, 0), (0, NEW_PAD - s_len)))
        o_fox_s = _paged_attn(page_table, q_rows(pre_s["qft"], FOX_HEADS, FOX_DH, False),
                              new_rows(pre_s["ft"]), fkv_t, l, heads=FOX_HEADS, s_len=s_len,
                              decay=(lfn, lf_t))
        x1s = _mix(xs, sc_as, sh_as, gt_as, idx_s, lw, o_rows(o_nsa_s, NSA_HEADS, True),
                   o_rows(o_lat_s, MLA_HEADS, False), o_rows(o_fox_s, FOX_HEADS, False), tm=db,
                   value_lane=NSA_DH)
        xs, tail_s = _ffn(x1s, sc_fs, sh_fs, gt_fs, idx_s, lw, conv_pre_s[l], g_final[None, :], tm=db,
                          tiles_per_seq=s_len, time_major=True, final_norm=last)
        wfull = jnp.concatenate([win_t[l], jnp.transpose(pre_s["wt"], (2, 1, 0))], axis=2)
        wfull = wfull[:, :, wfull.shape[2] - min(WINDOW, past + s_len):]
        fm_s = lambda a, mid: jnp.swapaxes(_fm_to_rows(a, mid), 0, 1)
        rows_s.append((fm_s(pre_s["ct"], (2, NSA_DH)), fm_s(pre_s["st"], (2, NSA_DH)),
                       _fm_to_rows(wfull, (2, NSA_DH)),
                       fm_s(pre_s["mt"][:, :MLA_KV_RANK + MLA_ROPE], (MLA_KV_RANK + MLA_ROPE,)),
                       fm_s(pre_s["ft"], (2, 1, FOX_DH)), fm_s(pre_s["lft"], (FOX_HEADS,)),
                       jnp.transpose(tail_s, (1, 0, 2))))

    y_prompt = xp.reshape(b, t, d)
    y_sample = jnp.transpose(xs.reshape(s_len, db, d), (1, 0, 2))
    outs_p = [jnp.stack(a) for a in zip(*rows_p)]
    outs_s = [jnp.stack(a) for a in zip(*rows_s)]
    return (y_prompt, y_sample, *outs_p, *outs_s)
```

```python
import functools
import math

import numpy as np
import jax
import jax.numpy as jnp
from jax import lax
from jax.experimental import pallas as pl
from jax.experimental.pallas import tpu as pltpu

F32 = jnp.float32
BF16 = jnp.bfloat16

NSA_HEADS = 4
NSA_DH = 64
CMP_BLOCK = 32
SEL_BLOCK = 64
N_SEL = 16
WINDOW = 512
MLA_HEADS = 4
MLA_NOPE = 64
MLA_ROPE = 32
MLA_KV_RANK = 64
ROPE_BASE = 10000.0
FOX_HEADS = 8
FOX_DH = 64
REL_BUCKETS = 32
REL_MAX_DIST = 128
CONV_W = 3
EPS = 1e-6
NEG = -1e30
LOG2E = math.log2(math.e)
LANE = 128
VMEM_LIMIT = 56 * 1024 * 1024

NSA_SCALE = NSA_DH ** -0.5
MLA_SCALE = (MLA_NOPE + MLA_ROPE) ** -0.5
FOX_SCALE = FOX_DH ** -0.5

ONES_ROW = 64
VROWS = 80
DECAY_PARTS = 3

_NT = (((1,), (1,)), ((), ()))


def _cparams(n_axes):
    return pltpu.CompilerParams(dimension_semantics=("arbitrary",) * n_axes,
                                vmem_limit_bytes=VMEM_LIMIT)


def _dot(a, b):
    return jnp.dot(a, b, preferred_element_type=F32)


def _dot_nt(a, b):
    return lax.dot_general(a, b, _NT, preferred_element_type=F32)


def _dot_f32(a, b):
    return jnp.dot(a, b, preferred_element_type=F32, precision=lax.Precision.HIGHEST)


def _dot_terms(x, w, terms):
    out = None
    rest = x
    for _ in range(terms):
        piece = rest.astype(BF16)
        rest = rest - piece.astype(F32)
        d = _dot(piece, w)
        out = d if out is None else out + d
    return out


def _rms(x, g):
    return x * lax.rsqrt(jnp.mean(x * x, axis=-1, keepdims=True) + EPS) * g


def _log2(n):
    assert n & (n - 1) == 0, n
    return n.bit_length() - 1


def _ada_body(c_ref, w_ref, b_ref, o_ref):
    c = c_ref[...]
    sc = (c * jax.nn.sigmoid(c)).astype(BF16)
    o_ref[0] = _dot(sc, w_ref[0].astype(BF16)) + b_ref[0]


def _adaln(c_all, w_ada, b_ada):
    depth, d, n6 = w_ada.shape
    rows = c_all.shape[0]
    tn = 512
    return pl.pallas_call(
        _ada_body,
        grid=(depth, n6 // tn),
        in_specs=[pl.BlockSpec((rows, d), lambda l, n: (0, 0)),
                  pl.BlockSpec((1, d, tn), lambda l, n: (l, 0, n)),
                  pl.BlockSpec((1, 1, tn), lambda l, n: (l, 0, n))],
        out_specs=pl.BlockSpec((1, rows, tn), lambda l, n: (l, 0, n)),
        out_shape=jax.ShapeDtypeStruct((depth, rows, n6), F32),
        compiler_params=_cparams(2),
        name="adaln",
    )(c_all, w_ada, b_ada.reshape(depth, 1, n6))


_T_C, _T_S, _T_W, _T_F, _T_M, _T_LF, _T_G, _T_QD = 0, 128, 256, 384, 512, 640, 648, 664
_T_QN = _T_QD + 256
_T_QF = _T_QN + NSA_HEADS * LANE
_T_END = _T_QF + FOX_HEADS * LANE
_R_KS, _R_KW, _R_KF, _R_KM, _R_KMS, _R_C, _R_LF, _R_END = 0, 128, 256, 384, 512, 640, 768, 896


def _value_tile(vt):
    n = vt.shape[1]
    return jnp.concatenate([vt, jnp.ones((VROWS - ONES_ROW, n), F32)], axis=0).astype(BF16)


def _pre_body(x_ref, sc_ref, sh_ref, g_ref, wrow_ref, wt_ref, gq_ref, wuq_ref, wuk_ref, pp_ref,
              gkv_ref, gkvr_ref, bf_ref, bfr_ref, aug_ref, cs1_ref, cs2_ref, csq1_ref, csq2_ref,
              cr1_ref, cr2_ref, tril_ref, avg_ref,
              ct_ref, st_ref, wtt_ref, ft_ref, mt_ref, lf_ref, gt_ref, qn_ref, qf_ref, qm_ref,
              vs_ref, vw_ref, vf_ref, vm_ref, ks_ref, kw_ref, kf_ref, km_ref, kvc_ref,
              carry_ref, *, tiles_per_seq, with_keys):
    x = x_ref[...]
    tm = x.shape[0]
    h = _rms(x, g_ref[...]) * (1.0 + sc_ref[0]) + sh_ref[0]
    hb = h.astype(BF16)

    pt = _dot_nt(wt_ref[...], hb)
    ct_ref[0] = pt[_T_C:_T_C + 128]
    st_ref[0] = pt[_T_S:_T_S + 128]
    wtt_ref[0] = pt[_T_W:_T_W + 128]
    ft_ref[0] = pt[_T_F:_T_F + 128]
    ckv = pt[_T_M:_T_M + 64]
    ckv = ckv * lax.rsqrt(jnp.mean(ckv * ckv, axis=0, keepdims=True) + EPS) * gkv_ref[...]
    krot = pt[_T_M + 64:_T_M + 96] * cs1_ref[...] + pt[_T_M + 96:_T_M + 128] * cs2_ref[...]
    mt_ref[0, 0:64] = ckv
    mt_ref[0, 64:96] = krot
    mt_ref[0, 96:128] = jnp.zeros_like(krot)
    lf_ref[0] = jax.nn.log_sigmoid(pt[_T_LF:_T_LF + 8] + bf_ref[...])
    gt_ref[0] = jax.nn.sigmoid(pt[_T_G:_T_G + 16])
    qn_ref[0] = pt[_T_QN:_T_QF].astype(BF16)
    qf_ref[0] = (pt[_T_QF:_T_END] + aug_ref[...]).astype(BF16)
    qd = pt[_T_QD:_T_QD + 256]
    qd = qd * lax.rsqrt(jnp.mean(qd * qd, axis=0, keepdims=True) + EPS) * gq_ref[...]
    q = _dot(wuq_ref[...], qd.astype(BF16))
    qrot = q[256:384] * csq1_ref[...] + q[384:512] * csq2_ref[...]
    qm = _dot(wuk_ref[...], q[:256].astype(BF16)) * (MLA_SCALE * LOG2E)
    qm = qm + _dot(pp_ref[...], (qrot * (MLA_SCALE * LOG2E)).astype(BF16))
    qm_ref[0] = qm.astype(BF16)
    vs_ref[0] = _value_tile(pt[_T_S + 64:_T_S + 128])
    vw_ref[0] = _value_tile(pt[_T_W + 64:_T_W + 128])
    vf_ref[0] = _value_tile(pt[_T_F + 64:_T_F + 128])
    vm_ref[0] = _value_tile(ckv)

    if with_keys:
        pr = _dot(hb, wrow_ref[...])
        lane = lax.broadcasted_iota(jnp.int32, (tm, LANE), 1)
        t_in_seq = (pl.program_id(0) % tiles_per_seq) * tm
        pos = t_in_seq + lax.broadcasted_iota(jnp.int32, (tm, LANE), 0)
        blk_ind = jnp.where(lane == jnp.right_shift(pos, _log2(SEL_BLOCK)), 1.0, 0.0)
        ks_ref[:, 0:LANE] = pr[:, _R_KS:_R_KS + LANE].astype(BF16)
        ks_ref[:, LANE:2 * LANE] = blk_ind.astype(BF16)
        kw_ref[...] = pr[:, _R_KW:_R_KW + LANE].astype(BF16)
        kvd = pr[:, _R_KM:_R_KM + LANE]
        is_c = lane < MLA_KV_RANK
        ms = jnp.sum(jnp.where(is_c, kvd * kvd, 0.0), axis=1, keepdims=True) / MLA_KV_RANK
        km = jnp.where(is_c, kvd * lax.rsqrt(ms + EPS) * gkvr_ref[...],
                       kvd * cr1_ref[...] + pr[:, _R_KMS:_R_KMS + LANE] * cr2_ref[...])
        km_ref[...] = km.astype(BF16)
        @pl.when(pl.program_id(0) % tiles_per_seq == 0)
        def _():
            carry_ref[...] = jnp.zeros_like(carry_ref)
        lfr = jnp.where(lane < FOX_HEADS, jax.nn.log_sigmoid(pr[:, _R_LF:_R_LF + LANE] + bfr_ref[...]),
                        0.0)
        csum = _dot_f32(tril_ref[...], lfr) + carry_ref[0:1]
        carry_ref[...] = jnp.broadcast_to(csum[tm - 1:], carry_ref.shape)
        kf = pr[:, _R_KF:_R_KF + LANE]
        rest = -csum * LOG2E
        for part in range(DECAY_PARTS):
            term = rest.astype(BF16).astype(F32)
            rest = rest - term
            kf = kf + pltpu.roll(term, FOX_DH + part * FOX_HEADS, 1)
        kf_ref[...] = kf.astype(BF16)
        kvc_ref[...] = _dot_f32(avg_ref[...], pr[:, _R_C:_R_C + LANE])
    else:
        ks_ref[...] = jnp.zeros(ks_ref.shape, BF16)
        kw_ref[...] = jnp.zeros(kw_ref.shape, BF16)
        kf_ref[...] = jnp.zeros(kf_ref.shape, BF16)
        km_ref[...] = jnp.zeros(km_ref.shape, BF16)
        kvc_ref[...] = jnp.zeros(kvc_ref.shape, F32)


def _pre_proj(x, sc, sh, mod_index, lw, pos_tabs, *, n_groups, tiles_per_group, n_pos_tiles, tm,
              with_keys):
    n, d = x.shape
    cs1, cs2, csq1, csq2, cr1, cr2 = pos_tabs
    n_tiles = n // tm
    tpg = tiles_per_group
    tril = np.tril(np.ones((tm, tm), np.float32))
    nmean = 8
    avg = (np.arange(tm)[None, :] // CMP_BLOCK == np.arange(nmean)[:, None]).astype(np.float32) / CMP_BLOCK
    row = lambda i: (i, 0)
    const2 = lambda i: (0, 0)
    tcol = lambda i: (i // tpg, 0, i % tpg)
    pcol = lambda i: (0, i % n_pos_tiles)
    prow = lambda i: (i % n_pos_tiles, 0)
    ncols = tpg * tm

    def tspec(r):
        return pl.BlockSpec((1, r, tm), tcol)

    def tshape(r, dt=F32):
        return jax.ShapeDtypeStruct((n_groups, r, ncols), dt)

    nq = NSA_HEADS * LANE
    nf = FOX_HEADS * LANE
    consts = [lw["g_attn"], lw["w_row"], lw["w_t"], lw["g_q"], lw["w_uq"], lw["w_uk"], lw["pp"],
              lw["g_kv"], lw["g_kv_row"], lw["b_f"], lw["b_f_row"], lw["q_aug"]]
    outs = pl.pallas_call(
        functools.partial(_pre_body, tiles_per_seq=tpg, with_keys=with_keys),
        grid=(n_tiles,),
        in_specs=[pl.BlockSpec((tm, d), row),
                  pl.BlockSpec((1,) + sc.shape[1:], mod_index),
                  pl.BlockSpec((1,) + sh.shape[1:], mod_index)]
        + [pl.BlockSpec(c.shape, const2) for c in consts]
        + [pl.BlockSpec((32, tm), pcol), pl.BlockSpec((32, tm), pcol),
           pl.BlockSpec((128, tm), pcol), pl.BlockSpec((128, tm), pcol),
           pl.BlockSpec((tm, 128), prow), pl.BlockSpec((tm, 128), prow),
           pl.BlockSpec((tm, tm), const2), pl.BlockSpec((nmean, tm), const2)],
        out_specs=[tspec(128), tspec(128), tspec(128), tspec(128), tspec(128), tspec(8), tspec(16),
                   tspec(nq), tspec(nf), tspec(nq), tspec(VROWS), tspec(VROWS), tspec(VROWS), tspec(VROWS),
                   pl.BlockSpec((tm, 2 * LANE), row), pl.BlockSpec((tm, LANE), row),
                   pl.BlockSpec((tm, LANE), row), pl.BlockSpec((tm, LANE), row),
                   pl.BlockSpec((nmean, LANE), row)],
        out_shape=[tshape(128), tshape(128), tshape(128), tshape(128), tshape(128), tshape(8),
                   tshape(16), tshape(nq, BF16), tshape(nf, BF16), tshape(nq, BF16),
                   tshape(VROWS, BF16), tshape(VROWS, BF16), tshape(VROWS, BF16), tshape(VROWS, BF16),
                   jax.ShapeDtypeStruct((n, 2 * LANE), BF16), jax.ShapeDtypeStruct((n, LANE), BF16),
                   jax.ShapeDtypeStruct((n, LANE), BF16), jax.ShapeDtypeStruct((n, LANE), BF16),
                   jax.ShapeDtypeStruct((n_tiles * nmean, LANE), F32)],
        scratch_shapes=[pltpu.VMEM((8, LANE), F32)],
        compiler_params=_cparams(1),
        name="pre_proj",
    )(x, sc, sh, *consts, cs1, cs2, csq1, csq2, cr1, cr2, tril, avg)
    keys = ("ct", "st", "wt", "ft", "mt", "lft", "gt", "qnt", "qft", "qmt", "vst", "vwt", "vft", "vmt",
            "ks", "kw", "kf", "km", "kvc")
    return dict(zip(keys, outs))


def _tflash_init(m_ref, acc_ref):
    m_ref[...] = jnp.full(m_ref.shape, NEG, F32)
    acc_ref[...] = jnp.zeros(acc_ref.shape, F32)


def _tflash_update(s, vt, m_ref, acc_ref):
    _tflash_update_many([(s, vt)], m_ref, acc_ref)


def _tflash_update_many(tiles, m_ref, acc_ref):
    m_old = m_ref[...]
    parts = []
    m_new = m_old
    for s, vt in tiles:
        mk = jnp.max(s, axis=0, keepdims=True)
        parts.append((mk, _dot(vt, jnp.exp2(s - mk).astype(BF16))))
        m_new = jnp.maximum(m_new, mk)
    acc = jnp.exp2(m_old - m_new) * acc_ref[...]
    for mk, pv in parts:
        acc = acc + jnp.exp2(mk - m_new) * pv
    acc_ref[...] = acc
    m_ref[...] = m_new


def _tflash_out(acc, heads, tq):
    o = acc[:ONES_ROW] * (1.0 / acc[ONES_ROW:ONES_ROW + 1])
    o = jnp.concatenate([o, jnp.zeros_like(o)], axis=0)
    return jnp.concatenate([o[:, h * tq:(h + 1) * tq].T for h in range(heads)], axis=1).astype(BF16)


def _lane_stack(q_ref, heads):
    return jnp.concatenate([q_ref[0, h * LANE:(h + 1) * LANE, :] for h in range(heads)], axis=1)


def _causal_body(q_ref, k_ref, v_ref, o_ref, m_ref, acc_ref, *, heads, tq, tk, group):
    i = pl.program_id(1)
    q0 = i * tq
    qt = _lane_stack(q_ref, heads)
    _tflash_init(m_ref, acc_ref)

    def tile(c0):
        return _dot(k_ref[0, pl.ds(c0, tk), :], qt), v_ref[0, :, pl.ds(c0, tk)]

    def group_step(j, carry):
        c0 = pl.multiple_of(j * group * tk, group * tk)
        _tflash_update_many([tile(c0 + k * tk) for k in range(group)], m_ref, acc_ref)
        return carry

    def full_step(j, carry):
        s, vt = tile(pl.multiple_of(j * tk, tk))
        _tflash_update(s, vt, m_ref, acc_ref)
        return carry

    n_full = q0 // tk
    n_groups = n_full // group
    lax.fori_loop(0, n_groups, group_step, 0)
    done = group * n_groups
    n_pairs = (n_full - done) // 2

    def pair_step(j, carry):
        c0 = pl.multiple_of((done + 2 * j) * tk, tk)
        _tflash_update_many([tile(c0), tile(c0 + tk)], m_ref, acc_ref)
        return carry

    lax.fori_loop(0, n_pairs, pair_step, 0)
    lax.fori_loop(done + 2 * n_pairs, n_full, full_step, 0)
    c0 = pl.multiple_of(n_full * tk, tk)
    s, vt = tile(c0)
    key = c0 + lax.broadcasted_iota(jnp.int32, s.shape, 0)
    qpos = q0 + jnp.bitwise_and(lax.broadcasted_iota(jnp.int32, s.shape, 1), tq - 1)
    _tflash_update(jnp.where(key <= qpos, s, NEG), vt, m_ref, acc_ref)
    o_ref[...] = _tflash_out(acc_ref[...], heads, tq)


def _causal_attn(qt, k, vt, *, heads, tq, tk, group):
    b, _, t = qt.shape
    nq = t // tq
    assert tq & (tq - 1) == 0 and tk % tq == 0 and t % tk == 0
    m = heads * tq
    return pl.pallas_call(
        functools.partial(_causal_body, heads=heads, tq=tq, tk=tk, group=group),
        grid=(b, nq),
        in_specs=[pl.BlockSpec((1, heads * LANE, tq), lambda bi, i: (bi, 0, i)),
                  pl.BlockSpec((1, t, LANE), lambda bi, i: (bi, 0, 0)),
                  pl.BlockSpec((1, VROWS, t), lambda bi, i: (bi, 0, 0))],
        out_specs=pl.BlockSpec((tq, heads * LANE), lambda bi, i: (bi * nq + i, 0)),
        out_shape=jax.ShapeDtypeStruct((b * t, heads * LANE), BF16),
        scratch_shapes=[pltpu.VMEM((1, m), F32), pltpu.VMEM((VROWS, m), F32)],
        compiler_params=_cparams(2),
        name="causal_attn_h%d" % heads,
    )(qt, k, vt)


def _top_k_neg_mask_t(score, k, out):
    n = score.shape[0]
    rowi = lax.broadcasted_iota(jnp.int32, score.shape, 0).astype(F32)
    work = score
    for _ in range(k):
        mx = jnp.max(work, axis=0, keepdims=True)
        idx = jnp.min(jnp.where(work == mx, rowi, float(n)), axis=0, keepdims=True)
        hit = rowi == idx
        out = jnp.where(hit, 0.0, out)
        work = jnp.where(hit, -jnp.inf, work)
    return out


def _nsa_body(q_ref, kvc_ref, kvct_ref, ks_ref, vs_ref, kw_ref, vw_ref, g_ref, cb_ref, sb_ref, wb_ref,
              o_ref, m_ref, acc_ref, m2_ref, acc2_ref, *, tq, tk, pad):
    hh = NSA_HEADS
    i = pl.program_id(1)
    q0 = i * tq
    qt = _lane_stack(q_ref, hh)
    cb = jnp.concatenate([cb_ref[h] for h in range(hh)], axis=1)
    s = _dot(kvc_ref[0], qt) + cb
    e = jnp.exp2(s - jnp.max(s, axis=0, keepdims=True))
    p = e * (1.0 / jnp.sum(e, axis=0, keepdims=True))
    p = jnp.where(cb > 0.5 * NEG, p, 0.0)
    o_cmp = _dot(kvct_ref[0], p.astype(BF16))
    nc = p.shape[0]
    pc = p[:, 0:tq]
    for h in range(1, hh):
        pc = pc + p[:, h * tq:(h + 1) * tq]
    n_sel = nc // 2
    score = pc[:n_sel] + pc[n_sel:]
    blk = lax.broadcasted_iota(jnp.int32, (n_sel, tq), 0)
    qpos = q0 + lax.broadcasted_iota(jnp.int32, (n_sel, tq), 1)
    cur = jnp.right_shift(qpos, _log2(SEL_BLOCK))
    forced = (blk == 0) | (blk == cur) | (blk == cur - 1)
    future = blk * SEL_BLOCK > qpos
    score = jnp.where(forced, -jnp.inf, jnp.where(future, -1e6, score))
    selneg = _top_k_neg_mask_t(score, min(N_SEL, n_sel) - 3, jnp.where(forced, 0.0, NEG)).astype(BF16)
    if n_sel < LANE:
        selneg = jnp.concatenate([selneg, jnp.zeros((LANE - n_sel, tq), BF16)], axis=0)
    qa = jnp.concatenate([qt, jnp.concatenate([selneg] * hh, axis=1)], axis=0)

    _tflash_init(m_ref, acc_ref)

    def far_tile(c0, n):
        return _dot(ks_ref[0, pl.ds(pad + c0, n), :], qa), vs_ref[0, :, pl.ds(pad + c0, n)]

    def far(c0, n):
        _tflash_update_many([far_tile(c0 + k * tk, min(n, tk)) for k in range(max(n // tk, 1))],
                            m_ref, acc_ref)

    n_far = jnp.maximum(i - 1, 0)
    per = GROUP * tk // tq
    n_big = n_far // per

    def big_step(j, carry):
        far(pl.multiple_of(j * GROUP * tk, GROUP * tk), GROUP * tk)
        return carry

    def small_step(j, carry):
        far(pl.multiple_of(j * tq, tq), tq)
        return carry

    lax.fori_loop(0, n_big, big_step, 0)
    lax.fori_loop(n_big * per, n_far, small_step, 0)
    c0 = pl.multiple_of(q0 - tq, tq)
    s_n = _dot(ks_ref[0, pl.ds(pad + c0, 2 * tq), :], qa) + sb_ref[...]
    key = c0 + lax.broadcasted_iota(jnp.int32, s_n.shape, 0)
    _tflash_update(jnp.where(key >= 0, s_n, NEG), vs_ref[0, :, pl.ds(pad + c0, 2 * tq)], m_ref, acc_ref)
    acc_s = acc_ref[...]
    o_sel = acc_s[:NSA_DH] * (1.0 / acc_s[ONES_ROW:ONES_ROW + 1])

    c0 = pl.multiple_of(q0 - WINDOW, tq)
    nw = WINDOW + tq
    s_w = _dot(kw_ref[0, pl.ds(pad + c0, nw), :], qt) + wb_ref[...]
    key = c0 + lax.broadcasted_iota(jnp.int32, s_w.shape, 0)
    _tflash_init(m2_ref, acc2_ref)
    _tflash_update(jnp.where(key >= 0, s_w, NEG), vw_ref[0, :, pl.ds(pad + c0, nw)], m2_ref, acc2_ref)
    acc_w = acc2_ref[...]
    o_win = acc_w[:NSA_DH] * (1.0 / acc_w[ONES_ROW:ONES_ROW + 1])

    g = g_ref[0]
    outs = []
    for h in range(hh):
        sl = slice(h * tq, (h + 1) * tq)
        o_h = (g[3 * h:3 * h + 1] * o_cmp[NSA_DH:, sl] + g[3 * h + 1:3 * h + 2] * o_sel[:, sl]
               + g[3 * h + 2:3 * h + 3] * o_win[:, sl])
        outs.append(jnp.concatenate([o_h, jnp.zeros_like(o_h)], axis=0).T)
    o_ref[...] = jnp.concatenate(outs, axis=1).astype(BF16)


def _bucket_lower_bounds():
    d = np.arange(REL_MAX_DIST + 1)
    exact = REL_BUCKETS // 2
    scaled = np.log(np.maximum(d, 1) / exact) / math.log(REL_MAX_DIST / exact)
    large = np.minimum(exact + (scaled * (REL_BUCKETS - exact)).astype(np.int64), REL_BUCKETS - 1)
    bucket = np.where(d < exact, d, large)
    assert np.all(np.diff(bucket) >= 0) and bucket[-1] == REL_BUCKETS - 1
    return [int(np.argmax(bucket >= k)) for k in range(1, REL_BUCKETS)]


_BUCKET_LO = _bucket_lower_bounds()


def _rel_bias_t(table, dist):
    tab = table.astype(F32)
    extra = (None,) * dist.ndim
    col = lambda k: tab[k][(slice(None),) + extra]
    out = jnp.broadcast_to(col(0), (tab.shape[1],) + dist.shape)
    for k, lo in enumerate(_BUCKET_LO, start=1):
        out = jnp.where((dist >= lo)[None], col(k), out)
    return out


def _head_lanes(a):
    return jnp.concatenate([a[h] for h in range(a.shape[0])], axis=1)


def _nsa_prompt_tables(rel_table, t, tq):
    far = rel_table[REL_BUCKETS - 1].astype(F32)[:, None, None]
    nc = t // CMP_BLOCK
    order = jnp.concatenate([jnp.arange(0, nc, 2), jnp.arange(1, nc, 2)])
    c_end = order * CMP_BLOCK + CMP_BLOCK - 1
    dist = jnp.arange(t)[None, :] - c_end[:, None]
    cb = jnp.where(dist >= 0, _rel_bias_t(rel_table, dist) * LOG2E, NEG)
    dist = jnp.arange(tq)[None, :] + tq - jnp.arange(2 * tq)[:, None]
    sb = jnp.where(dist >= 0, (_rel_bias_t(rel_table, dist) - far) * LOG2E, NEG)
    dist = jnp.arange(tq)[None, :] + WINDOW - jnp.arange(WINDOW + tq)[:, None]
    wb = jnp.where((dist >= 0) & (dist < WINDOW), _rel_bias_t(rel_table, dist) * LOG2E, NEG)
    return cb, _head_lanes(sb), _head_lanes(wb)


def _nsa_prompt(qnt, kvc, kvct, ks, vst, kw, vwt, gt, tables, *, tq, tk, pad):
    b, _, t = qnt.shape
    nq = t // tq
    cb, sb, wb = tables
    nc = kvc.shape[1]
    hh = NSA_HEADS
    m = hh * tq
    per_b = lambda bi, i: (bi, 0, 0)
    tile = lambda bi, i: (bi, 0, i)
    return pl.pallas_call(
        functools.partial(_nsa_body, tq=tq, tk=tk, pad=pad),
        grid=(b, nq),
        in_specs=[pl.BlockSpec((1, hh * LANE, tq), tile),
                  pl.BlockSpec((1, nc, LANE), per_b),
                  pl.BlockSpec((1, LANE, nc), per_b),
                  pl.BlockSpec((1,) + ks.shape[1:], per_b),
                  pl.BlockSpec((1,) + vst.shape[1:], per_b),
                  pl.BlockSpec((1,) + kw.shape[1:], per_b),
                  pl.BlockSpec((1,) + vwt.shape[1:], per_b),
                  pl.BlockSpec((1, 16, tq), tile),
                  pl.BlockSpec((hh, nc, tq), lambda bi, i: (0, 0, i)),
                  pl.BlockSpec(sb.shape, lambda bi, i: (0, 0)),
                  pl.BlockSpec(wb.shape, lambda bi, i: (0, 0))],
        out_specs=pl.BlockSpec((tq, hh * LANE), lambda bi, i: (bi * nq + i, 0)),
        out_shape=jax.ShapeDtypeStruct((b * t, hh * LANE), BF16),
        scratch_shapes=[pltpu.VMEM((1, m), F32), pltpu.VMEM((VROWS, m), F32),
                        pltpu.VMEM((1, m), F32), pltpu.VMEM((VROWS, m), F32)],
        compiler_params=_cparams(2),
        name="nsa_prompt",
    )(qnt, kvc, kvct, ks, vst, kw, vwt, gt, cb, sb, wb)


def _mix_body(x_ref, sc_ref, sh_ref, gt_ref, g_ref, on_ref, om_ref, of_ref, wmg_ref, wbn_ref,
              wuv_ref, wbm_ref, wbf_ref, wo_ref, o_ref):
    x = x_ref[...]
    d = x.shape[1]
    h = _rms(x, g_ref[...]) * (1.0 + sc_ref[0]) + sh_ref[0]
    mg = jax.nn.sigmoid(_dot(h.astype(BF16), wmg_ref[...]))
    o_mla = _dot(om_ref[...], wuv_ref[...]).astype(BF16)
    t = mg[:, :d] * _dot(on_ref[...], wbn_ref[...])
    t = t + mg[:, d:2 * d] * _dot(o_mla, wbm_ref[...])
    t = t + mg[:, 2 * d:] * _dot(of_ref[...], wbf_ref[...])
    o_ref[...] = x + gt_ref[0] * _dot(t.astype(BF16), wo_ref[...])


def _mix(x, sc, sh, gt, mod_index, lw, o_nsa, o_lat, o_fox, *, tm, value_lane):
    n, d = x.shape
    row = lambda i: (i, 0)
    const2 = lambda i: (0, 0)
    ws = [lw["w_mg"], lw["w_br_n"][value_lane], lw["w_uv"], lw["w_br_m"], lw["w_br_f"][value_lane],
          lw["w_o"]]
    return pl.pallas_call(
        _mix_body,
        grid=(n // tm,),
        in_specs=[pl.BlockSpec((tm, d), row)]
        + [pl.BlockSpec((1,) + a.shape[1:], mod_index) for a in (sc, sh, gt)]
        + [pl.BlockSpec((1, d), const2)]
        + [pl.BlockSpec((tm, a.shape[1]), row) for a in (o_nsa, o_lat, o_fox)]
        + [pl.BlockSpec(w.shape, const2) for w in ws],
        out_specs=pl.BlockSpec((tm, d), row),
        out_shape=jax.ShapeDtypeStruct((n, d), F32),
        compiler_params=_cparams(1),
        name="mix",
    )(x, sc, sh, gt, lw["g_attn"], o_nsa, o_lat, o_fox, *ws)


def _gelu_tanh(x):
    return 0.5 * x * (1.0 + jnp.tanh(math.sqrt(2.0 / math.pi) * (x + 0.044715 * (x * x * x))))


def _ffn_body(x_ref, sc_ref, sh_ref, gt_ref, g_ref, pre_ref, win_ref, cw_ref, cb_ref, wout_ref,
              gf_ref, o_ref, tail_ref, hist_ref, *, tiles_per_seq, time_major, final_norm, d_ff):
    x = x_ref[...]
    tm = x.shape[0]
    i = pl.program_id(0)
    h = _rms(x, g_ref[...]) * (1.0 + sc_ref[0]) + sh_ref[0]
    ab = _dot(h.astype(BF16), win_ref[...])
    a = ab[:, :d_ff]
    b = ab[:, d_ff:]
    cw = cw_ref[...]
    if time_major:
        @pl.when(i % tiles_per_seq == 0)
        def _():
            hist_ref[0] = pre_ref[0]
            hist_ref[1] = pre_ref[1]
        a2 = hist_ref[0]
        a1 = hist_ref[1]
        hist_ref[0] = a1
        hist_ref[1] = a
        tail_ref[0] = a
    else:
        @pl.when(i % tiles_per_seq == 0)
        def _():
            hist_ref[0, 6:8] = pre_ref[0]
        prev = hist_ref[0, 6:8]
        rid = lax.broadcasted_iota(jnp.int32, (tm, 1), 0)
        a1 = jnp.where(rid == 0, prev[1:2], pltpu.roll(a, 1, 0))
        a2 = jnp.where(rid == 0, prev[0:1], jnp.where(rid == 1, prev[1:2], pltpu.roll(a, 2, 0)))
        hist_ref[0] = a[tm - 8:]
        tail_ref[0] = a[tm - 2:]
    conv = a2 * cw[0:1] + a1 * cw[1:2] + a * cw[2:3] + cb_ref[...]
    y = _dot((_gelu_tanh(conv) * b).astype(BF16), wout_ref[...])
    out = x + gt_ref[0] * y
    if final_norm:
        out = _rms(out, gf_ref[...])
    o_ref[...] = out


def _ffn(x, sc, sh, gt, mod_index, lw, prefix, g_final, *, tm, tiles_per_seq, time_major, final_norm):
    n, d = x.shape
    d_ff = lw["w_ffn_out"].shape[0]
    row = lambda i: (i, 0)
    const2 = lambda i: (0, 0)
    n_tiles = n // tm
    if time_major:
        pre_spec = pl.BlockSpec(prefix.shape, lambda i: (0, 0, 0))
        tail_spec = pl.BlockSpec((1, tm, d_ff),
                                 lambda i: (jnp.maximum(i - (tiles_per_seq - 2), 0), 0, 0))
        tail_shape = jax.ShapeDtypeStruct((2, tm, d_ff), F32)
        hist = pltpu.VMEM((2, tm, d_ff), F32)
    else:
        pre_spec = pl.BlockSpec((1, 2, d_ff), lambda i: (i // tiles_per_seq, 0, 0))
        tail_spec = pl.BlockSpec((1, 2, d_ff), lambda i: (i // tiles_per_seq, 0, 0))
        tail_shape = jax.ShapeDtypeStruct((n_tiles // tiles_per_seq, 2, d_ff), F32)
        hist = pltpu.VMEM((1, 8, d_ff), F32)
    return pl.pallas_call(
        functools.partial(_ffn_body, tiles_per_seq=tiles_per_seq, time_major=time_major,
                          final_norm=final_norm, d_ff=d_ff),
        grid=(n_tiles,),
        in_specs=[pl.BlockSpec((tm, d), row)]
        + [pl.BlockSpec((1,) + a.shape[1:], mod_index) for a in (sc, sh, gt)]
        + [pl.BlockSpec((1, d), const2), pre_spec,
           pl.BlockSpec(lw["w_ffn_in"].shape, const2),
           pl.BlockSpec((CONV_W, d_ff), const2), pl.BlockSpec((1, d_ff), const2),
           pl.BlockSpec(lw["w_ffn_out"].shape, const2), pl.BlockSpec((1, d), const2)],
        out_specs=[pl.BlockSpec((tm, d), row), tail_spec],
        out_shape=[jax.ShapeDtypeStruct((n, d), F32), tail_shape],
        scratch_shapes=[hist],
        compiler_params=_cparams(1),
        name="conv_ffn",
    )(x, sc, sh, gt, lw["g_ffn"], prefix, lw["w_ffn_in"], lw["conv_w"], lw["conv_b"],
      lw["w_ffn_out"], g_final)


def _pad_heads(w, heads, dh, scale=1.0):
    k = w.shape[0]
    w = (w * scale).reshape(k, heads, dh)
    return jnp.pad(w, ((0, 0), (0, 0), (0, LANE - dh))).reshape(k, heads * LANE)


def _pad_head_rows(w, heads, dh, offset):
    n = w.shape[1]
    w = w.reshape(heads, dh, n)
    return jnp.pad(w, ((0, 0), (offset, LANE - dh - offset), (0, 0))).reshape(heads * LANE, n)


def _pad_cols(w, n=LANE):
    return jnp.pad(w, ((0, 0), (0, n - w.shape[1])))


def _prep_layer(l, p):
    d = p["w_in"].shape[1]
    w_in = p["w_in"][l]
    nsa_w = NSA_HEADS * NSA_DH
    fox_w = FOX_HEADS * FOX_DH
    q_rank = p["mla_g_q"].shape[1]
    kv_w = MLA_KV_RANK + MLA_ROPE
    splits = (nsa_w, 2 * NSA_DH, 2 * NSA_DH, 2 * NSA_DH, 3 * NSA_HEADS, q_rank, kv_w,
              fox_w, 2 * FOX_DH, FOX_HEADS, 3 * d)
    cuts = [int(c) for c in np.cumsum(splits)[:-1]]
    (w_nq, w_nc, w_ns, w_nw, w_ng, w_qd, w_kvd, w_fq, w_fkv, w_ff, w_mg) = jnp.split(w_in, cuts, axis=1)
    half = MLA_ROPE // 2
    w_kr = w_kvd[:, MLA_KV_RANK:]
    w_kr_sw = jnp.concatenate([w_kr[:, half:], w_kr[:, :half]], axis=1)
    w_t = jnp.concatenate([w_nc, w_ns, w_nw, w_fkv, w_kvd, w_kr_sw, w_ff, _pad_cols(w_ng, 16), w_qd,
                           _pad_heads(w_nq, NSA_HEADS, NSA_DH, NSA_SCALE * LOG2E),
                           _pad_heads(w_fq, FOX_HEADS, FOX_DH, FOX_SCALE * LOG2E)], axis=1)
    assert w_t.shape[1] == _T_END
    w_row = jnp.concatenate([_pad_cols(w_ns[:, :NSA_DH]), _pad_cols(w_nw[:, :NSA_DH]),
                             _pad_cols(w_fkv[:, :FOX_DH]), _pad_cols(w_kvd),
                             _pad_cols(jnp.pad(w_kr_sw, ((0, 0), (MLA_KV_RANK, 0)))),
                             w_nc, _pad_cols(w_ff)], axis=1)
    assert w_row.shape[1] == _R_END
    aug = np.zeros((FOX_HEADS, LANE), np.float32)
    for h in range(FOX_HEADS):
        for part in range(DECAY_PARTS):
            aug[h, FOX_DH + part * FOX_HEADS + h] = 1.0
    w_uq = p["mla_w_uq"][l].reshape(q_rank, MLA_HEADS, MLA_NOPE + MLA_ROPE)
    uq_nope = w_uq[:, :, :MLA_NOPE].reshape(q_rank, -1)
    uq_rope = w_uq[:, :, MLA_NOPE:]
    uq_rope_sw = jnp.concatenate([uq_rope[:, :, half:], uq_rope[:, :, :half]], axis=2)
    w_uq2 = jnp.concatenate([uq_nope, uq_rope.reshape(q_rank, -1), uq_rope_sw.reshape(q_rank, -1)],
                            axis=1)
    w_uk = p["mla_w_uk"][l]
    eye_h = np.eye(MLA_HEADS, dtype=np.float32)
    uk_bd = jnp.einsum("chd,hg->hdgc", w_uk, eye_h)
    uk_bd = jnp.pad(uk_bd, ((0, 0), (0, 0), (0, 0), (0, LANE - MLA_KV_RANK)))
    uk_bd = uk_bd.reshape(MLA_HEADS * MLA_NOPE, MLA_HEADS * LANE)
    rr = np.arange(MLA_HEADS * MLA_ROPE)
    pp = (np.arange(MLA_HEADS * LANE)[None, :]
          == ((rr // MLA_ROPE) * LANE + MLA_KV_RANK + rr % MLA_ROPE)[:, None]).astype(np.float32)
    w_uv = p["mla_w_uv"][l]
    uv_bd = jnp.einsum("chv,hg->hcgv", w_uv, eye_h)
    uv_bd = jnp.pad(uv_bd, ((0, 0), (0, LANE - MLA_KV_RANK), (0, 0), (0, 0)))
    uv_bd = uv_bd.reshape(MLA_HEADS * LANE, -1).astype(BF16)
    w_br = p["w_br"][l]
    mla_w = w_uv.shape[1] * w_uv.shape[2]
    br_n, br_f = w_br[:nsa_w], w_br[nsa_w + mla_w:]
    g_kv = p["mla_g_kv"][l]
    b_f = p["fox_b_f"][l]
    return {
        "g_attn": p["g_attn"][l][None, :], "g_ffn": p["g_ffn"][l][None, :],
        "w_row": w_row.astype(BF16), "w_t": w_t.T.astype(BF16), "g_q": p["mla_g_q"][l][:, None],
        "w_uq": w_uq2.T.astype(BF16), "w_uk": uk_bd.T.astype(BF16), "pp": pp.T.astype(BF16),
        "g_kv": g_kv[:, None], "g_kv_row": _pad_cols(g_kv[None, :]),
        "b_f": b_f[:, None], "b_f_row": _pad_cols(b_f[None, :]),
        "q_aug": jnp.asarray(aug.reshape(FOX_HEADS * LANE, 1)),
        "w_mg": w_mg.astype(BF16),
        "w_br_n": {o: _pad_head_rows(br_n, NSA_HEADS, NSA_DH, o).astype(BF16) for o in (0, NSA_DH)},
        "w_uv": uv_bd,
        "w_br_m": w_br[nsa_w:nsa_w + mla_w].astype(BF16),
        "w_br_f": {o: _pad_head_rows(br_f, FOX_HEADS, FOX_DH, o).astype(BF16) for o in (0, FOX_DH)},
        "w_o": p["w_o"][l].astype(BF16),
        "w_ffn_in": p["w_ffn_in"][l].astype(BF16), "conv_w": p["conv_w"][l],
        "conv_b": p["conv_b"][l][None, :], "w_ffn_out": p["w_ffn_out"][l].astype(BF16),
    }


def _rope_tables(pos):
    half = MLA_ROPE // 2
    inv = ROPE_BASE ** (-jnp.arange(half, dtype=F32) / half)
    ang = pos.astype(F32)[:, None] * inv[None, :]
    cos, sin = jnp.cos(ang), jnp.sin(ang)
    c1 = jnp.concatenate([cos, cos], axis=1)
    c2 = jnp.concatenate([-sin, sin], axis=1)
    padr = ((0, 0), (MLA_KV_RANK, LANE - MLA_KV_RANK - MLA_ROPE))
    return (c1.T, c2.T, jnp.tile(c1, (1, MLA_HEADS)).T, jnp.tile(c2, (1, MLA_HEADS)).T,
            jnp.pad(c1, padr), jnp.pad(c2, padr))


def _page_copy(cache_ref, layer, page, buf, slot, p, rows, sem):
    dst = buf.at[slot, pl.ds(0, rows), pl.ds(pl.multiple_of(p * LANE, LANE), LANE)]
    return pltpu.make_async_copy(cache_ref.at[layer, page], dst, sem)


def _wait_all(buf, sem):
    pltpu.make_async_copy(buf, buf, sem).wait()


def _merge_partials(parts):
    m = parts[0][0]
    for mc, _, _ in parts[1:]:
        m = jnp.maximum(m, mc)
    l = acc = None
    for mc, lc, ac in parts:
        w = jnp.exp2(mc - m)
        l = w * lc if l is None else l + w * lc
        acc = w * ac if acc is None else acc + w * ac
    return acc / l


def _softmax_parts(parts):
    m = parts[0].max(axis=-1, keepdims=True)
    for s in parts[1:]:
        m = jnp.maximum(m, s.max(axis=-1, keepdims=True))
    ps = [jnp.exp2(s - m) for s in parts]
    l = ps[0].sum(axis=-1, keepdims=True)
    for p in ps[1:]:
        l = l + p.sum(axis=-1, keepdims=True)
    return ps, l


def _paged_body(pt_ref, q_ref, knew_ref, *rest, layer, n_seq, n_pages, rows, heads, s_len, tk,
                with_decay):
    if with_decay:
        (lfn_ref, tri_s_ref, tri_ref, cache_ref, lcache_ref, o_ref, kbuf, lbuf, sem) = rest
    else:
        (cache_ref, o_ref, kbuf, sem) = rest
    b = pl.program_id(0)
    slot = b % 2
    past = n_pages * LANE
    sp = knew_ref.shape[1]

    def issue(seq, sl):
        def body(p, c):
            pg = pt_ref[seq, p]
            _page_copy(cache_ref, layer, pg, kbuf, sl, p, rows, sem.at[sl, 0]).start()
            if with_decay:
                pltpu.make_async_copy(lcache_ref.at[layer, pg], lbuf.at[sl, p], sem.at[sl, 1]).start()
            return c
        lax.fori_loop(0, n_pages, body, 0, unroll=4)

    def wait(sl):
        _wait_all(kbuf.at[sl, pl.ds(0, rows), :], sem.at[sl, 0])
        if with_decay:
            _wait_all(lbuf.at[sl], sem.at[sl, 1])

    @pl.when(b == 0)
    def _():
        if rows < LANE:
            kbuf[:, rows:, :] = jnp.zeros((2, LANE - rows, past), F32)
        issue(0, 0)

    @pl.when(b + 1 < n_seq)
    def _():
        issue(b + 1, 1 - slot)

    wait(slot)

    q = q_ref[0]
    m_rows = q.shape[0]
    knew = knew_ref[0].astype(BF16)
    s_new = _dot_nt(q, knew)
    if with_decay:
        cin = _dot_terms(lbuf[slot].reshape(n_pages * 8, LANE), tri_ref[...], 3)
        run = jnp.zeros((8, 1), F32)
        negc = []
        for p in range(n_pages):
            cp = cin[p * 8:(p + 1) * 8]
            negc.append(-(cp + run) * LOG2E)
            run = run + cp[:, LANE - 1:]
        cs_new = _dot_f32(lfn_ref[0], tri_s_ref[...])
        s_new = s_new + jnp.concatenate([-(run + cs_new) * LOG2E] * s_len, axis=0)
    sq = jnp.right_shift(lax.broadcasted_iota(jnp.int32, (m_rows, sp), 0), _log2(heads))
    jj = lax.broadcasted_iota(jnp.int32, (m_rows, sp), 1)
    s_new = jnp.where(jj <= sq, s_new, NEG)
    m = jnp.max(s_new, axis=-1, keepdims=True)
    p = jnp.exp2(s_new - m)
    parts = [(m, jnp.sum(p, axis=-1, keepdims=True), _dot(p.astype(BF16), knew))]
    ppc = tk // LANE
    for c in range(past // tk):
        kt = kbuf[slot, :, c * tk:(c + 1) * tk].astype(BF16)
        s = _dot(q, kt)
        if with_decay:
            nc = jnp.concatenate(negc[c * ppc:(c + 1) * ppc], axis=1)
            s = s + jnp.concatenate([nc] * s_len, axis=0)
        m = jnp.max(s, axis=-1, keepdims=True)
        p = jnp.exp2(s - m)
        parts.append((m, jnp.sum(p, axis=-1, keepdims=True), _dot_nt(p.astype(BF16), kt)))
    o_ref[0] = _merge_partials(parts).astype(BF16)


def _paged_attn(page_table, q, knew, cache_t, layer, *, heads, s_len, decay=None):
    n_seq, n_pages = page_table.shape
    rows = cache_t.shape[2]
    past = n_pages * LANE
    tk = past // 2
    m = q.shape[1]
    sp = knew.shape[1]
    with_decay = decay is not None
    per_b = lambda b, pt: (b, 0, 0)
    in_specs = [pl.BlockSpec((1, m, LANE), per_b), pl.BlockSpec((1, sp, LANE), per_b)]
    args = [q, knew]
    scratch = [pltpu.VMEM((2, LANE, past), F32)]
    if with_decay:
        lfn, lcache_t = decay
        tri_s = np.triu(np.ones((sp, sp), np.float32))
        tri = jnp.asarray(np.triu(np.ones((LANE, LANE), np.float32)), BF16)
        in_specs += [pl.BlockSpec((1, 8, sp), per_b), pl.BlockSpec((sp, sp), lambda b, pt: (0, 0)),
                     pl.BlockSpec((LANE, LANE), lambda b, pt: (0, 0)),
                     pl.BlockSpec(memory_space=pl.ANY), pl.BlockSpec(memory_space=pl.ANY)]
        args += [lfn, tri_s, tri, cache_t, lcache_t]
        scratch.append(pltpu.VMEM((2, n_pages, 8, LANE), F32))
    else:
        in_specs.append(pl.BlockSpec(memory_space=pl.ANY))
        args.append(cache_t)
    scratch.append(pltpu.SemaphoreType.DMA((2, 2)))
    return pl.pallas_call(
        functools.partial(_paged_body, layer=layer, n_seq=n_seq, n_pages=n_pages, rows=rows,
                          heads=heads, s_len=s_len, tk=tk, with_decay=with_decay),
        grid_spec=pltpu.PrefetchScalarGridSpec(
            num_scalar_prefetch=1, grid=(n_seq,), in_specs=in_specs,
            out_specs=pl.BlockSpec((1, m, LANE), per_b), scratch_shapes=scratch),
        out_shape=jax.ShapeDtypeStruct((n_seq, m, LANE), BF16),
        compiler_params=_cparams(1),
        name="paged_attn_decay" if with_decay else "paged_attn",
    )(page_table, *args)


def _nsa_s1_body(pt_ref, q_ref, wnew_ref, win_ref, cb_ref, wb_ref, wnb_ref, amat_ref, pair_ref,
                 cache_ref, ocmp_ref, owin_ref, idx_ref, kbuf, sem, *, layer, n_seq, n_pages, s_len,
                 k_free):
    b = pl.program_id(0)
    slot = b % 2
    past = n_pages * LANE
    hh = NSA_HEADS

    def issue(seq, sl):
        def body(p, c):
            _page_copy(cache_ref, layer, pt_ref[seq, p], kbuf, sl, p, LANE, sem.at[sl]).start()
            return c
        lax.fori_loop(0, n_pages, body, 0, unroll=4)

    @pl.when(b == 0)
    def _():
        issue(0, 0)

    @pl.when(b + 1 < n_seq)
    def _():
        issue(b + 1, 1 - slot)

    _wait_all(kbuf.at[slot], sem.at[slot])
    q = q_ref[0]
    tc = amat_ref.shape[0]
    amat = amat_ref[...]
    means = []
    for c in range(past // tc):
        means.append(_dot_terms(kbuf[slot, :, c * tc:(c + 1) * tc], amat, 2))
    kvc = jnp.concatenate(means, axis=1).astype(BF16)
    s = _dot(q, kvc) + cb_ref[...]
    e = jnp.exp2(s - jnp.max(s, axis=-1, keepdims=True))
    p = e / jnp.sum(e, axis=-1, keepdims=True)
    ocmp_ref[0] = _dot_nt(p.astype(BF16), kvc)
    ps = _dot_f32(p, pair_ref[...])
    score = ps[0:s_len]
    for h in range(1, hh):
        score = score + ps[h * s_len:(h + 1) * s_len]
    n_past = score.shape[1]
    lane = lax.broadcasted_iota(jnp.int32, score.shape, 1).astype(F32)
    work = jnp.where((lane == 0.0) | (lane == n_past - 1.0), -jnp.inf, score)
    out_lane = lax.broadcasted_iota(jnp.int32, (s_len, LANE), 1)
    idx_out = jnp.zeros((s_len, LANE), F32)
    for r in range(k_free):
        mx = jnp.max(work, axis=-1, keepdims=True)
        idx = jnp.min(jnp.where(work == mx, lane, float(n_past)), axis=-1, keepdims=True)
        idx_out = jnp.where(out_lane == r, idx, idx_out)
        work = jnp.where(lane == idx, -jnp.inf, work)
    idx_ref[0] = idx_out.astype(jnp.int32)
    wst = win_ref[0, 0].astype(BF16)
    wnew = wnew_ref[0].astype(BF16)
    (p_w, p_n), l = _softmax_parts([_dot(q, wst) + wb_ref[...], _dot_nt(q, wnew) + wnb_ref[...]])
    owin_ref[0] = (_dot_nt(p_w.astype(BF16), wst) + _dot(p_n.astype(BF16), wnew)) / l


def _nsa_s1(page_table, q, wnew, win_t, cache_t, layer, tables, *, s_len, k_free):
    n_seq, n_pages = page_table.shape
    past = n_pages * LANE
    cb, wb, wnb = tables
    m = q.shape[1]
    sp = wnew.shape[1]
    wlen = win_t.shape[3]
    tc = min(past, CMP_BLOCK * LANE)
    amat = (np.arange(tc)[:, None] // CMP_BLOCK == np.arange(tc // CMP_BLOCK)[None, :])
    amat = jnp.asarray(amat.astype(np.float32) / CMP_BLOCK, BF16)
    nc = past // CMP_BLOCK
    ratio = SEL_BLOCK // CMP_BLOCK
    pair = (np.arange(nc)[:, None] // ratio == np.arange(nc // ratio)[None, :]).astype(np.float32)
    per_b = lambda b, pt: (b, 0, 0)
    c2 = lambda b, pt: (0, 0)
    return pl.pallas_call(
        functools.partial(_nsa_s1_body, layer=layer, n_seq=n_seq, n_pages=n_pages, s_len=s_len,
                          k_free=k_free),
        grid_spec=pltpu.PrefetchScalarGridSpec(
            num_scalar_prefetch=1, grid=(n_seq,),
            in_specs=[pl.BlockSpec((1, m, LANE), per_b), pl.BlockSpec((1, sp, LANE), per_b),
                      pl.BlockSpec((1, 1, LANE, wlen), lambda b, pt: (layer, b, 0, 0)),
                      pl.BlockSpec(cb.shape, c2), pl.BlockSpec(wb.shape, c2),
                      pl.BlockSpec(wnb.shape, c2), pl.BlockSpec(amat.shape, c2),
                      pl.BlockSpec(pair.shape, c2), pl.BlockSpec(memory_space=pl.ANY)],
            out_specs=[pl.BlockSpec((1, m, LANE), per_b), pl.BlockSpec((1, m, LANE), per_b),
                       pl.BlockSpec((1, s_len, LANE), per_b)],
            scratch_shapes=[pltpu.VMEM((2, LANE, past), F32), pltpu.SemaphoreType.DMA((2,))]),
        out_shape=[jax.ShapeDtypeStruct((n_seq, m, LANE), F32),
                   jax.ShapeDtypeStruct((n_seq, m, LANE), F32),
                   jax.ShapeDtypeStruct((n_seq, s_len, LANE), jnp.int32)],
        compiler_params=_cparams(1),
        name="nsa_sample_cmp",
    )(page_table, q, wnew, win_t, cb, wb, wnb, amat, pair, cache_t)


def _nsa_s2_body(pt_ref, idx_ref, q_ref, snew_ref, g_ref, ocmp_ref, owin_ref, lb_ref, nb_ref,
                 cache_ref, o_ref, kbuf, sem, *, layer, n_seq, n_pages, s_len, k_free):
    b = pl.program_id(0)
    slot = b % 2
    n_own = s_len * k_free
    n_slots = n_own + 2
    per_page = LANE // SEL_BLOCK

    def block_of(seq, j):
        if j < n_own:
            return idx_ref[(seq * s_len + j // k_free) * LANE + j % k_free]
        return 0 if j == n_own else n_pages * per_page - 1

    def issue(seq, sl):
        for j in range(n_slots):
            blk = block_of(seq, j)
            pg = pt_ref[seq, blk >> _log2(per_page)]
            _page_copy(cache_ref, layer, pg, kbuf, sl, j, LANE, sem.at[sl]).start()

    @pl.when(b == 0)
    def _():
        issue(0, 0)

    @pl.when(b + 1 < n_seq)
    def _():
        issue(b + 1, 1 - slot)

    _wait_all(kbuf.at[slot], sem.at[slot])
    q = q_ref[0]
    m_rows = q.shape[0]
    kt = kbuf[slot].astype(BF16)
    s = _dot(q, kt)
    assert s_len & (s_len - 1) == 0
    row_s = jnp.bitwise_and(lax.broadcasted_iota(jnp.int32, (m_rows, LANE), 0), s_len - 1)
    lane_half = jnp.right_shift(lax.broadcasted_iota(jnp.int32, (m_rows, LANE), 1), _log2(SEL_BLOCK))
    bias = []
    last_page = (n_pages - 1) * per_page
    for j in range(n_slots):
        blk = block_of(b, j)
        ok = lane_half == (blk & (per_page - 1))
        if j < n_own:
            ok = ok & (row_s == j // k_free)
            near = jnp.where(blk >= last_page, lb_ref[...], 0.0)
        else:
            near = lb_ref[...] if j == n_slots - 1 else 0.0
        bias.append(jnp.where(ok, near, NEG))
    s = s + jnp.concatenate(bias, axis=1)
    snew = snew_ref[0].astype(BF16)
    (p_s, p_n), l = _softmax_parts([s, _dot_nt(q, snew) + nb_ref[...]])
    o_sel = (_dot_nt(p_s.astype(BF16), kt) + _dot(p_n.astype(BF16), snew)) / l
    g = g_ref[0]
    o = g[:, 0:1] * ocmp_ref[0] + g[:, 1:2] * o_sel + g[:, 2:3] * owin_ref[0]
    o_ref[0] = o.astype(BF16)


def _nsa_s2(page_table, idx, q, snew, gates, o_cmp, o_win, cache_t, layer, tables, *, s_len, k_free):
    n_seq, n_pages = page_table.shape
    lb, nb = tables
    m = q.shape[1]
    sp = snew.shape[1]
    n_slots = s_len * k_free + 2
    per_b = lambda b, pt, ix: (b, 0, 0)
    c2 = lambda b, pt, ix: (0, 0)
    return pl.pallas_call(
        functools.partial(_nsa_s2_body, layer=layer, n_seq=n_seq, n_pages=n_pages, s_len=s_len,
                          k_free=k_free),
        grid_spec=pltpu.PrefetchScalarGridSpec(
            num_scalar_prefetch=2, grid=(n_seq,),
            in_specs=[pl.BlockSpec((1, m, LANE), per_b), pl.BlockSpec((1, sp, LANE), per_b),
                      pl.BlockSpec((1, m, LANE), per_b), pl.BlockSpec((1, m, LANE), per_b),
                      pl.BlockSpec((1, m, LANE), per_b), pl.BlockSpec(lb.shape, c2),
                      pl.BlockSpec(nb.shape, c2), pl.BlockSpec(memory_space=pl.ANY)],
            out_specs=pl.BlockSpec((1, m, LANE), per_b),
            scratch_shapes=[pltpu.VMEM((2, LANE, n_slots * LANE), F32),
                            pltpu.SemaphoreType.DMA((2,))]),
        out_shape=jax.ShapeDtypeStruct((n_seq, m, LANE), BF16),
        compiler_params=_cparams(1),
        name="nsa_sample_sel",
    )(page_table, idx.reshape(-1), q, snew, gates, o_cmp, o_win, lb, nb, cache_t)


def _nsa_sample_tables(rel_table, past, s_len, sp, wlen):
    hh = NSA_HEADS
    far = rel_table[REL_BUCKETS - 1].astype(F32)[:, None, None]
    qpos = past + np.arange(s_len)
    rows = lambda a: a.reshape(hh * s_len, a.shape[-1])

    def bias(dist):
        bucket = np.searchsorted(np.asarray(_BUCKET_LO), np.maximum(dist, 0), side="right")
        return jnp.moveaxis(rel_table.astype(F32)[bucket], -1, 0)

    c_end = np.arange(past // CMP_BLOCK) * CMP_BLOCK + CMP_BLOCK - 1
    cb = rows(bias(qpos[:, None] - c_end[None, :]) * LOG2E)
    dist = qpos[:, None] - (past - wlen + np.arange(wlen))[None, :]
    wb = rows(jnp.where(dist < WINDOW, bias(dist) * LOG2E, NEG))
    dist_new = np.arange(s_len)[:, None] - np.arange(sp)[None, :]
    newb = bias(dist_new)
    wnb = rows(jnp.where(dist_new >= 0, newb * LOG2E, NEG))
    assert LANE >= REL_MAX_DIST
    dist = qpos[:, None] - (past - LANE + np.arange(LANE))[None, :]
    lb = rows((bias(dist) - far) * LOG2E)
    nb = rows(jnp.where(dist_new >= 0, (newb - far) * LOG2E, NEG))
    return (cb, wb, wnb), (lb, nb)


TQ = 256
TQ_FOX = 256
TK = 256
GROUP = 4
TM_PROMPT = 256
TM_FFN = 512
NEW_PAD = 16
KEY_PAD = WINDOW


def _fm_to_rows(a, mid):
    lead = a.shape[:-2]
    n = a.shape[-1]
    a = a.reshape(lead + mid + (n,))
    return jnp.moveaxis(a, -1, len(lead))


def kernel(x_prompt, x_sample, cache_nsa_cmp, cache_nsa_sel, state_nsa_win, cache_mla, cache_fox_kv, cache_fox_logf, state_ffn_conv, page_table, c_prompt, c_sample, rel_table, w_ada, b_ada, g_attn, g_ffn, w_in, mla_g_q, mla_w_uq, mla_g_kv, mla_w_uk, mla_w_uv, fox_b_f, w_br, w_o, w_ffn_in, conv_w, conv_b, w_ffn_out, g_final):
    b, t, d = x_prompt.shape
    db, s_len, _ = x_sample.shape
    depth = w_in.shape[0]
    pool = cache_nsa_cmp.shape[1]
    n_pages = page_table.shape[1]
    past = n_pages * LANE
    d_ff = w_ffn_out.shape[1]
    wlen = state_nsa_win.shape[2]
    assert cache_nsa_cmp.shape[2] == LANE and TQ >= REL_MAX_DIST and t % TK == 0
    assert t // SEL_BLOCK <= LANE and TM_PROMPT == 8 * CMP_BLOCK
    params = dict(w_in=w_in, mla_g_q=mla_g_q, mla_w_uq=mla_w_uq, mla_g_kv=mla_g_kv, mla_w_uk=mla_w_uk,
                  mla_w_uv=mla_w_uv, fox_b_f=fox_b_f, w_br=w_br, w_o=w_o, w_ffn_in=w_ffn_in,
                  conv_w=conv_w, conv_b=conv_b, w_ffn_out=w_ffn_out, g_attn=g_attn, g_ffn=g_ffn)

    n_c = b + db
    c_all = jnp.pad(jnp.concatenate([c_prompt, c_sample], axis=0), ((0, -n_c % 8), (0, 0)))
    mods = _adaln(c_all, w_ada, b_ada)

    cmp_t = jnp.transpose(cache_nsa_cmp, (0, 1, 3, 4, 2)).reshape(depth, pool, LANE, LANE)
    sel_t = jnp.transpose(cache_nsa_sel, (0, 1, 3, 4, 2)).reshape(depth, pool, LANE, LANE)
    fkv_t = jnp.transpose(cache_fox_kv, (0, 1, 3, 4, 5, 2)).reshape(depth, pool, LANE, LANE)
    mla_t = jnp.transpose(cache_mla, (0, 1, 3, 2))
    lf_t = jnp.transpose(cache_fox_logf, (0, 1, 3, 2))
    win_t = jnp.transpose(state_nsa_win, (0, 1, 3, 4, 2)).reshape(depth, db, LANE, wlen)
    conv_pre_s = jnp.transpose(state_ffn_conv, (0, 2, 1, 3))
    conv_pre_p = jnp.zeros((b, CONV_W - 1, d_ff), F32)

    rope_p = _rope_tables(jnp.arange(t))
    rope_s = _rope_tables(jnp.repeat(past + jnp.arange(s_len), db))
    tabs_p = _nsa_prompt_tables(rel_table, t, TQ)
    tabs_s1, tabs_s2 = _nsa_sample_tables(rel_table, past, s_len, NEW_PAD, wlen)
    n_sel_s = -(-(past + s_len) // SEL_BLOCK)
    k_free = min(N_SEL, n_sel_s) - 3

    tpb = t // TM_PROMPT
    tpb_ffn = t // TM_FFN
    idx_p = lambda i: (i // tpb, 0, 0)
    idx_pf = lambda i: (i // tpb_ffn, 0, 0)
    idx_s = lambda i: (0, 0, 0)
    xp = x_prompt.reshape(b * t, d)
    xs = jnp.transpose(x_sample, (1, 0, 2)).reshape(s_len * db, d)

    def new_rows(a):
        return jnp.pad(jnp.transpose(a, (2, 0, 1)), ((0, 0), (0, NEW_PAD - s_len), (0, 0)))

    def q_rows(qt, heads, width, head_major):
        q = jnp.transpose(qt.reshape(s_len, heads, LANE, db)[:, :, :width],
                          (3, 1, 0, 2) if head_major else (3, 0, 1, 2))
        return jnp.pad(q.reshape(db, s_len * heads, width), ((0, 0), (0, 0), (0, LANE - width)))

    def o_rows(o, heads, head_major):
        o = o.reshape((db, heads, s_len, LANE) if head_major else (db, s_len, heads, LANE))
        o = jnp.transpose(o, (2, 0, 1, 3) if head_major else (1, 0, 2, 3))
        return o.reshape(s_len * db, heads * LANE)

    def front_pad(a, axis):
        pads = [(0, 0)] * a.ndim
        pads[axis] = (KEY_PAD, 0)
        return jnp.pad(a, pads)

    rows_p, rows_s = [], []
    for l in range(depth):
        lw = _prep_layer(l, params)
        m6 = [mods[l][:, k * d:(k + 1) * d] for k in range(6)]
        sh_a, sc_a, gt_a, sh_f, sc_f, gt_f = [a[:b][:, None, :] for a in m6]
        sh_as, sc_as, gt_as, sh_fs, sc_fs, gt_fs = [a[b:n_c][None] for a in m6]
        last = l == depth - 1

        pre = _pre_proj(xp, sc_a, sh_a, idx_p, lw, rope_p, n_groups=b, tiles_per_group=tpb,
                        n_pos_tiles=tpb, tm=TM_PROMPT, with_keys=True)
        nc = t // CMP_BLOCK
        kvc = pre["kvc"].reshape(b, nc // 2, 2, LANE)
        kvc = jnp.transpose(kvc, (0, 2, 1, 3)).reshape(b, nc, LANE)
        kvc_k = jnp.pad(kvc[:, :, :NSA_DH], ((0, 0), (0, 0), (0, LANE - NSA_DH))).astype(BF16)
        kvct = jnp.transpose(kvc, (0, 2, 1)).astype(BF16)
        rs = lambda a: a.reshape(b, t, a.shape[-1])
        o_nsa = _nsa_prompt(pre["qnt"], kvc_k, kvct, front_pad(rs(pre["ks"]), 1),
                            front_pad(pre["vst"], 2), front_pad(rs(pre["kw"]), 1),
                            front_pad(pre["vwt"], 2), pre["gt"], tabs_p, tq=TQ, tk=TK, pad=KEY_PAD)
        o_lat = _causal_attn(pre["qmt"], rs(pre["km"]), pre["vmt"], heads=MLA_HEADS, tq=TQ, tk=TK,
                             group=GROUP)
        o_fox = _causal_attn(pre["qft"], rs(pre["kf"]), pre["vft"], heads=FOX_HEADS, tq=TQ_FOX, tk=TK,
                             group=GROUP)
        x1 = _mix(xp, sc_a, sh_a, gt_a, idx_pf, lw, o_nsa, o_lat, o_fox, tm=TM_FFN, value_lane=0)
        xp, tail_p = _ffn(x1, sc_f, sh_f, gt_f, idx_pf, lw, conv_pre_p, g_final[None, :], tm=TM_FFN,
                          tiles_per_seq=tpb_ffn, time_major=False, final_norm=last)
        wkeep = min(WINDOW, t)
        rows_p.append((_fm_to_rows(pre["ct"], (2, NSA_DH)), _fm_to_rows(pre["st"], (2, NSA_DH)),
                       _fm_to_rows(pre["wt"][:, :, t - wkeep:], (2, NSA_DH)),
                       _fm_to_rows(pre["mt"][:, :MLA_KV_RANK + MLA_ROPE], (MLA_KV_RANK + MLA_ROPE,)),
                       _fm_to_rows(pre["ft"], (2, 1, FOX_DH)), _fm_to_rows(pre["lft"], (FOX_HEADS,)),
                       tail_p))

        pre_s = _pre_proj(xs, sc_as, sh_as, idx_s, lw, rope_s, n_groups=s_len, tiles_per_group=1,
                          n_pos_tiles=s_len, tm=db, with_keys=False)
        qn_s = q_rows(pre_s["qnt"], NSA_HEADS, NSA_DH, True)
        gates_s = jnp.transpose(pre_s["gt"][:, :3 * NSA_HEADS].reshape(s_len, NSA_HEADS, 3, db),
                                (3, 1, 0, 2)).reshape(db, NSA_HEADS * s_len, 3)
        gates_s = jnp.pad(gates_s, ((0, 0), (0, 0), (0, LANE - 3)))
        o_cmp, o_win, idx = _nsa_s1(page_table, qn_s, new_rows(pre_s["wt"]), win_t, cmp_t, l, tabs_s1,
                                    s_len=s_len, k_free=k_free)
        o_nsa_s = _nsa_s2(page_table, idx, qn_s, new_rows(pre_s["st"]), gates_s, o_cmp, o_win, sel_t, l,
                          tabs_s2, s_len=s_len, k_free=k_free)
        o_lat_s = _paged_attn(page_table, q_rows(pre_s["qmt"], MLA_HEADS, LANE, False),
                              new_rows(pre_s["mt"]), mla_t, l, heads=MLA_HEADS, s_len=s_len)
        lfn = jnp.pad(jnp.transpose(pre_s["lft"], (2, 1, 0)), ((0, 0), (0, 0), (0, NEW_PAD - s_len)))
        o_fox_s = _paged_attn(page_table, q_rows(pre_s["qft"], FOX_HEADS, FOX_DH, False),
                              new_rows(pre_s["ft"]), fkv_t, l, heads=FOX_HEADS, s_len=s_len,
                              decay=(lfn, lf_t))
        x1s = _mix(xs, sc_as, sh_as, gt_as, idx_s, lw, o_rows(o_nsa_s, NSA_HEADS, True),
                   o_rows(o_lat_s, MLA_HEADS, False), o_rows(o_fox_s, FOX_HEADS, False), tm=db,
                   value_lane=NSA_DH)
        xs, tail_s = _ffn(x1s, sc_fs, sh_fs, gt_fs, idx_s, lw, conv_pre_s[l], g_final[None, :], tm=db,
                          tiles_per_seq=s_len, time_major=True, final_norm=last)
        wfull = jnp.concatenate([win_t[l], jnp.transpose(pre_s["wt"], (2, 1, 0))], axis=2)
        wfull = wfull[:, :, wfull.shape[2] - min(WINDOW, past + s_len):]
        fm_s = lambda a, mid: jnp.swapaxes(_fm_to_rows(a, mid), 0, 1)
        rows_s.append((fm_s(pre_s["ct"], (2, NSA_DH)), fm_s(pre_s["st"], (2, NSA_DH)),
                       _fm_to_rows(wfull, (2, NSA_DH)),
                       fm_s(pre_s["mt"][:, :MLA_KV_RANK + MLA_ROPE], (MLA_KV_RANK + MLA_ROPE,)),
                       fm_s(pre_s["ft"], (2, 1, FOX_DH)), fm_s(pre_s["lft"], (FOX_HEADS,)),
                       jnp.transpose(tail_s, (1, 0, 2))))

    y_prompt = xp.reshape(b, t, d)
    y_sample = jnp.transpose(xs.reshape(s_len, db, d), (1, 0, 2))
    outs_p = [jnp.stack(a) for a in zip(*rows_p)]
    outs_s = [jnp.stack(a) for a in zip(*rows_s)]
    return (y_prompt, y_sample, *outs_p, *outs_s)
```
